```python
import math
import jax, jax.numpy as jnp
from jax import lax
import numpy as np

D_MODEL = 1024
BATCH = 32
SEQ = 2048
DEPTH = 4

N_MIXERS = 3
N_LAYERS_A = (DEPTH + 2) // 3
N_LAYERS_B = (DEPTH + 1) // 3
N_LAYERS_C = DEPTH // 3

DEEPNORM_ALPHA = (2.0 * DEPTH) ** 0.25
DEEPNORM_BETA = (8.0 * DEPTH) ** -0.25
LN_EPS = 1e-5
F32 = jnp.float32

GDN_HEADS = 8
GDN_HEAD_DIM = 128
GDN_WIDTH = GDN_HEADS * GDN_HEAD_DIM
GDN_CONV = 5
GDN_CHUNK = 64
GDN_IN = 4 * GDN_WIDTH + 4 * GDN_HEADS

ATT_HEADS = 16
ATT_KV_HEADS = 4
ATT_HEAD_DIM = 64
ATT_GROUP = ATT_HEADS // ATT_KV_HEADS
ATT_WINDOW = 128
ATT_QBLOCK = 128
ATT_KSPAN = ATT_QBLOCK + 2 * ATT_WINDOW
ATT_IN = (ATT_HEADS + 2 * ATT_KV_HEADS) * ATT_HEAD_DIM
REL_BUCKETS = 32
REL_MAX_DIST = 128

POOL_WINDOWS = (2, 4, 8, 16)
POOL_GROUPS = len(POOL_WINDOWS)
POOL_GROUP_DIM = D_MODEL // POOL_GROUPS

N_EXPERTS = 16
N_EXPERT_GROUPS = 4
EXPERTS_PER_GROUP = N_EXPERTS // N_EXPERT_GROUPS
TOP_K = 2
EXPERT_DIM = 512
MOE_BLOCK = 256

kernel_name = "hybrid_gdn_swa_pool_moe_encoder"


def _layer_norm(x, g, b):
    xf = x.astype(F32)
    mu = jnp.mean(xf, -1, keepdims=True)
    xc = xf - mu
    var = jnp.mean(xc * xc, -1, keepdims=True)
    return (xc * lax.rsqrt(var + LN_EPS) * g.astype(F32) + b.astype(F32)).astype(x.dtype)


def _l2norm(t):
    tf = t.astype(F32)
    return tf * lax.rsqrt(jnp.sum(tf * tf, -1, keepdims=True) + 1e-6)


def _depthwise_conv_centred(u, w):
    k = w.shape[0]
    return lax.conv_general_dilated(
        u, w[:, None, :], window_strides=(1,), padding=[(k // 2, k // 2)],
        dimension_numbers=("NWC", "WIO", "NWC"), feature_group_count=u.shape[-1])


def _chunk_gated_delta(q, k, v, g, beta):
    b, s, h, dk = q.shape
    dv = v.shape[-1]
    c = GDN_CHUNK
    n = s // c

    def to_chunks(t):
        t = t.astype(F32).reshape((b, n, c, h) + t.shape[3:])
        return jnp.moveaxis(t, 3, 1)

    q, k, v, g, beta = (to_chunks(t) for t in (q, k, v, g, beta))
    G = jnp.cumsum(g, axis=-1)
    idx = jnp.arange(c)
    incl = idx[:, None] >= idx[None, :]
    strict = idx[:, None] > idx[None, :]
    gamma = jnp.exp(jnp.where(incl, G[..., :, None] - G[..., None, :], -jnp.inf))
    kk = jnp.einsum("bhntd,bhnsd->bhnts", k, k)
    m = jnp.where(strict, kk * gamma, 0.0) * beta[..., :, None]
    tri = m + jnp.eye(c, dtype=F32)
    rhs = jnp.concatenate([v * beta[..., None], k * (beta * jnp.exp(G))[..., None]], axis=-1)
    sol = lax.linalg.triangular_solve(tri, rhs, left_side=True, lower=True, unit_diagonal=True)
    u, w = sol[..., :dv], sol[..., dv:]
    aqk = jnp.einsum("bhntd,bhnsd->bhnts", q, k) * gamma
    qd = q * jnp.exp(G)[..., None]
    kd = k * jnp.exp(G[..., -1:] - G)[..., None]
    dend = jnp.exp(G[..., -1])

    def step(state, inp):
        u_c, w_c, aqk_c, qd_c, kd_c, d_c = inp
        v_new = u_c - jnp.einsum("bhck,bhkv->bhcv", w_c, state)
        o_c = jnp.einsum("bhck,bhkv->bhcv", qd_c, state) + jnp.einsum("bhts,bhsv->bhtv", aqk_c, v_new)
        state = state * d_c[..., None, None] + jnp.einsum("bhck,bhcv->bhkv", kd_c, v_new)
        return state, o_c

    xs = tuple(jnp.moveaxis(t, 2, 0) for t in (u, w, aqk, qd, kd, dend))
    _, o = lax.scan(step, jnp.zeros((b, h, dk, dv), F32), xs)
    return jnp.transpose(o, (1, 0, 3, 2, 4)).reshape(b, s, h, dv)


def gated_deltanet_mixer(x, w_in, conv_w, a_log, dt_bias, norm_w, w_out):
    b, s, _ = x.shape
    proj = x @ w_in
    qkv = jax.nn.silu(_depthwise_conv_centred(proj[..., :3 * GDN_WIDTH], conv_w))
    z = proj[..., 3 * GDN_WIDTH:4 * GDN_WIDTH].reshape(b, s, GDN_HEADS, GDN_HEAD_DIM)
    gates = proj[..., 4 * GDN_WIDTH:].reshape(b, s, 4, GDN_HEADS).astype(F32)
    q, k, v = (t.reshape(b, s, GDN_HEADS, GDN_HEAD_DIM) for t in jnp.split(qkv, 3, axis=-1))
    q = _l2norm(q) * GDN_HEAD_DIM ** -0.5
    k = _l2norm(k)
    decay = -jnp.exp(a_log.astype(F32)) * jax.nn.softplus(gates[:, :, 0:2] + dt_bias.astype(F32))
    beta = jax.nn.sigmoid(gates[:, :, 2:4])
    o_fwd = _chunk_gated_delta(q, k, v, decay[:, :, 0], beta[:, :, 0])
    flip = lambda t: jnp.flip(t, axis=1)
    o_bwd = flip(_chunk_gated_delta(flip(q), flip(k), flip(v), flip(decay[:, :, 1]), flip(beta[:, :, 1])))
    o = o_fwd + o_bwd
    o = o * lax.rsqrt(jnp.mean(o * o, -1, keepdims=True) + 1e-6) * norm_w.astype(F32) * jax.nn.silu(z.astype(F32))
    return o.reshape(b, s, GDN_WIDTH).astype(x.dtype) @ w_out


def _t5_bucket(rel):
    half = REL_BUCKETS // 2
    max_exact = half // 2
    n = jnp.abs(rel)
    large = max_exact + (jnp.log(jnp.maximum(n, 1).astype(F32) / max_exact)
                         / math.log(REL_MAX_DIST / max_exact) * (half - max_exact)).astype(jnp.int32)
    large = jnp.minimum(large, half - 1)
    return (rel > 0).astype(jnp.int32) * half + jnp.where(n < max_exact, n, large)


def windowed_gqa_mixer(x, w_in, sink, rel_bias, w_out):
    b, s, _ = x.shape
    qw = ATT_HEADS * ATT_HEAD_DIM
    kw = ATT_KV_HEADS * ATT_HEAD_DIM
    proj = x @ w_in
    q = (proj[..., :qw] * ATT_HEAD_DIM ** -0.5).reshape(b, s, ATT_KV_HEADS, ATT_GROUP, ATT_HEAD_DIM)
    k = proj[..., qw:qw + kw].reshape(b, s, ATT_KV_HEADS, ATT_HEAD_DIM)
    v = proj[..., qw + kw:].reshape(b, s, ATT_KV_HEADS, ATT_HEAD_DIM)
    pad = ((0, 0), (ATT_WINDOW, ATT_WINDOW), (0, 0), (0, 0))
    kp = jnp.pad(k, pad)
    vp = jnp.pad(v, pad)
    rel = jnp.arange(ATT_KSPAN)[None, :] - ATT_WINDOW - jnp.arange(ATT_QBLOCK)[:, None]
    in_window = jnp.abs(rel) <= ATT_WINDOW
    bias = jnp.transpose(rel_bias[_t5_bucket(rel)], (2, 0, 1)).astype(F32)
    bias = bias.reshape(ATT_KV_HEADS, ATT_GROUP, ATT_QBLOCK, ATT_KSPAN)
    sink_l = sink.astype(F32).reshape(ATT_KV_HEADS, ATT_GROUP)[None, :, :, None]

    def block(j):
        start = j * ATT_QBLOCK
        qj = lax.dynamic_slice_in_dim(q, start, ATT_QBLOCK, axis=1)
        kj = lax.dynamic_slice_in_dim(kp, start, ATT_KSPAN, axis=1)
        vj = lax.dynamic_slice_in_dim(vp, start, ATT_KSPAN, axis=1)
        logits = jnp.einsum("bqgrd,bkgd->bgrqk", qj, kj, preferred_element_type=F32) + bias
        kpos = start - ATT_WINDOW + jnp.arange(ATT_KSPAN)
        valid = in_window & ((kpos >= 0) & (kpos < s))[None, :]
        logits = jnp.where(valid, logits, -jnp.inf)
        mx = jnp.maximum(jnp.max(logits, -1), sink_l)
        p = jnp.exp(logits - mx[..., None])
        den = jnp.sum(p, -1) + jnp.exp(sink_l - mx)
        o = jnp.einsum("bgrqk,bkgd->bqgrd", p / den[..., None], vj.astype(F32))
        return o.reshape(b, ATT_QBLOCK, qw).astype(x.dtype)

    o = lax.map(block, jnp.arange(s // ATT_QBLOCK))
    o = jnp.moveaxis(o, 0, 1).reshape(b, s, qw)
    return o @ w_out


def multiscale_pool_mixer(x, w_in, w_group, scale, w_out):
    b, s, _ = x.shape
    u = (x @ w_in).astype(F32).reshape(b, s, POOL_GROUPS, POOL_GROUP_DIM)
    csum = jnp.pad(jnp.cumsum(u, axis=1), ((0, 0), (1, 0), (0, 0), (0, 0)))
    pos = jnp.arange(s)
    mixed = []
    for gi, win in enumerate(POOL_WINDOWS):
        lo = jnp.clip(pos - win // 2, 0, s)
        hi = jnp.clip(pos + win - win // 2, 0, s)
        cs = csum[:, :, gi]
        mean = (jnp.take(cs, hi, axis=1) - jnp.take(cs, lo, axis=1)) / (hi - lo).astype(F32)[None, :, None]
        mixed.append(mean - u[:, :, gi])
    m = jnp.stack(mixed, axis=2)
    y = jnp.einsum("bsgc,gcd->bsgd", m, w_group.astype(F32)).reshape(b, s, D_MODEL) * scale.astype(F32)
    return y.astype(x.dtype) @ w_out


def grouped_moe(x, router_w, router_b, w_gate, w_up, w_down):
    b, s, d = x.shape
    t = b * s
    xf = x.reshape(t, d)
    scores = jax.nn.sigmoid(jnp.dot(xf, router_w, preferred_element_type=F32))
    biased = (scores + router_b.astype(F32)).reshape(t, N_EXPERT_GROUPS, EXPERTS_PER_GROUP)
    group_score = jnp.sum(lax.top_k(biased, TOP_K)[0], axis=-1)
    group = jnp.argmax(group_score, axis=-1)
    in_group = biased[jnp.arange(t), group]
    _, local = lax.top_k(in_group, TOP_K)
    expert = group[:, None] * EXPERTS_PER_GROUP + local
    gate = jnp.take_along_axis(scores, expert, axis=1)
    gate = gate / jnp.sum(gate, -1, keepdims=True)

    n_assign = t * TOP_K
    e_flat = expert.reshape(n_assign)
    order = jnp.argsort(e_flat)
    e_sorted = e_flat[order]
    tok_sorted = order // TOP_K
    gate_sorted = gate.reshape(n_assign)[order]
    counts = jnp.zeros((N_EXPERTS,), jnp.int32).at[e_flat].add(1)
    starts = jnp.cumsum(counts) - counts
    padded = (counts + MOE_BLOCK - 1) // MOE_BLOCK * MOE_BLOCK
    pends = jnp.cumsum(padded)
    pstarts = pends - padded
    dest = pstarts[e_sorted] + jnp.arange(n_assign) - starts[e_sorted]
    n_blocks = -(-n_assign // MOE_BLOCK) + N_EXPERTS
    row_tok = jnp.full((n_blocks * MOE_BLOCK,), t, jnp.int32).at[dest].set(tok_sorted)
    block_expert = jnp.minimum(
        jnp.sum((jnp.arange(n_blocks) * MOE_BLOCK)[:, None] >= pends[None, :], -1), N_EXPERTS - 1)
    x_rows = jnp.concatenate([xf, jnp.zeros((1, d), xf.dtype)])[row_tok].reshape(n_blocks, MOE_BLOCK, d)

    def expert_block(args):
        xr, e = args
        hdn = jax.nn.silu(xr @ w_gate[e]) * (xr @ w_up[e])
        return hdn @ w_down[e]

    y_rows = lax.map(expert_block, (x_rows, block_expert)).reshape(n_blocks * MOE_BLOCK, d)
    contrib = y_rows[dest] * gate_sorted[:, None].astype(x.dtype)
    out = jax.ops.segment_sum(contrib, tok_sorted, num_segments=t)
    return out.reshape(b, s, d)


def setup_inputs(seed: int = 0) -> dict:
    key = jax.random.key(seed)
    ks = jax.random.split(key, 24)
    nrm = lambda k, shape, sc: jax.random.normal(k, shape, jnp.float32) * sc
    x = nrm(ks[0], (BATCH, SEQ, D_MODEL), 1.0)
    a_w_in = nrm(ks[1], (N_LAYERS_A, D_MODEL, GDN_IN), D_MODEL ** -0.5)
    a_conv = nrm(ks[2], (N_LAYERS_A, GDN_CONV, 3 * GDN_WIDTH), GDN_CONV ** -0.5)
    a_A_log = jnp.log(jax.random.uniform(ks[3], (N_LAYERS_A, 2, GDN_HEADS), jnp.float32, 1.0, 16.0))
    dt = jnp.exp(jax.random.uniform(ks[4], (N_LAYERS_A, 2, GDN_HEADS), jnp.float32,
                                    math.log(1e-3), math.log(1e-1)))
    a_dt_bias = dt + jnp.log(-jnp.expm1(-dt))
    a_norm_w = 1.0 + nrm(ks[5], (N_LAYERS_A, GDN_HEAD_DIM), 0.02)
    a_w_out = nrm(ks[6], (N_LAYERS_A, GDN_WIDTH, D_MODEL), GDN_WIDTH ** -0.5 * DEEPNORM_BETA)
    b_w_in = nrm(ks[7], (N_LAYERS_B, D_MODEL, ATT_IN), D_MODEL ** -0.5)
    b_sink = nrm(ks[8], (N_LAYERS_B, ATT_HEADS), 1.0)
    b_w_out = nrm(ks[9], (N_LAYERS_B, ATT_HEADS * ATT_HEAD_DIM, D_MODEL),
                  (ATT_HEADS * ATT_HEAD_DIM) ** -0.5 * DEEPNORM_BETA)
    rel_bias = nrm(ks[10], (REL_BUCKETS, ATT_HEADS), 0.5)
    c_w_in = nrm(ks[11], (N_LAYERS_C, D_MODEL, D_MODEL), D_MODEL ** -0.5)
    c_w_group = nrm(ks[12], (N_LAYERS_C, POOL_GROUPS, POOL_GROUP_DIM, POOL_GROUP_DIM), POOL_GROUP_DIM ** -0.5)
    c_scale = 1.0 + nrm(ks[13], (N_LAYERS_C, D_MODEL), 0.1)
    c_w_out = nrm(ks[14], (N_LAYERS_C, D_MODEL, D_MODEL), D_MODEL ** -0.5 * DEEPNORM_BETA)
    router_w = nrm(ks[15], (D_MODEL, N_EXPERTS), D_MODEL ** -0.5)
    router_b = nrm(ks[16], (N_EXPERTS,), 0.01)
    moe_w_gate = nrm(ks[17], (DEPTH, N_EXPERTS, D_MODEL, EXPERT_DIM), D_MODEL ** -0.5)
    moe_w_up = nrm(ks[18], (DEPTH, N_EXPERTS, D_MODEL, EXPERT_DIM), D_MODEL ** -0.5)
    moe_w_down = nrm(ks[19], (DEPTH, N_EXPERTS, EXPERT_DIM, D_MODEL), EXPERT_DIM ** -0.5 * DEEPNORM_BETA)
    ln_g = 1.0 + nrm(ks[20], (DEPTH, 2, D_MODEL), 0.02)
    ln_b = nrm(ks[21], (DEPTH, 2, D_MODEL), 0.02)
    return {"x": x, "a_w_in": a_w_in, "a_conv": a_conv, "a_A_log": a_A_log, "a_dt_bias": a_dt_bias,
            "a_norm_w": a_norm_w, "a_w_out": a_w_out, "b_w_in": b_w_in, "b_sink": b_sink,
            "b_w_out": b_w_out, "rel_bias": rel_bias, "c_w_in": c_w_in, "c_w_group": c_w_group,
            "c_scale": c_scale, "c_w_out": c_w_out, "router_w": router_w, "router_b": router_b,
            "moe_w_gate": moe_w_gate, "moe_w_up": moe_w_up, "moe_w_down": moe_w_down,
            "ln_g": ln_g, "ln_b": ln_b}


def reference(x, a_w_in, a_conv, a_A_log, a_dt_bias, a_norm_w, a_w_out, b_w_in, b_sink, b_w_out,
              rel_bias, c_w_in, c_w_group, c_scale, c_w_out, router_w, router_b,
              moe_w_gate, moe_w_up, moe_w_down, ln_g, ln_b):
    for i in range(DEPTH):
        kind, j = i % N_MIXERS, i // N_MIXERS
        if kind == 0:
            h = gated_deltanet_mixer(x, a_w_in[j], a_conv[j], a_A_log[j], a_dt_bias[j], a_norm_w[j], a_w_out[j])
        elif kind == 1:
            h = windowed_gqa_mixer(x, b_w_in[j], b_sink[j], rel_bias, b_w_out[j])
        else:
            h = multiscale_pool_mixer(x, c_w_in[j], c_w_group[j], c_scale[j], c_w_out[j])
        x = _layer_norm(DEEPNORM_ALPHA * x + h, ln_g[i, 0], ln_b[i, 0])
        f = grouped_moe(x, router_w, router_b, moe_w_gate[i], moe_w_up[i], moe_w_down[i])
        x = _layer_norm(DEEPNORM_ALPHA * x + f, ln_g[i, 1], ln_b[i, 1])
    return x
```

```python
import functools
import math

import jax
import jax.numpy as jnp
from jax import lax
from jax.experimental import pallas as pl
from jax.experimental.pallas import tpu as pltpu

F32 = jnp.float32
BF16 = jnp.bfloat16
I32 = jnp.int32

D_MODEL = 1024
DEPTH = 4
N_MIXERS = 3
DEEPNORM_ALPHA = (2.0 * DEPTH) ** 0.25
LN_EPS = 1e-5

GDN_HEADS = 8
GDN_HEAD_DIM = 128
GDN_WIDTH = GDN_HEADS * GDN_HEAD_DIM
GDN_CONV = 5
GDN_CHUNK = 64

ATT_HEADS = 16
ATT_KV_HEADS = 4
ATT_HEAD_DIM = 64
ATT_GROUP = ATT_HEADS // ATT_KV_HEADS
ATT_WINDOW = 128
ATT_QBLOCK = 128
ATT_KSPAN = ATT_QBLOCK + 2 * ATT_WINDOW
REL_BUCKETS = 32
REL_MAX_DIST = 128

POOL_WINDOWS = (2, 4, 8, 16)
POOL_GROUP_DIM = D_MODEL // len(POOL_WINDOWS)

N_EXPERTS = 16
N_EXPERT_GROUPS = 4
EXPERTS_PER_GROUP = N_EXPERTS // N_EXPERT_GROUPS
TOP_K = 2
EXPERT_DIM = 512

V7X_LANES = 128
V7X_SUBLANES = 8
V7X_VMEM_LIMIT_BYTES = 56 * 1024 * 1024

ROW_TILE = 512
MOE_BLOCK = 512
NEG_BIG = -1e30

HIGHEST = lax.Precision.HIGHEST


def _cparams(*sem):
    return pltpu.CompilerParams(dimension_semantics=tuple(sem), vmem_limit_bytes=V7X_VMEM_LIMIT_BYTES)


def _dot(a, b, **kw):
    return jnp.dot(a, b, preferred_element_type=F32, **kw)


def _dot_nt(a, b, **kw):
    return lax.dot_general(a, b, (((1,), (1,)), ((), ())), preferred_element_type=F32, **kw)


def _dot_tn(a, b, **kw):
    return lax.dot_general(a, b, (((0,), (0,)), ((), ())), preferred_element_type=F32, **kw)


def _silu(x):
    return x * jax.nn.sigmoid(x)


def _linear_kernel(x_ref, w_ref, o_ref):
    o_ref[...] = _dot(x_ref[...].astype(BF16), w_ref[...]).astype(o_ref.dtype)


def _linear(x, w_bf16, out_dtype, tm=ROW_TILE):
    m, k = x.shape
    n = w_bf16.shape[1]
    return pl.pallas_call(
        _linear_kernel,
        out_shape=jax.ShapeDtypeStruct((m, n), out_dtype),
        grid=(m // tm,),
        in_specs=[pl.BlockSpec((tm, k), lambda i: (i, 0)), pl.BlockSpec((k, n), lambda i: (0, 0))],
        out_specs=pl.BlockSpec((tm, n), lambda i: (i, 0)),
        compiler_params=_cparams("parallel"),
        name="linear",
    )(x, w_bf16)


def _layer_norm_rows(s, g, b):
    mu = jnp.mean(s, axis=-1, keepdims=True)
    xc = s - mu
    var = jnp.mean(xc * xc, axis=-1, keepdims=True)
    return xc * lax.rsqrt(var + LN_EPS) * g + b


def _top2_of4(a, b, c, d):
    hi1, lo1 = jnp.maximum(a, b), jnp.minimum(a, b)
    hi2, lo2 = jnp.maximum(c, d), jnp.minimum(c, d)
    return jnp.maximum(hi1, hi2) + jnp.maximum(jnp.minimum(hi1, hi2), jnp.maximum(lo1, lo2))


def _route_rows(scores, biased):
    bi = [biased[e:e + 1, :] for e in range(N_EXPERTS)]
    sc = [scores[e:e + 1, :] for e in range(N_EXPERTS)]
    gs = [_top2_of4(*bi[4 * g:4 * g + 4]) for g in range(N_EXPERT_GROUPS)]
    group = jnp.zeros_like(gs[0], dtype=I32)
    best = gs[0]
    for g in range(1, N_EXPERT_GROUPS):
        upd = gs[g] > best
        group = jnp.where(upd, g, group)
        best = jnp.where(upd, gs[g], best)

    def pick(rows, j):
        out = rows[j]
        for g in range(1, N_EXPERT_GROUPS):
            out = jnp.where(group == g, rows[4 * g + j], out)
        return out

    v = [pick(bi, j) for j in range(EXPERTS_PER_GROUP)]
    s = [pick(sc, j) for j in range(EXPERTS_PER_GROUP)]
    i1 = jnp.zeros_like(group)
    b1 = v[0]
    for j in range(1, EXPERTS_PER_GROUP):
        upd = v[j] > b1
        i1 = jnp.where(upd, j, i1)
        b1 = jnp.where(upd, v[j], b1)
    i2 = jnp.full_like(group, -1)
    b2 = jnp.full_like(b1, -jnp.inf)
    for j in range(EXPERTS_PER_GROUP):
        upd = (i1 != j) & ((v[j] > b2) | (i2 < 0))
        i2 = jnp.where(upd, j, i2)
        b2 = jnp.where(upd, v[j], b2)
    s1 = s[0]
    s2 = s[0]
    for j in range(1, EXPERTS_PER_GROUP):
        s1 = jnp.where(i1 == j, s[j], s1)
        s2 = jnp.where(i2 == j, s[j], s2)
    den = s1 + s2
    return group * EXPERTS_PER_GROUP + i1, group * EXPERTS_PER_GROUP + i2, s1 / den, s2 / den


def _post_mixer_kernel(h_ref, w_ref, x_ref, g_ref, b_ref, rwt_ref, rb_ref, ustrict_ref, eye_ref,
                       x1_ref, ri_ref, gc_ref, cnt_ref, base_ref):
    tm = x_ref.shape[0]

    @pl.when(pl.program_id(0) == 0)
    def _():
        base_ref[...] = jnp.zeros_like(base_ref)

    y = _dot(h_ref[...], w_ref[...])
    x1 = _layer_norm_rows(DEEPNORM_ALPHA * x_ref[...] + y, g_ref[...], b_ref[...])
    x1_ref[...] = x1

    logits = _dot_nt(rwt_ref[...], x1, precision=HIGHEST)
    scores = jax.nn.sigmoid(logits)
    e1, e2, g1, g2 = _route_rows(scores, scores + rb_ref[...])

    eidx = lax.broadcasted_iota(I32, (N_EXPERTS, tm), 0)
    hit1 = eidx == e1
    hit2 = eidx == e2
    onehot = jnp.where(hit1 | hit2, 1.0, 0.0)
    base = base_ref[:, 0:1]
    before = _dot(onehot.astype(BF16), ustrict_ref[...]) + base
    r1 = jnp.sum(jnp.where(hit1, before, 0.0), axis=0, keepdims=True)
    r2 = jnp.sum(jnp.where(hit2, before, 0.0), axis=0, keepdims=True)
    new_base = base + jnp.sum(onehot, axis=1, keepdims=True)
    base_ref[...] = jnp.broadcast_to(new_base, base_ref.shape)
    cnt_ref[...] = jnp.broadcast_to(new_base, cnt_ref.shape)

    ri_ref[...] = jnp.concatenate(
        [e1, e2, r1.astype(I32), r2.astype(I32), jnp.zeros((4, tm), I32)], axis=0)
    gates = jnp.concatenate([g1, g2, jnp.zeros((6, tm), F32)], axis=0)
    gc_ref[...] = _dot_nt(eye_ref[...], gates, precision=HIGHEST)


def _post_mixer(h_bf16, w_out_bf16, x, ln_g, ln_b, router_wt, router_b_col, tm=ROW_TILE):
    t, d = x.shape
    ustrict = (jnp.arange(tm)[:, None] < jnp.arange(tm)[None, :]).astype(BF16)
    eye = jnp.eye(tm, dtype=F32)
    full = lambda shape: pl.BlockSpec(shape, lambda i: (0,) * len(shape))
    return pl.pallas_call(
        _post_mixer_kernel,
        out_shape=(jax.ShapeDtypeStruct((t, d), F32),
                   jax.ShapeDtypeStruct((8, t), I32),
                   jax.ShapeDtypeStruct((t, 8), F32),
                   jax.ShapeDtypeStruct((N_EXPERTS, V7X_LANES), F32)),
        grid=(t // tm,),
        in_specs=[pl.BlockSpec((tm, h_bf16.shape[1]), lambda i: (i, 0)),
                  full(w_out_bf16.shape),
                  pl.BlockSpec((tm, d), lambda i: (i, 0)),
                  full((1, d)), full((1, d)),
                  full((N_EXPERTS, d)), full((N_EXPERTS, 1)),
                  full((tm, tm)), full((tm, tm))],
        out_specs=(pl.BlockSpec((tm, d), lambda i: (i, 0)),
                   pl.BlockSpec((8, tm), lambda i: (0, i)),
                   pl.BlockSpec((tm, 8), lambda i: (i, 0)),
                   pl.BlockSpec((N_EXPERTS, V7X_LANES), lambda i: (0, 0))),
        scratch_shapes=[pltpu.VMEM((N_EXPERTS, V7X_LANES), F32)],
        compiler_params=_cparams("arbitrary"),
        name="post_mixer_router",
    )(h_bf16, w_out_bf16, x, ln_g.reshape(1, d), ln_b.reshape(1, d), router_wt, router_b_col, ustrict, eye)


def _row_copy(src_ref, src_row, dst_ref, dst_row, sem):
    return pltpu.make_async_copy(src_ref.at[pl.ds(src_row, 1), :], dst_ref.at[pl.ds(dst_row, 1), :], sem)


def _dispatch_kernel(x_ref, dest_hbm, xs_in_hbm, xs_hbm, idx_smem, idx_sem, row_sem):
    del xs_in_hbm
    tm = x_ref.shape[0]
    i = pl.program_id(0)
    idx_copy = pltpu.make_async_copy(dest_hbm.at[i], idx_smem, idx_sem)
    idx_copy.start()
    idx_copy.wait()

    def issue(j, carry):
        for k in range(TOP_K):
            _row_copy(x_ref, j, xs_hbm, idx_smem[k, j], row_sem).start()
        return carry

    lax.fori_loop(0, tm, issue, 0, unroll=8)

    def drain(j, carry):
        for k in range(TOP_K):
            _row_copy(x_ref, j, xs_hbm, idx_smem[k, j], row_sem).wait()
        return carry

    lax.fori_loop(0, tm, drain, 0, unroll=8)


def _dispatch(x1, dest_tiles, n_rows, tm=ROW_TILE):
    t, d = x1.shape
    xs0 = jnp.zeros((n_rows, d), F32)
    return pl.pallas_call(
        _dispatch_kernel,
        out_shape=jax.ShapeDtypeStruct((n_rows, d), F32),
        grid=(t // tm,),
        in_specs=[pl.BlockSpec((tm, d), lambda i: (i, 0)),
                  pl.BlockSpec(memory_space=pl.ANY),
                  pl.BlockSpec(memory_space=pl.ANY)],
        out_specs=pl.BlockSpec(memory_space=pl.ANY),
        scratch_shapes=[pltpu.SMEM((TOP_K, tm), I32), pltpu.SemaphoreType.DMA, pltpu.SemaphoreType.DMA],
        input_output_aliases={2: 0},
        compiler_params=_cparams("arbitrary"),
        name="moe_dispatch",
    )(x1, dest_tiles, xs0)


def _expert_kernel(be_ref, xs_ref, wg_ref, wu_ref, wd_ref, ys_ref):
    del be_ref
    x = xs_ref[...].astype(BF16)
    hidden = _silu(_dot(x, wg_ref[0].astype(BF16))) * _dot(x, wu_ref[0].astype(BF16))
    ys_ref[...] = _dot(hidden.astype(BF16), wd_ref[0].astype(BF16))


def _experts(block_expert, xs, wg, wu, wd):
    n_rows, d = xs.shape
    n_blocks = n_rows // MOE_BLOCK
    f = wg.shape[2]
    grid_spec = pltpu.PrefetchScalarGridSpec(
        num_scalar_prefetch=1,
        grid=(n_blocks,),
        in_specs=[pl.BlockSpec((MOE_BLOCK, d), lambda i, be: (i, 0)),
                  pl.BlockSpec((1, d, f), lambda i, be: (be[i], 0, 0)),
                  pl.BlockSpec((1, d, f), lambda i, be: (be[i], 0, 0)),
                  pl.BlockSpec((1, f, d), lambda i, be: (be[i], 0, 0))],
        out_specs=pl.BlockSpec((MOE_BLOCK, d), lambda i, be: (i, 0)),
    )
    return pl.pallas_call(
        _expert_kernel,
        out_shape=jax.ShapeDtypeStruct((n_rows, d), F32),
        grid_spec=grid_spec,
        compiler_params=_cparams("parallel"),
        name="moe_experts",
    )(block_expert, xs, wg, wu, wd)


def _combine_kernel(x1_ref, gc_ref, g_ref, b_ref, dest_hbm, ys_hbm, o_ref, idx_smem, ybuf, idx_sem, row_sem):
    tm = x1_ref.shape[0]
    i = pl.program_id(0)
    idx_copy = pltpu.make_async_copy(dest_hbm.at[i], idx_smem, idx_sem)
    idx_copy.start()
    idx_copy.wait()

    def issue(j, carry):
        for k in range(TOP_K):
            _row_copy(ys_hbm, idx_smem[k, j], ybuf.at[k], j, row_sem).start()
        return carry

    lax.fori_loop(0, tm, issue, 0, unroll=8)

    def drain(j, carry):
        for k in range(TOP_K):
            _row_copy(ys_hbm, idx_smem[k, j], ybuf.at[k], j, row_sem).wait()
        return carry

    lax.fori_loop(0, tm, drain, 0, unroll=8)

    gc = gc_ref[...]
    f = ybuf[0] * gc[:, 0:1] + ybuf[1] * gc[:, 1:2]
    o_ref[...] = _layer_norm_rows(DEEPNORM_ALPHA * x1_ref[...] + f, g_ref[...], b_ref[...])


def _combine(x1, gate_cols, ln_g, ln_b, dest_tiles, ys, tm=ROW_TILE):
    t, d = x1.shape
    full = lambda shape: pl.BlockSpec(shape, lambda i: (0,) * len(shape))
    return pl.pallas_call(
        _combine_kernel,
        out_shape=jax.ShapeDtypeStruct((t, d), F32),
        grid=(t // tm,),
        in_specs=[pl.BlockSpec((tm, d), lambda i: (i, 0)),
                  pl.BlockSpec((tm, 8), lambda i: (i, 0)),
                  full((1, d)), full((1, d)),
                  pl.BlockSpec(memory_space=pl.ANY),
                  pl.BlockSpec(memory_space=pl.ANY)],
        out_specs=pl.BlockSpec((tm, d), lambda i: (i, 0)),
        scratch_shapes=[pltpu.SMEM((TOP_K, tm), I32), pltpu.VMEM((TOP_K, tm, d), F32),
                        pltpu.SemaphoreType.DMA, pltpu.SemaphoreType.DMA],
        compiler_params=_cparams("arbitrary"),
        name="moe_combine_ln",
    )(x1, gate_cols, ln_g.reshape(1, d), ln_b.reshape(1, d), dest_tiles, ys)


def _moe_layer(x1, route_i, gate_cols, counts, wg, wu, wd, ln_g, ln_b, tm=ROW_TILE):
    t, d = x1.shape
    n_tiles = t // tm
    n_rows = t * TOP_K + N_EXPERTS * MOE_BLOCK
    counts = counts.astype(I32)
    padded = (counts + MOE_BLOCK - 1) // MOE_BLOCK * MOE_BLOCK
    pends = jnp.cumsum(padded)
    pstarts = pends - padded
    eids = route_i[0:2]
    start_of = jnp.sum(jnp.where(eids[None] == jnp.arange(N_EXPERTS, dtype=I32)[:, None, None],
                                 pstarts[:, None, None], 0), axis=0)
    dest = start_of + route_i[2:4]
    dest_tiles = dest.reshape(TOP_K, n_tiles, tm).transpose(1, 0, 2)
    n_blocks = n_rows // MOE_BLOCK
    block_expert = jnp.minimum(
        jnp.sum((jnp.arange(n_blocks, dtype=I32) * MOE_BLOCK)[:, None] >= pends[None, :], axis=-1),
        N_EXPERTS - 1).astype(I32)
    xs = _dispatch(x1, dest_tiles, n_rows, tm)
    ys = _experts(block_expert, xs, wg, wu, wd)
    return _combine(x1, gate_cols, ln_g, ln_b, dest_tiles, ys, tm)


POOL_PAD = 16


def _pool_kernel(u_ref, wg_ref, scale_ref, o_ref, pad_ref):
    s, c = u_ref.shape[1], u_ref.shape[2]
    u = u_ref[0]
    zeros = jnp.zeros((POOL_PAD, c), F32)
    pad_ref[pl.ds(0, POOL_PAD), :] = zeros
    pad_ref[pl.ds(POOL_PAD + s, POOL_PAD), :] = zeros
    pad_ref[pl.ds(POOL_PAD, s), :] = u
    pos = lax.broadcasted_iota(I32, (s, c), 0)
    group = pl.program_id(1)
    for gi, win in enumerate(POOL_WINDOWS):
        @pl.when(group == gi)
        def _(win=win):
            half = win // 2
            total = pad_ref[pl.ds(POOL_PAD - half, s), :]
            for j in range(1 - half, half):
                total = total + pad_ref[pl.ds(POOL_PAD + j, s), :]
            count = (jnp.minimum(pos + half, s) - jnp.maximum(pos - half, 0)).astype(F32)
            mixed = total / count - u
            y = _dot(mixed.astype(BF16), wg_ref[0]) * scale_ref[...]
            o_ref[0] = y.astype(o_ref.dtype)


def _pool_core(u, w_group_bf16, scale):
    b, s, d = u.shape
    c = POOL_GROUP_DIM
    return pl.pallas_call(
        _pool_kernel,
        out_shape=jax.ShapeDtypeStruct((b, s, d), BF16),
        grid=(b, d // c),
        in_specs=[pl.BlockSpec((1, s, c), lambda i, g: (i, 0, g)),
                  pl.BlockSpec((1, c, c), lambda i, g: (g, 0, 0)),
                  pl.BlockSpec((1, c), lambda i, g: (0, g))],
        out_specs=pl.BlockSpec((1, s, c), lambda i, g: (i, 0, g)),
        scratch_shapes=[pltpu.VMEM((s + 2 * POOL_PAD, c), F32)],
        compiler_params=_cparams("parallel", "parallel"),
        name="pool_core",
    )(u, w_group_bf16, scale.reshape(1, d))


def _t5_bucket(rel):
    half = REL_BUCKETS // 2
    max_exact = half // 2
    n = jnp.abs(rel)
    large = max_exact + (jnp.log(jnp.maximum(n, 1).astype(F32) / max_exact)
                         / math.log(REL_MAX_DIST / max_exact) * (half - max_exact)).astype(I32)
    large = jnp.minimum(large, half - 1)
    return (rel > 0).astype(I32) * half + jnp.where(n < max_exact, n, large)


def _attn_bias_table(rel_bias):
    rel = jnp.arange(ATT_KSPAN)[None, :] - ATT_WINDOW - jnp.arange(ATT_QBLOCK)[:, None]
    bias = jnp.transpose(rel_bias[_t5_bucket(rel)], (2, 0, 1)).astype(F32)
    return jnp.where((jnp.abs(rel) <= ATT_WINDOW)[None], bias, NEG_BIG)


def _attn_kernel(q_ref, kp_ref, kc_ref, kn_ref, vp_ref, vc_ref, vn_ref, bias_ref, sink_ref, o_ref):
    j = pl.program_id(1)
    nb = pl.num_programs(1)
    qb = ATT_QBLOCK
    col = lax.broadcasted_iota(I32, (qb, ATT_KSPAN), 1)
    edge = jnp.where(((j == 0) & (col < qb)) | ((j == nb - 1) & (col >= 2 * qb)), NEG_BIG, 0.0)
    k_all = jnp.concatenate([kp_ref[...], kc_ref[...], kn_ref[...]], axis=0)
    v_all = jnp.concatenate([vp_ref[...], vc_ref[...], vn_ref[...]], axis=0)
    q = q_ref[...]
    outs = []
    for g in range(ATT_KV_HEADS):
        k_g = k_all[:, g * ATT_HEAD_DIM:(g + 1) * ATT_HEAD_DIM]
        v_g = v_all[:, g * ATT_HEAD_DIM:(g + 1) * ATT_HEAD_DIM]
        for r in range(ATT_GROUP):
            hd = g * ATT_GROUP + r
            q_h = q[:, hd * ATT_HEAD_DIM:(hd + 1) * ATT_HEAD_DIM]
            logits = _dot_nt(q_h, k_g) * (ATT_HEAD_DIM ** -0.5) + bias_ref[hd] + edge
            sink = sink_ref[hd]
            mx = jnp.maximum(jnp.max(logits, axis=-1, keepdims=True), sink)
            p = jnp.exp(logits - mx)
            den = jnp.sum(p, axis=-1, keepdims=True) + jnp.exp(sink - mx)
            outs.append(_dot(p.astype(BF16), v_g) / den)
    o_ref[...] = jnp.concatenate(outs, axis=-1).astype(o_ref.dtype)


def _attn_core(proj, bias_tbl, sink, b, s):
    qw = ATT_HEADS * ATT_HEAD_DIM
    kw = ATT_KV_HEADS * ATT_HEAD_DIM
    nb = s // ATT_QBLOCK
    kcol, vcol = qw // kw, qw // kw + 1
    row = lambda i, j: i * nb + j
    prev = lambda i, j: i * nb + jnp.maximum(j - 1, 0)
    nxt = lambda i, j: i * nb + jnp.minimum(j + 1, nb - 1)
    kv_spec = lambda rowf, c: pl.BlockSpec((ATT_QBLOCK, kw), lambda i, j: (rowf(i, j), c))
    return pl.pallas_call(
        _attn_kernel,
        out_shape=jax.ShapeDtypeStruct((b * s, qw), BF16),
        grid=(b, nb),
        in_specs=[pl.BlockSpec((ATT_QBLOCK, qw), lambda i, j: (row(i, j), 0)),
                  kv_spec(prev, kcol), kv_spec(row, kcol), kv_spec(nxt, kcol),
                  kv_spec(prev, vcol), kv_spec(row, vcol), kv_spec(nxt, vcol),
                  pl.BlockSpec((ATT_HEADS, ATT_QBLOCK, ATT_KSPAN), lambda i, j: (0, 0, 0)),
                  pl.BlockSpec(memory_space=pltpu.SMEM)],
        out_specs=pl.BlockSpec((ATT_QBLOCK, qw), lambda i, j: (row(i, j), 0)),
        compiler_params=_cparams("parallel", "parallel"),
        name="attn_core",
    )(proj, proj, proj, proj, proj, proj, proj, bias_tbl, sink.astype(F32))


GDN_GATE_LANES = V7X_LANES
GDN_CONV_PAD = 8


def _softplus(x):
    return jnp.maximum(x, 0.0) + jnp.log(1.0 + jnp.exp(-jnp.abs(x)))


def _gdn_gates_kernel(x_ref, w_ref, alog_ref, dtb_ref, o_ref):
    s = x_ref.shape[0]
    c = GDN_CHUNK
    gl = _dot(x_ref[...].astype(BF16), w_ref[...])
    decay = -jnp.exp(alog_ref[...]) * _softplus(gl + dtb_ref[...])
    beta = jax.nn.sigmoid(gl)
    r = lax.broadcasted_iota(I32, (c, c), 0)
    q = lax.broadcasted_iota(I32, (c, c), 1)
    lower = jnp.where(r >= q, 1.0, 0.0)
    upper = jnp.where(r <= q, 1.0, 0.0)
    lane = lax.broadcasted_iota(I32, (c, GDN_GATE_LANES), 1)
    for n in range(s // c):
        d_c = decay[n * c:(n + 1) * c]
        pre = _dot(lower, d_c, precision=HIGHEST)
        suf = _dot(upper, d_c, precision=HIGHEST)
        o_ref[pl.ds(n * c, c), :] = jnp.where(lane < GDN_HEADS, pre,
                                              jnp.where(lane < 2 * GDN_HEADS, suf, beta[n * c:(n + 1) * c]))


def _gdn_gates(xf, w_gate_bf16, alog_vec, dtb_vec, b, s):
    d = xf.shape[1]
    full = lambda shape: pl.BlockSpec(shape, lambda i: (0,) * len(shape))
    return pl.pallas_call(
        _gdn_gates_kernel,
        out_shape=jax.ShapeDtypeStruct((b * s, GDN_GATE_LANES), F32),
        grid=(b,),
        in_specs=[pl.BlockSpec((s, d), lambda i: (i, 0)), full((d, GDN_GATE_LANES)),
                  full((1, GDN_GATE_LANES)), full((1, GDN_GATE_LANES))],
        out_specs=pl.BlockSpec((s, GDN_GATE_LANES), lambda i: (i, 0)),
        compiler_params=_cparams("parallel"),
        name="gdn_gates",
    )(xf, w_gate_bf16, alog_vec, dtb_vec)


def _block_diag2(r, is_b):
    return jnp.concatenate([jnp.where(is_b, 0.0, r), jnp.where(is_b, r, 0.0)], axis=0)


def _gdn_core_kernel(q_ref, k_ref, v_ref, z_ref, gt_ref, cq_ref, ck_ref, cv_ref, nw_ref, o_ref,
                     pad_ref, qn_ref, kn_ref, vv_ref, u_ref, w_ref, qd_ref, kd_ref, aqk_ref, dend_ref, of_ref, ob_ref):
    s, dk = q_ref.shape
    c = GDN_CHUNK
    n_chunks = s // c
    head = pl.program_id(1)

    zeros = jnp.zeros((GDN_CONV_PAD, dk), F32)
    pad_ref[pl.ds(0, GDN_CONV_PAD), :] = zeros
    pad_ref[pl.ds(GDN_CONV_PAD + s, GDN_CONV_PAD), :] = zeros
    slab = min(s, 256)
    for src, cw_ref, dst, mode in ((q_ref, cq_ref, qn_ref, "q"), (k_ref, ck_ref, kn_ref, "k"), (v_ref, cv_ref, vv_ref, "v")):
        pad_ref[pl.ds(GDN_CONV_PAD, s), :] = src[...].astype(F32)
        cw = cw_ref[...]
        for r0 in range(0, s, slab):
            acc = None
            for j in range(GDN_CONV):
                term = pad_ref[pl.ds(GDN_CONV_PAD + r0 + j - GDN_CONV // 2, slab), :] * cw[j:j + 1, :]
                acc = term if acc is None else acc + term
            y = _silu(acc)
            if mode != "v":
                y = y * lax.rsqrt(jnp.sum(y * y, axis=-1, keepdims=True) + 1e-6)
            if mode == "q":
                y = y * (GDN_HEAD_DIM ** -0.5)
            dst[pl.ds(r0, slab), :] = y

    lane = lax.broadcasted_iota(I32, (c, 2 * c), 1)
    row = lax.broadcasted_iota(I32, (c, 2 * c), 0)
    is_b = lane >= c
    col = jnp.where(is_b, lane - c, lane)
    ahead = jnp.where(is_b, col - row, row - col)
    incl = ahead >= 0
    strict = ahead > 0
    eye2 = jnp.where(row == col, 1.0, 0.0)
    shift = jnp.where(head == 0, 0, V7X_LANES - head)

    def prep(n, carry):
        rows = pl.ds(pl.multiple_of(n * c, c), c)
        qc, kc, vc = qn_ref[rows, :], kn_ref[rows, :], vv_ref[rows, :]
        gt = pltpu.roll(gt_ref[rows, :], shift, axis=1)
        g_f, g_b = gt[:, 0:1], gt[:, GDN_HEADS:GDN_HEADS + 1]
        b_f, b_b = gt[:, 2 * GDN_HEADS:2 * GDN_HEADS + 1], gt[:, 3 * GDN_HEADS:3 * GDN_HEADS + 1]
        a_mat = jnp.where(lane == 0, g_f, jnp.where(lane == 2, g_b, jnp.where((lane == 1) | (lane == 3), 1.0, 0.0)))
        b_top = jnp.where(lane == 0, 1.0, jnp.where(lane == 1, -g_f, 0.0))
        b_bot = jnp.where(lane == 2, 1.0, jnp.where(lane == 3, -g_b, 0.0))
        dmat = _dot_nt(a_mat, jnp.concatenate([b_top, b_bot], axis=0), precision=HIGHEST)
        gamma = jnp.exp(jnp.where(incl, dmat, NEG_BIG))
        kcb = kc.astype(BF16)
        prod = _dot_nt(jnp.concatenate([qc.astype(BF16), kcb], axis=0), jnp.concatenate([kcb, kcb], axis=0))
        qk2, kk2 = prod[:c], prod[c:]
        beta2 = jnp.where(is_b, b_b, b_f)
        m = jnp.where(strict, kk2 * gamma, 0.0) * beta2
        x_inv = eye2 - m
        p = _dot(m.astype(BF16), _block_diag2(m, is_b).astype(BF16))
        for step in range(5):
            y = _dot(jnp.concatenate([x_inv, p], axis=0).astype(BF16), _block_diag2(p, is_b).astype(BF16))
            x_inv = x_inv + y[:c]
            p = y[c:]
        eg_f, eg_b = jnp.exp(g_f), jnp.exp(g_b)
        rhs = jnp.concatenate([jnp.concatenate([vc * b_f, kc * (b_f * eg_f)], axis=1),
                               jnp.concatenate([vc * b_b, kc * (b_b * eg_b)], axis=1)], axis=0)
        sol = _dot(_block_diag2(x_inv, is_b).astype(BF16), rhs.astype(BF16))
        gl_f, gl_b = g_f[c - 1:c, :], g_b[0:1, :]
        for d, (eg, gl, g_d) in enumerate(((eg_f, gl_f, g_f), (eg_b, gl_b, g_b))):
            u_ref[d, rows, :] = sol[d * c:(d + 1) * c, :dk]
            w_ref[d, rows, :] = sol[d * c:(d + 1) * c, dk:]
            qd_ref[d, rows, :] = qc * eg
            kd_ref[d, rows, :] = kc * jnp.exp(gl - g_d)
        aqk_ref[rows, :] = jnp.where(incl, qk2 * gamma, 0.0)
        dend = jnp.where(is_b[:V7X_SUBLANES], jnp.exp(gl_b), jnp.exp(gl_f))
        dend_ref[pl.ds(pl.multiple_of(n * V7X_SUBLANES, V7X_SUBLANES), V7X_SUBLANES), :] = dend
        return carry

    lax.fori_loop(0, n_chunks, prep, 0)

    def step(n, states):
        new_states = []
        for d, (o_dst, chunk) in enumerate(((of_ref, n), (ob_ref, n_chunks - 1 - n))):
            rows = pl.ds(pl.multiple_of(chunk * c, c), c)
            st = states[d]
            stb = st.astype(BF16)
            v_new = u_ref[d, rows, :] - _dot(w_ref[d, rows, :].astype(BF16), stb)
            vnb = v_new.astype(BF16)
            aqk = aqk_ref[rows, :][:, d * c:(d + 1) * c]
            o_dst[rows, :] = _dot(qd_ref[d, rows, :].astype(BF16), stb) + _dot(aqk.astype(BF16), vnb)
            dend = dend_ref[pl.ds(pl.multiple_of(chunk * V7X_SUBLANES, V7X_SUBLANES), 1), :][:, d * c:d * c + 1]
            new_states.append(st * dend + _dot_tn(kd_ref[d, rows, :].astype(BF16), vnb))
        return tuple(new_states)

    zero_state = jnp.zeros((dk, dk), F32)
    lax.fori_loop(0, n_chunks, step, (zero_state, zero_state))

    for r0 in range(0, s, slab):
        rs = pl.ds(r0, slab)
        o = of_ref[rs, :] + ob_ref[rs, :]
        o = o * lax.rsqrt(jnp.mean(o * o, axis=-1, keepdims=True) + 1e-6) * nw_ref[...]
        o_ref[rs, :] = (o * _silu(z_ref[rs, :].astype(F32))).astype(o_ref.dtype)


def _gdn_core(proj, gates, conv_w, norm_w, b, s):
    dk = GDN_HEAD_DIM
    hh = GDN_HEADS
    blk = lambda off: pl.BlockSpec((s, dk), lambda i, h: (i, off + h))
    cblk = lambda off: pl.BlockSpec((GDN_CONV, dk), lambda i, h: (0, off + h))
    n_chunks = s // GDN_CHUNK
    seq = lambda: pltpu.VMEM((s, dk), F32)
    seq2 = lambda: pltpu.VMEM((2, s, dk), F32)
    return pl.pallas_call(
        _gdn_core_kernel,
        out_shape=jax.ShapeDtypeStruct((b * s, GDN_WIDTH), BF16),
        grid=(b, hh),
        in_specs=[blk(0), blk(hh), blk(2 * hh), blk(3 * hh),
                  pl.BlockSpec((s, GDN_GATE_LANES), lambda i, h: (i, 0)),
                  cblk(0), cblk(hh), cblk(2 * hh),
                  pl.BlockSpec((1, dk), lambda i, h: (0, 0))],
        out_specs=pl.BlockSpec((s, dk), lambda i, h: (i, h)),
        scratch_shapes=[pltpu.VMEM((s + 2 * GDN_CONV_PAD, dk), F32), seq(), seq(), seq(),
                        seq2(), seq2(), seq2(), seq2(), seq(),
                        pltpu.VMEM((n_chunks * V7X_SUBLANES, 2 * GDN_CHUNK), F32), seq(), seq()],
        compiler_params=_cparams("parallel", "parallel"),
        name="gdn_core",
    )(proj, proj, proj, proj, gates, conv_w, conv_w, conv_w, norm_w.reshape(1, dk))


def _gdn_mixer_core(xf, w_in, conv_w, a_log, dt_bias, norm_w, b, s):
    n_main = 4 * GDN_WIDTH
    n_gate = 4 * GDN_HEADS
    proj = _linear(xf, w_in[:, :n_main].astype(BF16), BF16)
    w_gate = jnp.pad(w_in[:, n_main:], ((0, 0), (0, GDN_GATE_LANES - n_gate))).astype(BF16)
    lane_pad = lambda v: jnp.pad(v.astype(F32).reshape(1, 2 * GDN_HEADS), ((0, 0), (0, GDN_GATE_LANES - 2 * GDN_HEADS)))
    gates = _gdn_gates(xf, w_gate, lane_pad(a_log), lane_pad(dt_bias), b, s)
    return _gdn_core(proj, gates, conv_w.astype(F32), norm_w.astype(F32), b, s)


def kernel(x, a_w_in, a_conv, a_A_log, a_dt_bias, a_norm_w, a_w_out, b_w_in, b_sink, b_w_out, rel_bias, c_w_in, c_w_group, c_scale, c_w_out, router_w, router_b, moe_w_gate, moe_w_up, moe_w_down, ln_g, ln_b):
    b, s, d = x.shape
    t = b * s
    xf = x.reshape(t, d)
    router_wt = router_w.T
    router_b_col = router_b.reshape(N_EXPERTS, 1).astype(F32)
    bias_tbl = _attn_bias_table(rel_bias)
    for i in range(DEPTH):
        kind, j = i % N_MIXERS, i // N_MIXERS
        if kind == 0:
            h = _gdn_mixer_core(xf, a_w_in[j], a_conv[j], a_A_log[j], a_dt_bias[j], a_norm_w[j], b, s)
            w_out = a_w_out[j]
        elif kind == 1:
            proj = _linear(xf, b_w_in[j].astype(BF16), BF16)
            h = _attn_core(proj, bias_tbl, b_sink[j], b, s)
            w_out = b_w_out[j]
        else:
            u = _linear(xf, c_w_in[j].astype(BF16), F32)
            h = _pool_core(u.reshape(b, s, d), c_w_group[j].astype(BF16), c_scale[j]).reshape(t, d)
            w_out = c_w_out[j]
        x1, route_i, gate_cols, counts = _post_mixer(h, w_out.astype(BF16), xf, ln_g[i, 0], ln_b[i, 0],
                                                     router_wt, router_b_col)
        xf = _moe_layer(x1, route_i, gate_cols, counts[:, 0], moe_w_gate[i], moe_w_up[i], moe_w_down[i],
                        ln_g[i, 1], ln_b[i, 1])
    return xf.reshape(b, s, d)
```

```python
import functools
import math

import jax
import jax.numpy as jnp
from jax import lax
from jax.experimental import pallas as pl
from jax.experimental.pallas import tpu as pltpu

F32 = jnp.float32
BF16 = jnp.bfloat16
I32 = jnp.int32

D_MODEL = 1024
DEPTH = 4
N_MIXERS = 3
DEEPNORM_ALPHA = (2.0 * DEPTH) ** 0.25
LN_EPS = 1e-5

GDN_HEADS = 8
GDN_HEAD_DIM = 128
GDN_WIDTH = GDN_HEADS * GDN_HEAD_DIM
GDN_CONV = 5
GDN_CHUNK = 64

ATT_HEADS = 16
ATT_KV_HEADS = 4
ATT_HEAD_DIM = 64
ATT_GROUP = ATT_HEADS // ATT_KV_HEADS
ATT_WINDOW = 128
ATT_QBLOCK = 128
ATT_KSPAN = ATT_QBLOCK + 2 * ATT_WINDOW
REL_BUCKETS = 32
REL_MAX_DIST = 128

POOL_WINDOWS = (2, 4, 8, 16)
POOL_GROUP_DIM = D_MODEL // len(POOL_WINDOWS)

N_EXPERTS = 16
N_EXPERT_GROUPS = 4
EXPERTS_PER_GROUP = N_EXPERTS // N_EXPERT_GROUPS
TOP_K = 2
EXPERT_DIM = 512

V7X_LANES = 128
V7X_SUBLANES = 8
V7X_VMEM_LIMIT_BYTES = 56 * 1024 * 1024

ROW_TILE = 512
MOE_BLOCK = 512
NEG_BIG = -1e30

HIGHEST = lax.Precision.HIGHEST


def _cparams(*sem):
    return pltpu.CompilerParams(dimension_semantics=tuple(sem), vmem_limit_bytes=V7X_VMEM_LIMIT_BYTES)


def _dot(a, b, **kw):
    return jnp.dot(a, b, preferred_element_type=F32, **kw)


def _dot_nt(a, b, **kw):
    return lax.dot_general(a, b, (((1,), (1,)), ((), ())), preferred_element_type=F32, **kw)


def _dot_tn(a, b, **kw):
    return lax.dot_general(a, b, (((0,), (0,)), ((), ())), preferred_element_type=F32, **kw)


def _silu(x):
    return x * jax.nn.sigmoid(x)


def _linear_kernel(x_ref, w_ref, o_ref):
    o_ref[...] = _dot(x_ref[...].astype(BF16), w_ref[...]).astype(o_ref.dtype)


def _linear(x, w_bf16, out_dtype, tm=ROW_TILE):
    m, k = x.shape
    n = w_bf16.shape[1]
    return pl.pallas_call(
        _linear_kernel,
        out_shape=jax.ShapeDtypeStruct((m, n), out_dtype),
        grid=(m // tm,),
        in_specs=[pl.BlockSpec((tm, k), lambda i: (i, 0)), pl.BlockSpec((k, n), lambda i: (0, 0))],
        out_specs=pl.BlockSpec((tm, n), lambda i: (i, 0)),
        compiler_params=_cparams("parallel"),
        name="linear",
    )(x, w_bf16)


def _layer_norm_rows(s, g, b):
    mu = jnp.mean(s, axis=-1, keepdims=True)
    xc = s - mu
    var = jnp.mean(xc * xc, axis=-1, keepdims=True)
    return xc * lax.rsqrt(var + LN_EPS) * g + b


def _top2_of4(a, b, c, d):
    hi1, lo1 = jnp.maximum(a, b), jnp.minimum(a, b)
    hi2, lo2 = jnp.maximum(c, d), jnp.minimum(c, d)
    return jnp.maximum(hi1, hi2) + jnp.maximum(jnp.minimum(hi1, hi2), jnp.maximum(lo1, lo2))


def _route_rows(scores, biased):
    bi = [biased[e:e + 1, :] for e in range(N_EXPERTS)]
    sc = [scores[e:e + 1, :] for e in range(N_EXPERTS)]
    gs = [_top2_of4(*bi[4 * g:4 * g + 4]) for g in range(N_EXPERT_GROUPS)]
    group = jnp.zeros_like(gs[0], dtype=I32)
    best = gs[0]
    for g in range(1, N_EXPERT_GROUPS):
        upd = gs[g] > best
        group = jnp.where(upd, g, group)
        best = jnp.where(upd, gs[g], best)

    def pick(rows, j):
        out = rows[j]
        for g in range(1, N_EXPERT_GROUPS):
            out = jnp.where(group == g, rows[4 * g + j], out)
        return out

    v = [pick(bi, j) for j in range(EXPERTS_PER_GROUP)]
    s = [pick(sc, j) for j in range(EXPERTS_PER_GROUP)]
    i1 = jnp.zeros_like(group)
    b1 = v[0]
    for j in range(1, EXPERTS_PER_GROUP):
        upd = v[j] > b1
        i1 = jnp.where(upd, j, i1)
        b1 = jnp.where(upd, v[j], b1)
    i2 = jnp.full_like(group, -1)
    b2 = jnp.full_like(b1, -jnp.inf)
    for j in range(EXPERTS_PER_GROUP):
        upd = (i1 != j) & ((v[j] > b2) | (i2 < 0))
        i2 = jnp.where(upd, j, i2)
        b2 = jnp.where(upd, v[j], b2)
    s1 = s[0]
    s2 = s[0]
    for j in range(1, EXPERTS_PER_GROUP):
        s1 = jnp.where(i1 == j, s[j], s1)
        s2 = jnp.where(i2 == j, s[j], s2)
    den = s1 + s2
    return group * EXPERTS_PER_GROUP + i1, group * EXPERTS_PER_GROUP + i2, s1 / den, s2 / den


def _post_mixer_kernel(h_ref, w_ref, x_ref, g_ref, b_ref, rwt_ref, rb_ref, ustrict_ref, eye_ref,
                       x1_ref, ri_ref, gc_ref, cnt_ref, base_ref):
    tm = x_ref.shape[0]

    @pl.when(pl.program_id(0) == 0)
    def _():
        base_ref[...] = jnp.zeros_like(base_ref)

    y = _dot(h_ref[...], w_ref[...])
    x1 = _layer_norm_rows(DEEPNORM_ALPHA * x_ref[...] + y, g_ref[...], b_ref[...])
    x1_ref[...] = x1

    logits = _dot_nt(rwt_ref[...], x1, precision=HIGHEST)
    scores = jax.nn.sigmoid(logits)
    e1, e2, g1, g2 = _route_rows(scores, scores + rb_ref[...])

    eidx = lax.broadcasted_iota(I32, (N_EXPERTS, tm), 0)
    hit1 = eidx == e1
    hit2 = eidx == e2
    onehot = jnp.where(hit1 | hit2, 1.0, 0.0)
    base = base_ref[:, 0:1]
    before = _dot(onehot.astype(BF16), ustrict_ref[...]) + base
    r1 = jnp.sum(jnp.where(hit1, before, 0.0), axis=0, keepdims=True)
    r2 = jnp.sum(jnp.where(hit2, before, 0.0), axis=0, keepdims=True)
    new_base = base + jnp.sum(onehot, axis=1, keepdims=True)
    base_ref[...] = jnp.broadcast_to(new_base, base_ref.shape)
    cnt_ref[...] = jnp.broadcast_to(new_base, cnt_ref.shape)

    ri_ref[...] = jnp.concatenate(
        [e1, e2, r1.astype(I32), r2.astype(I32), jnp.zeros((4, tm), I32)], axis=0)
    gates = jnp.concatenate([g1, g2, jnp.zeros((6, tm), F32)], axis=0)
    gc_ref[...] = _dot_nt(eye_ref[...], gates, precision=HIGHEST)


def _post_mixer(h_bf16, w_out_bf16, x, ln_g, ln_b, router_wt, router_b_col, tm=ROW_TILE):
    t, d = x.shape
    ustrict = (jnp.arange(tm)[:, None] < jnp.arange(tm)[None, :]).astype(BF16)
    eye = jnp.eye(tm, dtype=F32)
    full = lambda shape: pl.BlockSpec(shape, lambda i: (0,) * len(shape))
    return pl.pallas_call(
        _post_mixer_kernel,
        out_shape=(jax.ShapeDtypeStruct((t, d), F32),
                   jax.ShapeDtypeStruct((8, t), I32),
                   jax.ShapeDtypeStruct((t, 8), F32),
                   jax.ShapeDtypeStruct((N_EXPERTS, V7X_LANES), F32)),
        grid=(t // tm,),
        in_specs=[pl.BlockSpec((tm, h_bf16.shape[1]), lambda i: (i, 0)),
                  full(w_out_bf16.shape),
                  pl.BlockSpec((tm, d), lambda i: (i, 0)),
                  full((1, d)), full((1, d)),
                  full((N_EXPERTS, d)), full((N_EXPERTS, 1)),
                  full((tm, tm)), full((tm, tm))],
        out_specs=(pl.BlockSpec((tm, d), lambda i: (i, 0)),
                   pl.BlockSpec((8, tm), lambda i: (0, i)),
                   pl.BlockSpec((tm, 8), lambda i: (i, 0)),
                   pl.BlockSpec((N_EXPERTS, V7X_LANES), lambda i: (0, 0))),
        scratch_shapes=[pltpu.VMEM((N_EXPERTS, V7X_LANES), F32)],
        compiler_params=_cparams("arbitrary"),
        name="post_mixer_router",
    )(h_bf16, w_out_bf16, x, ln_g.reshape(1, d), ln_b.reshape(1, d), router_wt, router_b_col, ustrict, eye)


def _row_copy(src_ref, src_row, dst_ref, dst_row, sem):
    return pltpu.make_async_copy(src_ref.at[pl.ds(src_row, 1), :], dst_ref.at[pl.ds(dst_row, 1), :], sem)


def _dispatch_kernel(x_ref, dest_hbm, xs_in_hbm, xs_hbm, idx_smem, idx_sem, row_sem):
    del xs_in_hbm
    tm = x_ref.shape[0]
    i = pl.program_id(0)
    idx_copy = pltpu.make_async_copy(dest_hbm.at[i], idx_smem, idx_sem)
    idx_copy.start()
    idx_copy.wait()

    def issue(j, carry):
        for k in range(TOP_K):
            _row_copy(x_ref, j, xs_hbm, idx_smem[k, j], row_sem).start()
        return carry

    lax.fori_loop(0, tm, issue, 0, unroll=8)

    def drain(j, carry):
        for k in range(TOP_K):
            _row_copy(x_ref, j, xs_hbm, idx_smem[k, j], row_sem).wait()
        return carry

    lax.fori_loop(0, tm, drain, 0, unroll=8)


def _dispatch(x1, dest_tiles, n_rows, tm=ROW_TILE):
    t, d = x1.shape
    xs0 = jnp.zeros((n_rows, d), F32)
    return pl.pallas_call(
        _dispatch_kernel,
        out_shape=jax.ShapeDtypeStruct((n_rows, d), F32),
        grid=(t // tm,),
        in_specs=[pl.BlockSpec((tm, d), lambda i: (i, 0)),
                  pl.BlockSpec(memory_space=pl.ANY),
                  pl.BlockSpec(memory_space=pl.ANY)],
        out_specs=pl.BlockSpec(memory_space=pl.ANY),
        scratch_shapes=[pltpu.SMEM((TOP_K, tm), I32), pltpu.SemaphoreType.DMA, pltpu.SemaphoreType.DMA],
        input_output_aliases={2: 0},
        compiler_params=_cparams("arbitrary"),
        name="moe_dispatch",
    )(x1, dest_tiles, xs0)


def _expert_kernel(be_ref, xs_ref, wg_ref, wu_ref, wd_ref, ys_ref):
    del be_ref
    x = xs_ref[...].astype(BF16)
    hidden = _silu(_dot(x, wg_ref[0].astype(BF16))) * _dot(x, wu_ref[0].astype(BF16))
    ys_ref[...] = _dot(hidden.astype(BF16), wd_ref[0].astype(BF16))


def _experts(block_expert, xs, wg, wu, wd):
    n_rows, d = xs.shape
    n_blocks = n_rows // MOE_BLOCK
    f = wg.shape[2]
    grid_spec = pltpu.PrefetchScalarGridSpec(
        num_scalar_prefetch=1,
        grid=(n_blocks,),
        in_specs=[pl.BlockSpec((MOE_BLOCK, d), lambda i, be: (i, 0)),
                  pl.BlockSpec((1, d, f), lambda i, be: (be[i], 0, 0)),
                  pl.BlockSpec((1, d, f), lambda i, be: (be[i], 0, 0)),
                  pl.BlockSpec((1, f, d), lambda i, be: (be[i], 0, 0))],
        out_specs=pl.BlockSpec((MOE_BLOCK, d), lambda i, be: (i, 0)),
    )
    return pl.pallas_call(
        _expert_kernel,
        out_shape=jax.ShapeDtypeStruct((n_rows, d), F32),
        grid_spec=grid_spec,
        compiler_params=_cparams("parallel"),
        name="moe_experts",
    )(block_expert, xs, wg, wu, wd)


def _combine_kernel(x1_ref, gc_ref, g_ref, b_ref, dest_hbm, ys_hbm, o_ref, idx_smem, ybuf, idx_sem, row_sem):
    tm = x1_ref.shape[0]
    i = pl.program_id(0)
    idx_copy = pltpu.make_async_copy(dest_hbm.at[i], idx_smem, idx_sem)
    idx_copy.start()
    idx_copy.wait()

    def issue(j, carry):
        for k in range(TOP_K):
            _row_copy(ys_hbm, idx_smem[k, j], ybuf.at[k], j, row_sem).start()
        return carry

    lax.fori_loop(0, tm, issue, 0, unroll=8)

    def drain(j, carry):
        for k in range(TOP_K):
            _row_copy(ys_hbm, idx_smem[k, j], ybuf.at[k], j, row_sem).wait()
        return carry

    lax.fori_loop(0, tm, drain, 0, unroll=8)

    gc = gc_ref[...]
    f = ybuf[0] * gc[:, 0:1] + ybuf[1] * gc[:, 1:2]
    o_ref[...] = _layer_norm_rows(DEEPNORM_ALPHA * x1_ref[...] + f, g_ref[...], b_ref[...])


def _combine(x1, gate_cols, ln_g, ln_b, dest_tiles, ys, tm=ROW_TILE):
    t, d = x1.shape
    full = lambda shape: pl.BlockSpec(shape, lambda i: (0,) * len(shape))
    return pl.pallas_call(
        _combine_kernel,
        out_shape=jax.ShapeDtypeStruct((t, d), F32),
        grid=(t // tm,),
        in_specs=[pl.BlockSpec((tm, d), lambda i: (i, 0)),
                  pl.BlockSpec((tm, 8), lambda i: (i, 0)),
                  full((1, d)), full((1, d)),
                  pl.BlockSpec(memory_space=pl.ANY),
                  pl.BlockSpec(memory_space=pl.ANY)],
        out_specs=pl.BlockSpec((tm, d), lambda i: (i, 0)),
        scratch_shapes=[pltpu.SMEM((TOP_K, tm), I32), pltpu.VMEM((TOP_K, tm, d), F32),
                        pltpu.SemaphoreType.DMA, pltpu.SemaphoreType.DMA],
        compiler_params=_cparams("arbitrary"),
        name="moe_combine_ln",
    )(x1, gate_cols, ln_g.reshape(1, d), ln_b.reshape(1, d), dest_tiles, ys)


def _moe_layer(x1, route_i, gate_cols, counts, wg, wu, wd, ln_g, ln_b, tm=ROW_TILE):
    t, d = x1.shape
    n_tiles = t // tm
    n_rows = t * TOP_K + N_EXPERTS * MOE_BLOCK
    counts = counts.astype(I32)
    padded = (counts + MOE_BLOCK - 1) // MOE_BLOCK * MOE_BLOCK
    pends = jnp.cumsum(padded)
    pstarts = pends - padded
    eids = route_i[0:2]
    start_of = jnp.sum(jnp.where(eids[None] == jnp.arange(N_EXPERTS, dtype=I32)[:, None, None],
                                 pstarts[:, None, None], 0), axis=0)
    dest = start_of + route_i[2:4]
    dest_tiles = dest.reshape(TOP_K, n_tiles, tm).transpose(1, 0, 2)
    n_blocks = n_rows // MOE_BLOCK
    block_expert = jnp.minimum(
        jnp.sum((jnp.arange(n_blocks, dtype=I32) * MOE_BLOCK)[:, None] >= pends[None, :], axis=-1),
        N_EXPERTS - 1).astype(I32)
    xs = _dispatch(x1, dest_tiles, n_rows, tm)
    ys = _experts(block_expert, xs, wg, wu, wd)
    return _combine(x1, gate_cols, ln_g, ln_b, dest_tiles, ys, tm)


POOL_PAD = 16


def _pool_kernel(u_ref, wg_ref, scale_ref, o_ref, pad_ref):
    s, c = u_ref.shape[1], u_ref.shape[2]
    u = u_ref[0]
    zeros = jnp.zeros((POOL_PAD, c), F32)
    pad_ref[pl.ds(0, POOL_PAD), :] = zeros
    pad_ref[pl.ds(POOL_PAD + s, POOL_PAD), :] = zeros
    pad_ref[pl.ds(POOL_PAD, s), :] = u
    pos = lax.broadcasted_iota(I32, (s, c), 0)
    group = pl.program_id(1)
    for gi, win in enumerate(POOL_WINDOWS):
        @pl.when(group == gi)
        def _(win=win):
            half = win // 2
            total = pad_ref[pl.ds(POOL_PAD - half, s), :]
            for j in range(1 - half, half):
                total = total + pad_ref[pl.ds(POOL_PAD + j, s), :]
            count = (jnp.minimum(pos + half, s) - jnp.maximum(pos - half, 0)).astype(F32)
            mixed = total / count - u
            y = _dot(mixed.astype(BF16), wg_ref[0]) * scale_ref[...]
            o_ref[0] = y.astype(o_ref.dtype)


def _pool_core(u, w_group_bf16, scale):
    b, s, d = u.shape
    c = POOL_GROUP_DIM
    return pl.pallas_call(
        _pool_kernel,
        out_shape=jax.ShapeDtypeStruct((b, s, d), BF16),
        grid=(b, d // c),
        in_specs=[pl.BlockSpec((1, s, c), lambda i, g: (i, 0, g)),
                  pl.BlockSpec((1, c, c), lambda i, g: (g, 0, 0)),
                  pl.BlockSpec((1, c), lambda i, g: (0, g))],
        out_specs=pl.BlockSpec((1, s, c), lambda i, g: (i, 0, g)),
        scratch_shapes=[pltpu.VMEM((s + 2 * POOL_PAD, c), F32)],
        compiler_params=_cparams("parallel", "parallel"),
        name="pool_core",
    )(u, w_group_bf16, scale.reshape(1, d))


def _t5_bucket(rel):
    half = REL_BUCKETS // 2
    max_exact = half // 2
    n = jnp.abs(rel)
    large = max_exact + (jnp.log(jnp.maximum(n, 1).astype(F32) / max_exact)
                         / math.log(REL_MAX_DIST / max_exact) * (half - max_exact)).astype(I32)
    large = jnp.minimum(large, half - 1)
    return (rel > 0).astype(I32) * half + jnp.where(n < max_exact, n, large)


def _attn_bias_table(rel_bias):
    rel = jnp.arange(ATT_KSPAN)[None, :] - ATT_WINDOW - jnp.arange(ATT_QBLOCK)[:, None]
    bias = jnp.transpose(rel_bias[_t5_bucket(rel)], (2, 0, 1)).astype(F32)
    return jnp.where((jnp.abs(rel) <= ATT_WINDOW)[None], bias, NEG_BIG)


def _attn_kernel(q_ref, kp_ref, kc_ref, kn_ref, vp_ref, vc_ref, vn_ref, bias_ref, sink_ref, o_ref):
    j = pl.program_id(1)
    nb = pl.num_programs(1)
    qb = ATT_QBLOCK
    col = lax.broadcasted_iota(I32, (qb, ATT_KSPAN), 1)
    edge = jnp.where(((j == 0) & (col < qb)) | ((j == nb - 1) & (col >= 2 * qb)), NEG_BIG, 0.0)
    k_all = jnp.concatenate([kp_ref[...], kc_ref[...], kn_ref[...]], axis=0)
    v_all = jnp.concatenate([vp_ref[...], vc_ref[...], vn_ref[...]], axis=0)
    q = q_ref[...]
    outs = []
    for g in range(ATT_KV_HEADS):
        k_g = k_all[:, g * ATT_HEAD_DIM:(g + 1) * ATT_HEAD_DIM]
        v_g = v_all[:, g * ATT_HEAD_DIM:(g + 1) * ATT_HEAD_DIM]
        for r in range(ATT_GROUP):
            hd = g * ATT_GROUP + r
            q_h = q[:, hd * ATT_HEAD_DIM:(hd + 1) * ATT_HEAD_DIM]
            logits = _dot_nt(q_h, k_g) * (ATT_HEAD_DIM ** -0.5) + bias_ref[hd] + edge
            sink = sink_ref[hd]
            mx = jnp.maximum(jnp.max(logits, axis=-1, keepdims=True), sink)
            p = jnp.exp(logits - mx)
            den = jnp.sum(p, axis=-1, keepdims=True) + jnp.exp(sink - mx)
            outs.append(_dot(p.astype(BF16), v_g) / den)
    o_ref[...] = jnp.concatenate(outs, axis=-1).astype(o_ref.dtype)


def _attn_core(proj, bias_tbl, sink, b, s):
    qw = ATT_HEADS * ATT_HEAD_DIM
    kw = ATT_KV_HEADS * ATT_HEAD_DIM
    nb = s // ATT_QBLOCK
    kcol, vcol = qw // kw, qw // kw + 1
    row = lambda i, j: i * nb + j
    prev = lambda i, j: i * nb + jnp.maximum(j - 1, 0)
    nxt = lambda i, j: i * nb + jnp.minimum(j + 1, nb - 1)
    kv_spec = lambda rowf, c: pl.BlockSpec((ATT_QBLOCK, kw), lambda i, j: (rowf(i, j), c))
    return pl.pallas_call(
        _attn_kernel,
        out_shape=jax.ShapeDtypeStruct((b * s, qw), BF16),
        grid=(b, nb),
        in_specs=[pl.BlockSpec((ATT_QBLOCK, qw), lambda i, j: (row(i, j), 0)),
                  kv_spec(prev, kcol), kv_spec(row, kcol), kv_spec(nxt, kcol),
                  kv_spec(prev, vcol), kv_spec(row, vcol), kv_spec(nxt, vcol),
                  pl.BlockSpec((ATT_HEADS, ATT_QBLOCK, ATT_KSPAN), lambda i, j: (0, 0, 0)),
                  pl.BlockSpec(memory_space=pltpu.SMEM)],
        out_specs=pl.BlockSpec((ATT_QBLOCK, qw), lambda i, j: (row(i, j), 0)),
        compiler_params=_cparams("parallel", "parallel"),
        name="attn_core",
    )(proj, proj, proj, proj, proj, proj, proj, bias_tbl, sink.astype(F32))


GDN_GATE_LANES = V7X_LANES
GDN_CONV_PAD = 8
GDN_PREP_GROUP = 8


def _softplus(x):
    return jnp.maximum(x, 0.0) + jnp.log(1.0 + jnp.exp(-jnp.abs(x)))


def _gdn_gates_kernel(x_ref, w_ref, alog_ref, dtb_ref, o_ref):
    s = x_ref.shape[0]
    c = GDN_CHUNK
    gl = _dot(x_ref[...].astype(BF16), w_ref[...])
    decay = -jnp.exp(alog_ref[...]) * _softplus(gl + dtb_ref[...])
    beta = jax.nn.sigmoid(gl)
    r = lax.broadcasted_iota(I32, (c, c), 0)
    q = lax.broadcasted_iota(I32, (c, c), 1)
    lower = jnp.where(r >= q, 1.0, 0.0)
    upper = jnp.where(r <= q, 1.0, 0.0)
    lane = lax.broadcasted_iota(I32, (c, GDN_GATE_LANES), 1)
    for n in range(s // c):
        d_c = decay[n * c:(n + 1) * c]
        pre = _dot(lower, d_c, precision=HIGHEST)
        suf = _dot(upper, d_c, precision=HIGHEST)
        o_ref[pl.ds(n * c, c), :] = jnp.where(lane < GDN_HEADS, pre,
                                              jnp.where(lane < 2 * GDN_HEADS, suf, beta[n * c:(n + 1) * c]))


def _gdn_gates(xf, w_gate_bf16, alog_vec, dtb_vec, b, s):
    d = xf.shape[1]
    full = lambda shape: pl.BlockSpec(shape, lambda i: (0,) * len(shape))
    return pl.pallas_call(
        _gdn_gates_kernel,
        out_shape=jax.ShapeDtypeStruct((b * s, GDN_GATE_LANES), F32),
        grid=(b,),
        in_specs=[pl.BlockSpec((s, d), lambda i: (i, 0)), full((d, GDN_GATE_LANES)),
                  full((1, GDN_GATE_LANES)), full((1, GDN_GATE_LANES))],
        out_specs=pl.BlockSpec((s, GDN_GATE_LANES), lambda i: (i, 0)),
        compiler_params=_cparams("parallel"),
        name="gdn_gates",
    )(xf, w_gate_bf16, alog_vec, dtb_vec)


def _block_diag2(r, is_b):
    return jnp.concatenate([jnp.where(is_b, 0.0, r), jnp.where(is_b, r, 0.0)], axis=0)


def _gdn_core_kernel(q_ref, k_ref, v_ref, z_ref, gt_ref, cq_ref, ck_ref, cv_ref, nw_ref, o_ref,
                     pad_ref, qn_ref, kn_ref, vv_ref, u_ref, w_ref, qd_ref, kd_ref, aqk_ref, dend_ref, of_ref, ob_ref):
    s, dk = q_ref.shape
    c = GDN_CHUNK
    n_chunks = s // c
    head = pl.program_id(1)

    zeros = jnp.zeros((GDN_CONV_PAD, dk), F32)
    pad_ref[pl.ds(0, GDN_CONV_PAD), :] = zeros
    pad_ref[pl.ds(GDN_CONV_PAD + s, GDN_CONV_PAD), :] = zeros
    slab = min(s, 256)
    for src, cw_ref, dst, mode in ((q_ref, cq_ref, qn_ref, "q"), (k_ref, ck_ref, kn_ref, "k"), (v_ref, cv_ref, vv_ref, "v")):
        pad_ref[pl.ds(GDN_CONV_PAD, s), :] = src[...].astype(F32)
        cw = cw_ref[...]
        for r0 in range(0, s, slab):
            acc = None
            for j in range(GDN_CONV):
                term = pad_ref[pl.ds(GDN_CONV_PAD + r0 + j - GDN_CONV // 2, slab), :] * cw[j:j + 1, :]
                acc = term if acc is None else acc + term
            y = _silu(acc)
            if mode != "v":
                y = y * lax.rsqrt(jnp.sum(y * y, axis=-1, keepdims=True) + 1e-6)
            if mode == "q":
                y = y * (GDN_HEAD_DIM ** -0.5)
            dst[pl.ds(r0, slab), :] = y

    lane = lax.broadcasted_iota(I32, (c, 2 * c), 1)
    row = lax.broadcasted_iota(I32, (c, 2 * c), 0)
    is_b = lane >= c
    col = jnp.where(is_b, lane - c, lane)
    ahead = jnp.where(is_b, col - row, row - col)
    incl = ahead >= 0
    strict = ahead > 0
    eye2 = jnp.where(row == col, 1.0, 0.0)
    shift = jnp.where(head == 0, 0, V7X_LANES - head)

    group = min(GDN_PREP_GROUP, n_chunks)

    def prep(gi, carry):
        ids = [gi * group + i for i in range(group)]
        rows = [pl.ds(pl.multiple_of(n * c, c), c) for n in ids]
        qc = [qn_ref[r, :] for r in rows]
        kc = [kn_ref[r, :] for r in rows]
        vc = [vv_ref[r, :] for r in rows]
        gt = [pltpu.roll(gt_ref[r, :], shift, axis=1) for r in rows]
        g_f = [t[:, 0:1] for t in gt]
        g_b = [t[:, GDN_HEADS:GDN_HEADS + 1] for t in gt]
        b_f = [t[:, 2 * GDN_HEADS:2 * GDN_HEADS + 1] for t in gt]
        b_b = [t[:, 3 * GDN_HEADS:3 * GDN_HEADS + 1] for t in gt]
        dmat, prod = [], []
        for i in range(group):
            a_mat = jnp.where(lane == 0, g_f[i], jnp.where(lane == 2, g_b[i],
                                                           jnp.where((lane == 1) | (lane == 3), 1.0, 0.0)))
            b_top = jnp.where(lane == 0, 1.0, jnp.where(lane == 1, -g_f[i], 0.0))
            b_bot = jnp.where(lane == 2, 1.0, jnp.where(lane == 3, -g_b[i], 0.0))
            dmat.append(_dot_nt(a_mat, jnp.concatenate([b_top, b_bot], axis=0), precision=HIGHEST))
        for i in range(group):
            kcb = kc[i].astype(BF16)
            prod.append(_dot_nt(jnp.concatenate([qc[i].astype(BF16), kcb], axis=0),
                                jnp.concatenate([kcb, kcb], axis=0)))
        gamma = [jnp.exp(jnp.where(incl, d_i, NEG_BIG)) for d_i in dmat]
        m = [jnp.where(strict, prod[i][c:] * gamma[i], 0.0) * jnp.where(is_b, b_b[i], b_f[i]) for i in range(group)]
        x_inv = [eye2 - m_i for m_i in m]
        p = [_dot(m_i.astype(BF16), _block_diag2(m_i, is_b).astype(BF16)) for m_i in m]
        for _ in range(5):
            y = [_dot(jnp.concatenate([x_inv[i], p[i]], axis=0).astype(BF16), _block_diag2(p[i], is_b).astype(BF16))
                 for i in range(group)]
            x_inv = [x_inv[i] + y[i][:c] for i in range(group)]
            p = [y_i[c:] for y_i in y]
        eg_f = [jnp.exp(g) for g in g_f]
        eg_b = [jnp.exp(g) for g in g_b]
        sol = []
        for i in range(group):
            rhs = jnp.concatenate([jnp.concatenate([vc[i] * b_f[i], kc[i] * (b_f[i] * eg_f[i])], axis=1),
                                   jnp.concatenate([vc[i] * b_b[i], kc[i] * (b_b[i] * eg_b[i])], axis=1)], axis=0)
            sol.append(_dot(_block_diag2(x_inv[i], is_b).astype(BF16), rhs.astype(BF16)))
        for i in range(group):
            gl_f, gl_b = g_f[i][c - 1:c, :], g_b[i][0:1, :]
            for d, (eg, gl, g_d) in enumerate(((eg_f[i], gl_f, g_f[i]), (eg_b[i], gl_b, g_b[i]))):
                u_ref[d, rows[i], :] = sol[i][d * c:(d + 1) * c, :dk]
                w_ref[d, rows[i], :] = sol[i][d * c:(d + 1) * c, dk:]
                qd_ref[d, rows[i], :] = qc[i] * eg
                kd_ref[d, rows[i], :] = kc[i] * jnp.exp(gl - g_d)
            aqk_ref[rows[i], :] = jnp.where(incl, prod[i][:c] * gamma[i], 0.0)
            dend = jnp.where(is_b[:V7X_SUBLANES], jnp.exp(gl_b), jnp.exp(gl_f))
            dend_ref[pl.ds(pl.multiple_of(ids[i] * V7X_SUBLANES, V7X_SUBLANES), V7X_SUBLANES), :] = dend
        return carry

    lax.fori_loop(0, n_chunks // group, prep, 0)

    def step(n, states):
        chunks = (n, n_chunks - 1 - n)
        rows = [pl.ds(pl.multiple_of(ch * c, c), c) for ch in chunks]
        stb = [st.astype(BF16) for st in states]
        ws = [_dot(w_ref[d, rows[d], :].astype(BF16), stb[d]) for d in range(2)]
        qs = [_dot(qd_ref[d, rows[d], :].astype(BF16), stb[d]) for d in range(2)]
        vnb = [(u_ref[d, rows[d], :] - ws[d]).astype(BF16) for d in range(2)]
        av = [_dot(aqk_ref[rows[d], :][:, d * c:(d + 1) * c].astype(BF16), vnb[d]) for d in range(2)]
        kv = [_dot_tn(kd_ref[d, rows[d], :].astype(BF16), vnb[d]) for d in range(2)]
        new_states = []
        for d, o_dst in enumerate((of_ref, ob_ref)):
            o_dst[rows[d], :] = qs[d] + av[d]
            dend = dend_ref[pl.ds(pl.multiple_of(chunks[d] * V7X_SUBLANES, V7X_SUBLANES), 1), :][:, d * c:d * c + 1]
            new_states.append(states[d] * dend + kv[d])
        return tuple(new_states)

    zero_state = jnp.zeros((dk, dk), F32)
    lax.fori_loop(0, n_chunks, step, (zero_state, zero_state))

    for r0 in range(0, s, slab):
        rs = pl.ds(r0, slab)
        o = of_ref[rs, :] + ob_ref[rs, :]
        o = o * lax.rsqrt(jnp.mean(o * o, axis=-1, keepdims=True) + 1e-6) * nw_ref[...]
        o_ref[rs, :] = (o * _silu(z_ref[rs, :].astype(F32))).astype(o_ref.dtype)


def _gdn_core(proj, gates, conv_w, norm_w, b, s):
    dk = GDN_HEAD_DIM
    hh = GDN_HEADS
    blk = lambda off: pl.BlockSpec((s, dk), lambda i, h: (i, off + h))
    cblk = lambda off: pl.BlockSpec((GDN_CONV, dk), lambda i, h: (0, off + h))
    n_chunks = s // GDN_CHUNK
    seq = lambda: pltpu.VMEM((s, dk), F32)
    seq2 = lambda: pltpu.VMEM((2, s, dk), F32)
    return pl.pallas_call(
        _gdn_core_kernel,
        out_shape=jax.ShapeDtypeStruct((b * s, GDN_WIDTH), BF16),
        grid=(b, hh),
        in_specs=[blk(0), blk(hh), blk(2 * hh), blk(3 * hh),
                  pl.BlockSpec((s, GDN_GATE_LANES), lambda i, h: (i, 0)),
                  cblk(0), cblk(hh), cblk(2 * hh),
                  pl.BlockSpec((1, dk), lambda i, h: (0, 0))],
        out_specs=pl.BlockSpec((s, dk), lambda i, h: (i, h)),
        scratch_shapes=[pltpu.VMEM((s + 2 * GDN_CONV_PAD, dk), F32), seq(), seq(), seq(),
                        seq2(), seq2(), seq2(), seq2(), seq(),
                        pltpu.VMEM((n_chunks * V7X_SUBLANES, 2 * GDN_CHUNK), F32), seq(), seq()],
        compiler_params=_cparams("parallel", "parallel"),
        name="gdn_core",
    )(proj, proj, proj, proj, gates, conv_w, conv_w, conv_w, norm_w.reshape(1, dk))


def _gdn_mixer_core(xf, w_in, conv_w, a_log, dt_bias, norm_w, b, s):
    n_main = 4 * GDN_WIDTH
    n_gate = 4 * GDN_HEADS
    proj = _linear(xf, w_in[:, :n_main].astype(BF16), BF16)
    w_gate = jnp.pad(w_in[:, n_main:], ((0, 0), (0, GDN_GATE_LANES - n_gate))).astype(BF16)
    lane_pad = lambda v: jnp.pad(v.astype(F32).reshape(1, 2 * GDN_HEADS), ((0, 0), (0, GDN_GATE_LANES - 2 * GDN_HEADS)))
    gates = _gdn_gates(xf, w_gate, lane_pad(a_log), lane_pad(dt_bias), b, s)
    return _gdn_core(proj, gates, conv_w.astype(F32), norm_w.astype(F32), b, s)


def kernel(x, a_w_in, a_conv, a_A_log, a_dt_bias, a_norm_w, a_w_out, b_w_in, b_sink, b_w_out, rel_bias, c_w_in, c_w_group, c_scale, c_w_out, router_w, router_b, moe_w_gate, moe_w_up, moe_w_down, ln_g, ln_b):
    b, s, d = x.shape
    t = b * s
    xf = x.reshape(t, d)
    router_wt = router_w.T
    router_b_col = router_b.reshape(N_EXPERTS, 1).astype(F32)
    bias_tbl = _attn_bias_table(rel_bias)
    for i in range(DEPTH):
        kind, j = i % N_MIXERS, i // N_MIXERS
        if kind == 0:
            h = _gdn_mixer_core(xf, a_w_in[j], a_conv[j], a_A_log[j], a_dt_bias[j], a_norm_w[j], b, s)
            w_out = a_w_out[j]
        elif kind == 1:
            proj = _linear(xf, b_w_in[j].astype(BF16), BF16)
            h = _attn_core(proj, bias_tbl, b_sink[j], b, s)
            w_out = b_w_out[j]
        else:
            u = _linear(xf, c_w_in[j].astype(BF16), F32)
            h = _pool_core(u.reshape(b, s, d), c_w_group[j].astype(BF16), c_scale[j]).reshape(t, d)
            w_out = c_w_out[j]
        x1, route_i, gate_cols, counts = _post_mixer(h, w_out.astype(BF16), xf, ln_g[i, 0], ln_b[i, 0],
                                                     router_wt, router_b_col)
        xf = _moe_layer(x1, route_i, gate_cols, counts[:, 0], moe_w_gate[i], moe_w_up[i], moe_w_down[i],
                        ln_g[i, 1], ln_b[i, 1])
    return xf.reshape(b, s, d)
```

```python
import functools
import math

import jax
import jax.numpy as jnp
from jax import lax
from jax.experimental import pallas as pl
from jax.experimental.pallas import tpu as pltpu

F32 = jnp.float32
BF16 = jnp.bfloat16
I32 = jnp.int32

D_MODEL = 1024
DEPTH = 4
N_MIXERS = 3
DEEPNORM_ALPHA = (2.0 * DEPTH) ** 0.25
LN_EPS = 1e-5

GDN_HEADS = 8
GDN_HEAD_DIM = 128
GDN_WIDTH = GDN_HEADS * GDN_HEAD_DIM
GDN_CONV = 5
GDN_CHUNK = 64

ATT_HEADS = 16
ATT_KV_HEADS = 4
ATT_HEAD_DIM = 64
ATT_GROUP = ATT_HEADS // ATT_KV_HEADS
ATT_WINDOW = 128
ATT_QBLOCK = 128
ATT_KSPAN = ATT_QBLOCK + 2 * ATT_WINDOW
REL_BUCKETS = 32
REL_MAX_DIST = 128

POOL_WINDOWS = (2, 4, 8, 16)
POOL_GROUP_DIM = D_MODEL // len(POOL_WINDOWS)

N_EXPERTS = 16
N_EXPERT_GROUPS = 4
EXPERTS_PER_GROUP = N_EXPERTS // N_EXPERT_GROUPS
TOP_K = 2
EXPERT_DIM = 512

V7X_LANES = 128
V7X_SUBLANES = 8
V7X_VMEM_LIMIT_BYTES = 56 * 1024 * 1024

ROW_TILE = 512
MOE_BLOCK = 512
NEG_BIG = -1e30

HIGHEST = lax.Precision.HIGHEST


def _cparams(*sem):
    return pltpu.CompilerParams(dimension_semantics=tuple(sem), vmem_limit_bytes=V7X_VMEM_LIMIT_BYTES)


def _dot(a, b, **kw):
    return jnp.dot(a, b, preferred_element_type=F32, **kw)


def _dot_nt(a, b, **kw):
    return lax.dot_general(a, b, (((1,), (1,)), ((), ())), preferred_element_type=F32, **kw)


def _dot_tn(a, b, **kw):
    return lax.dot_general(a, b, (((0,), (0,)), ((), ())), preferred_element_type=F32, **kw)


def _silu(x):
    return x * jax.nn.sigmoid(x)


def _linear_kernel(x_ref, w_ref, o_ref):
    o_ref[...] = _dot(x_ref[...].astype(BF16), w_ref[...]).astype(o_ref.dtype)


def _linear(x, w_bf16, out_dtype, tm=ROW_TILE):
    m, k = x.shape
    n = w_bf16.shape[1]
    return pl.pallas_call(
        _linear_kernel,
        out_shape=jax.ShapeDtypeStruct((m, n), out_dtype),
        grid=(m // tm,),
        in_specs=[pl.BlockSpec((tm, k), lambda i: (i, 0)), pl.BlockSpec((k, n), lambda i: (0, 0))],
        out_specs=pl.BlockSpec((tm, n), lambda i: (i, 0)),
        compiler_params=_cparams("parallel"),
        name="linear",
    )(x, w_bf16)


def _layer_norm_rows(s, g, b):
    mu = jnp.mean(s, axis=-1, keepdims=True)
    xc = s - mu
    var = jnp.mean(xc * xc, axis=-1, keepdims=True)
    return xc * lax.rsqrt(var + LN_EPS) * g + b


def _top2_of4(a, b, c, d):
    hi1, lo1 = jnp.maximum(a, b), jnp.minimum(a, b)
    hi2, lo2 = jnp.maximum(c, d), jnp.minimum(c, d)
    return jnp.maximum(hi1, hi2) + jnp.maximum(jnp.minimum(hi1, hi2), jnp.maximum(lo1, lo2))


def _route_rows(scores, biased):
    bi = [biased[e:e + 1, :] for e in range(N_EXPERTS)]
    sc = [scores[e:e + 1, :] for e in range(N_EXPERTS)]
    gs = [_top2_of4(*bi[4 * g:4 * g + 4]) for g in range(N_EXPERT_GROUPS)]
    group = jnp.zeros_like(gs[0], dtype=I32)
    best = gs[0]
    for g in range(1, N_EXPERT_GROUPS):
        upd = gs[g] > best
        group = jnp.where(upd, g, group)
        best = jnp.where(upd, gs[g], best)

    def pick(rows, j):
        out = rows[j]
        for g in range(1, N_EXPERT_GROUPS):
            out = jnp.where(group == g, rows[4 * g + j], out)
        return out

    v = [pick(bi, j) for j in range(EXPERTS_PER_GROUP)]
    s = [pick(sc, j) for j in range(EXPERTS_PER_GROUP)]
    i1 = jnp.zeros_like(group)
    b1 = v[0]
    for j in range(1, EXPERTS_PER_GROUP):
        upd = v[j] > b1
        i1 = jnp.where(upd, j, i1)
        b1 = jnp.where(upd, v[j], b1)
    i2 = jnp.full_like(group, -1)
    b2 = jnp.full_like(b1, -jnp.inf)
    for j in range(EXPERTS_PER_GROUP):
        upd = (i1 != j) & ((v[j] > b2) | (i2 < 0))
        i2 = jnp.where(upd, j, i2)
        b2 = jnp.where(upd, v[j], b2)
    s1 = s[0]
    s2 = s[0]
    for j in range(1, EXPERTS_PER_GROUP):
        s1 = jnp.where(i1 == j, s[j], s1)
        s2 = jnp.where(i2 == j, s[j], s2)
    den = s1 + s2
    return group * EXPERTS_PER_GROUP + i1, group * EXPERTS_PER_GROUP + i2, s1 / den, s2 / den


def _post_mixer_kernel(h_ref, w_ref, x_ref, g_ref, b_ref, rwt_ref, rb_ref, ustrict_ref, eye_ref,
                       x1_ref, ri_ref, gc_ref, cnt_ref, base_ref):
    tm = x_ref.shape[0]

    @pl.when(pl.program_id(0) == 0)
    def _():
        base_ref[...] = jnp.zeros_like(base_ref)

    y = _dot(h_ref[...], w_ref[...])
    x1 = _layer_norm_rows(DEEPNORM_ALPHA * x_ref[...] + y, g_ref[...], b_ref[...])
    x1_ref[...] = x1

    logits = _dot_nt(rwt_ref[...], x1, precision=HIGHEST)
    scores = jax.nn.sigmoid(logits)
    e1, e2, g1, g2 = _route_rows(scores, scores + rb_ref[...])

    eidx = lax.broadcasted_iota(I32, (N_EXPERTS, tm), 0)
    hit1 = eidx == e1
    hit2 = eidx == e2
    onehot = jnp.where(hit1 | hit2, 1.0, 0.0)
    base = base_ref[:, 0:1]
    before = _dot(onehot.astype(BF16), ustrict_ref[...]) + base
    r1 = jnp.sum(jnp.where(hit1, before, 0.0), axis=0, keepdims=True)
    r2 = jnp.sum(jnp.where(hit2, before, 0.0), axis=0, keepdims=True)
    new_base = base + jnp.sum(onehot, axis=1, keepdims=True)
    base_ref[...] = jnp.broadcast_to(new_base, base_ref.shape)
    cnt_ref[...] = jnp.broadcast_to(new_base, cnt_ref.shape)

    ri_ref[...] = jnp.concatenate(
        [e1, e2, r1.astype(I32), r2.astype(I32), jnp.zeros((4, tm), I32)], axis=0)
    gates = jnp.concatenate([g1, g2, jnp.zeros((6, tm), F32)], axis=0)
    gc_ref[...] = _dot_nt(eye_ref[...], gates, precision=HIGHEST)


def _post_mixer(h_bf16, w_out_bf16, x, ln_g, ln_b, router_wt, router_b_col, tm=ROW_TILE):
    t, d = x.shape
    ustrict = (jnp.arange(tm)[:, None] < jnp.arange(tm)[None, :]).astype(BF16)
    eye = jnp.eye(tm, dtype=F32)
    full = lambda shape: pl.BlockSpec(shape, lambda i: (0,) * len(shape))
    return pl.pallas_call(
        _post_mixer_kernel,
        out_shape=(jax.ShapeDtypeStruct((t, d), F32),
                   jax.ShapeDtypeStruct((8, t), I32),
                   jax.ShapeDtypeStruct((t, 8), F32),
                   jax.ShapeDtypeStruct((N_EXPERTS, V7X_LANES), F32)),
        grid=(t // tm,),
        in_specs=[pl.BlockSpec((tm, h_bf16.shape[1]), lambda i: (i, 0)),
                  full(w_out_bf16.shape),
                  pl.BlockSpec((tm, d), lambda i: (i, 0)),
                  full((1, d)), full((1, d)),
                  full((N_EXPERTS, d)), full((N_EXPERTS, 1)),
                  full((tm, tm)), full((tm, tm))],
        out_specs=(pl.BlockSpec((tm, d), lambda i: (i, 0)),
                   pl.BlockSpec((8, tm), lambda i: (0, i)),
                   pl.BlockSpec((tm, 8), lambda i: (i, 0)),
                   pl.BlockSpec((N_EXPERTS, V7X_LANES), lambda i: (0, 0))),
        scratch_shapes=[pltpu.VMEM((N_EXPERTS, V7X_LANES), F32)],
        compiler_params=_cparams("arbitrary"),
        name="post_mixer_router",
    )(h_bf16, w_out_bf16, x, ln_g.reshape(1, d), ln_b.reshape(1, d), router_wt, router_b_col, ustrict, eye)


def _row_copy(src_ref, src_row, dst_ref, dst_row, sem):
    return pltpu.make_async_copy(src_ref.at[pl.ds(src_row, 1), :], dst_ref.at[pl.ds(dst_row, 1), :], sem)


def _dispatch_kernel(x_ref, dest_hbm, xs_in_hbm, xs_hbm, idx_smem, idx_sem, row_sem):
    del xs_in_hbm
    tm = x_ref.shape[0]
    i = pl.program_id(0)
    idx_copy = pltpu.make_async_copy(dest_hbm.at[i], idx_smem, idx_sem)
    idx_copy.start()
    idx_copy.wait()

    def issue(j, carry):
        for k in range(TOP_K):
            _row_copy(x_ref, j, xs_hbm, idx_smem[k, j], row_sem).start()
        return carry

    lax.fori_loop(0, tm, issue, 0, unroll=8)

    def drain(j, carry):
        for k in range(TOP_K):
            _row_copy(x_ref, j, xs_hbm, idx_smem[k, j], row_sem).wait()
        return carry

    lax.fori_loop(0, tm, drain, 0, unroll=8)


def _dispatch(x1, dest_tiles, n_rows, tm=ROW_TILE):
    t, d = x1.shape
    xs0 = jnp.zeros((n_rows, d), F32)
    return pl.pallas_call(
        _dispatch_kernel,
        out_shape=jax.ShapeDtypeStruct((n_rows, d), F32),
        grid=(t // tm,),
        in_specs=[pl.BlockSpec((tm, d), lambda i: (i, 0)),
                  pl.BlockSpec(memory_space=pl.ANY),
                  pl.BlockSpec(memory_space=pl.ANY)],
        out_specs=pl.BlockSpec(memory_space=pl.ANY),
        scratch_shapes=[pltpu.SMEM((TOP_K, tm), I32), pltpu.SemaphoreType.DMA, pltpu.SemaphoreType.DMA],
        input_output_aliases={2: 0},
        compiler_params=_cparams("arbitrary"),
        name="moe_dispatch",
    )(x1, dest_tiles, xs0)


def _expert_kernel(be_ref, xs_ref, wg_ref, wu_ref, wd_ref, ys_ref):
    del be_ref
    x = xs_ref[...].astype(BF16)
    hidden = _silu(_dot(x, wg_ref[0].astype(BF16))) * _dot(x, wu_ref[0].astype(BF16))
    ys_ref[...] = _dot(hidden.astype(BF16), wd_ref[0].astype(BF16))


def _experts(block_expert, xs, wg, wu, wd):
    n_rows, d = xs.shape
    n_blocks = n_rows // MOE_BLOCK
    f = wg.shape[2]
    grid_spec = pltpu.PrefetchScalarGridSpec(
        num_scalar_prefetch=1,
        grid=(n_blocks,),
        in_specs=[pl.BlockSpec((MOE_BLOCK, d), lambda i, be: (i, 0)),
                  pl.BlockSpec((1, d, f), lambda i, be: (be[i], 0, 0)),
                  pl.BlockSpec((1, d, f), lambda i, be: (be[i], 0, 0)),
                  pl.BlockSpec((1, f, d), lambda i, be: (be[i], 0, 0))],
        out_specs=pl.BlockSpec((MOE_BLOCK, d), lambda i, be: (i, 0)),
    )
    return pl.pallas_call(
        _expert_kernel,
        out_shape=jax.ShapeDtypeStruct((n_rows, d), F32),
        grid_spec=grid_spec,
        compiler_params=_cparams("parallel"),
        name="moe_experts",
    )(block_expert, xs, wg, wu, wd)


def _combine_kernel(x1_ref, gc_ref, g_ref, b_ref, dest_hbm, ys_hbm, o_ref, idx_smem, ybuf, idx_sem, row_sem):
    tm = x1_ref.shape[0]
    i = pl.program_id(0)
    idx_copy = pltpu.make_async_copy(dest_hbm.at[i], idx_smem, idx_sem)
    idx_copy.start()
    idx_copy.wait()

    def issue(j, carry):
        for k in range(TOP_K):
            _row_copy(ys_hbm, idx_smem[k, j], ybuf.at[k], j, row_sem).start()
        return carry

    lax.fori_loop(0, tm, issue, 0, unroll=8)

    def drain(j, carry):
        for k in range(TOP_K):
            _row_copy(ys_hbm, idx_smem[k, j], ybuf.at[k], j, row_sem).wait()
        return carry

    lax.fori_loop(0, tm, drain, 0, unroll=8)

    gc = gc_ref[...]
    f = ybuf[0] * gc[:, 0:1] + ybuf[1] * gc[:, 1:2]
    o_ref[...] = _layer_norm_rows(DEEPNORM_ALPHA * x1_ref[...] + f, g_ref[...], b_ref[...])


def _combine(x1, gate_cols, ln_g, ln_b, dest_tiles, ys, tm=ROW_TILE):
    t, d = x1.shape
    full = lambda shape: pl.BlockSpec(shape, lambda i: (0,) * len(shape))
    return pl.pallas_call(
        _combine_kernel,
        out_shape=jax.ShapeDtypeStruct((t, d), F32),
        grid=(t // tm,),
        in_specs=[pl.BlockSpec((tm, d), lambda i: (i, 0)),
                  pl.BlockSpec((tm, 8), lambda i: (i, 0)),
                  full((1, d)), full((1, d)),
                  pl.BlockSpec(memory_space=pl.ANY),
                  pl.BlockSpec(memory_space=pl.ANY)],
        out_specs=pl.BlockSpec((tm, d), lambda i: (i, 0)),
        scratch_shapes=[pltpu.SMEM((TOP_K, tm), I32), pltpu.VMEM((TOP_K, tm, d), F32),
                        pltpu.SemaphoreType.DMA, pltpu.SemaphoreType.DMA],
        compiler_params=_cparams("arbitrary"),
        name="moe_combine_ln",
    )(x1, gate_cols, ln_g.reshape(1, d), ln_b.reshape(1, d), dest_tiles, ys)


def _moe_layer(x1, route_i, gate_cols, counts, wg, wu, wd, ln_g, ln_b, tm=ROW_TILE):
    t, d = x1.shape
    n_tiles = t // tm
    n_rows = t * TOP_K + N_EXPERTS * MOE_BLOCK
    counts = counts.astype(I32)
    padded = (counts + MOE_BLOCK - 1) // MOE_BLOCK * MOE_BLOCK
    pends = jnp.cumsum(padded)
    pstarts = pends - padded
    eids = route_i[0:2]
    start_of = jnp.sum(jnp.where(eids[None] == jnp.arange(N_EXPERTS, dtype=I32)[:, None, None],
                                 pstarts[:, None, None], 0), axis=0)
    dest = start_of + route_i[2:4]
    dest_tiles = dest.reshape(TOP_K, n_tiles, tm).transpose(1, 0, 2)
    n_blocks = n_rows // MOE_BLOCK
    block_expert = jnp.minimum(
        jnp.sum((jnp.arange(n_blocks, dtype=I32) * MOE_BLOCK)[:, None] >= pends[None, :], axis=-1),
        N_EXPERTS - 1).astype(I32)
    xs = _dispatch(x1, dest_tiles, n_rows, tm)
    ys = _experts(block_expert, xs, wg, wu, wd)
    return _combine(x1, gate_cols, ln_g, ln_b, dest_tiles, ys, tm)


POOL_PAD = 16


def _pool_kernel(u_ref, wg_ref, scale_ref, o_ref, pad_ref):
    s, c = u_ref.shape[1], u_ref.shape[2]
    u = u_ref[0]
    zeros = jnp.zeros((POOL_PAD, c), F32)
    pad_ref[pl.ds(0, POOL_PAD), :] = zeros
    pad_ref[pl.ds(POOL_PAD + s, POOL_PAD), :] = zeros
    pad_ref[pl.ds(POOL_PAD, s), :] = u
    pos = lax.broadcasted_iota(I32, (s, c), 0)
    group = pl.program_id(1)
    for gi, win in enumerate(POOL_WINDOWS):
        @pl.when(group == gi)
        def _(win=win):
            half = win // 2
            total = pad_ref[pl.ds(POOL_PAD - half, s), :]
            for j in range(1 - half, half):
                total = total + pad_ref[pl.ds(POOL_PAD + j, s), :]
            count = (jnp.minimum(pos + half, s) - jnp.maximum(pos - half, 0)).astype(F32)
            mixed = total / count - u
            y = _dot(mixed.astype(BF16), wg_ref[0]) * scale_ref[...]
            o_ref[0] = y.astype(o_ref.dtype)


def _pool_core(u, w_group_bf16, scale):
    b, s, d = u.shape
    c = POOL_GROUP_DIM
    return pl.pallas_call(
        _pool_kernel,
        out_shape=jax.ShapeDtypeStruct((b, s, d), BF16),
        grid=(b, d // c),
        in_specs=[pl.BlockSpec((1, s, c), lambda i, g: (i, 0, g)),
                  pl.BlockSpec((1, c, c), lambda i, g: (g, 0, 0)),
                  pl.BlockSpec((1, c), lambda i, g: (0, g))],
        out_specs=pl.BlockSpec((1, s, c), lambda i, g: (i, 0, g)),
        scratch_shapes=[pltpu.VMEM((s + 2 * POOL_PAD, c), F32)],
        compiler_params=_cparams("parallel", "parallel"),
        name="pool_core",
    )(u, w_group_bf16, scale.reshape(1, d))


def _t5_bucket(rel):
    half = REL_BUCKETS // 2
    max_exact = half // 2
    n = jnp.abs(rel)
    large = max_exact + (jnp.log(jnp.maximum(n, 1).astype(F32) / max_exact)
                         / math.log(REL_MAX_DIST / max_exact) * (half - max_exact)).astype(I32)
    large = jnp.minimum(large, half - 1)
    return (rel > 0).astype(I32) * half + jnp.where(n < max_exact, n, large)


def _attn_bias_table(rel_bias):
    rel = jnp.arange(ATT_KSPAN)[None, :] - ATT_WINDOW - jnp.arange(ATT_QBLOCK)[:, None]
    bias = jnp.transpose(rel_bias[_t5_bucket(rel)], (2, 0, 1)).astype(F32)
    return jnp.where((jnp.abs(rel) <= ATT_WINDOW)[None], bias, NEG_BIG)


def _attn_kernel(q_ref, kp_ref, kc_ref, kn_ref, vp_ref, vc_ref, vn_ref, bias_ref, sink_ref, o_ref):
    j = pl.program_id(1)
    nb = pl.num_programs(1)
    qb = ATT_QBLOCK
    col = lax.broadcasted_iota(I32, (qb, ATT_KSPAN), 1)
    edge = jnp.where(((j == 0) & (col < qb)) | ((j == nb - 1) & (col >= 2 * qb)), NEG_BIG, 0.0)
    k_all = jnp.concatenate([kp_ref[...], kc_ref[...], kn_ref[...]], axis=0)
    v_all = jnp.concatenate([vp_ref[...], vc_ref[...], vn_ref[...]], axis=0)
    q = q_ref[...]
    outs = []
    for g in range(ATT_KV_HEADS):
        k_g = k_all[:, g * ATT_HEAD_DIM:(g + 1) * ATT_HEAD_DIM]
        v_g = v_all[:, g * ATT_HEAD_DIM:(g + 1) * ATT_HEAD_DIM]
        for r in range(ATT_GROUP):
            hd = g * ATT_GROUP + r
            q_h = q[:, hd * ATT_HEAD_DIM:(hd + 1) * ATT_HEAD_DIM]
            logits = _dot_nt(q_h, k_g) * (ATT_HEAD_DIM ** -0.5) + bias_ref[hd] + edge
            sink = sink_ref[hd]
            mx = jnp.maximum(jnp.max(logits, axis=-1, keepdims=True), sink)
            p = jnp.exp(logits - mx)
            den = jnp.sum(p, axis=-1, keepdims=True) + jnp.exp(sink - mx)
            outs.append(_dot(p.astype(BF16), v_g) / den)
    o_ref[...] = jnp.concatenate(outs, axis=-1).astype(o_ref.dtype)


def _attn_core(proj, bias_tbl, sink, b, s):
    qw = ATT_HEADS * ATT_HEAD_DIM
    kw = ATT_KV_HEADS * ATT_HEAD_DIM
    nb = s // ATT_QBLOCK
    kcol, vcol = qw // kw, qw // kw + 1
    row = lambda i, j: i * nb + j
    prev = lambda i, j: i * nb + jnp.maximum(j - 1, 0)
    nxt = lambda i, j: i * nb + jnp.minimum(j + 1, nb - 1)
    kv_spec = lambda rowf, c: pl.BlockSpec((ATT_QBLOCK, kw), lambda i, j: (rowf(i, j), c))
    return pl.pallas_call(
        _attn_kernel,
        out_shape=jax.ShapeDtypeStruct((b * s, qw), BF16),
        grid=(b, nb),
        in_specs=[pl.BlockSpec((ATT_QBLOCK, qw), lambda i, j: (row(i, j), 0)),
                  kv_spec(prev, kcol), kv_spec(row, kcol), kv_spec(nxt, kcol),
                  kv_spec(prev, vcol), kv_spec(row, vcol), kv_spec(nxt, vcol),
                  pl.BlockSpec((ATT_HEADS, ATT_QBLOCK, ATT_KSPAN), lambda i, j: (0, 0, 0)),
                  pl.BlockSpec(memory_space=pltpu.SMEM)],
        out_specs=pl.BlockSpec((ATT_QBLOCK, qw), lambda i, j: (row(i, j), 0)),
        compiler_params=_cparams("parallel", "parallel"),
        name="attn_core",
    )(proj, proj, proj, proj, proj, proj, proj, bias_tbl, sink.astype(F32))


GDN_GATE_LANES = V7X_LANES
GDN_CONV_PAD = 8
GDN_PREP_GROUP = 8
GDN_HEADS_PER_STEP = 2
GDN_PREP_MATMULS = 10


def _softplus(x):
    return jnp.maximum(x, 0.0) + jnp.log(1.0 + jnp.exp(-jnp.abs(x)))


def _gdn_gates_kernel(x_ref, w_ref, alog_ref, dtb_ref, o_ref):
    s = x_ref.shape[0]
    c = GDN_CHUNK
    gl = _dot(x_ref[...].astype(BF16), w_ref[...])
    decay = -jnp.exp(alog_ref[...]) * _softplus(gl + dtb_ref[...])
    beta = jax.nn.sigmoid(gl)
    r = lax.broadcasted_iota(I32, (c, c), 0)
    q = lax.broadcasted_iota(I32, (c, c), 1)
    lower = jnp.where(r >= q, 1.0, 0.0)
    upper = jnp.where(r <= q, 1.0, 0.0)
    lane = lax.broadcasted_iota(I32, (c, GDN_GATE_LANES), 1)
    for n in range(s // c):
        d_c = decay[n * c:(n + 1) * c]
        pre = _dot(lower, d_c, precision=HIGHEST)
        suf = _dot(upper, d_c, precision=HIGHEST)
        o_ref[pl.ds(n * c, c), :] = jnp.where(lane < GDN_HEADS, pre,
                                              jnp.where(lane < 2 * GDN_HEADS, suf, beta[n * c:(n + 1) * c]))


def _gdn_gates(xf, w_gate_bf16, alog_vec, dtb_vec, b, s):
    d = xf.shape[1]
    full = lambda shape: pl.BlockSpec(shape, lambda i: (0,) * len(shape))
    return pl.pallas_call(
        _gdn_gates_kernel,
        out_shape=jax.ShapeDtypeStruct((b * s, GDN_GATE_LANES), F32),
        grid=(b,),
        in_specs=[pl.BlockSpec((s, d), lambda i: (i, 0)), full((d, GDN_GATE_LANES)),
                  full((1, GDN_GATE_LANES)), full((1, GDN_GATE_LANES))],
        out_specs=pl.BlockSpec((s, GDN_GATE_LANES), lambda i: (i, 0)),
        compiler_params=_cparams("parallel"),
        name="gdn_gates",
    )(xf, w_gate_bf16, alog_vec, dtb_vec)


def _block_diag2(r, is_b):
    return jnp.concatenate([jnp.where(is_b, 0.0, r), jnp.where(is_b, r, 0.0)], axis=0)


def _gdn_core_kernel(q_ref, k_ref, v_ref, z_ref, gt_ref, cq_ref, ck_ref, cv_ref, nw_ref, o_ref,
                     pad_ref, qn_ref, kn_ref, vv_ref, u_ref, w_ref, qd_ref, kd_ref, aqk_ref, dend_ref, of_ref, ob_ref):
    s = q_ref.shape[0]
    dk = GDN_HEAD_DIM
    hps = q_ref.shape[1] // dk
    c = GDN_CHUNK
    n_chunks = s // c
    first_head = pl.program_id(1) * hps

    zeros = jnp.zeros((GDN_CONV_PAD, dk), F32)
    pad_ref[pl.ds(0, GDN_CONV_PAD), :] = zeros
    pad_ref[pl.ds(GDN_CONV_PAD + s, GDN_CONV_PAD), :] = zeros
    slab = min(s, 256)
    for hd in range(hps):
        cols = slice(hd * dk, (hd + 1) * dk)
        for src, cw_ref, dst, mode in ((q_ref, cq_ref, qn_ref, "q"), (k_ref, ck_ref, kn_ref, "k"), (v_ref, cv_ref, vv_ref, "v")):
            pad_ref[pl.ds(GDN_CONV_PAD, s), :] = src[:, cols].astype(F32)
            cw = cw_ref[:, cols]
            for r0 in range(0, s, slab):
                acc = None
                for j in range(GDN_CONV):
                    term = pad_ref[pl.ds(GDN_CONV_PAD + r0 + j - GDN_CONV // 2, slab), :] * cw[j:j + 1, :]
                    acc = term if acc is None else acc + term
                y = _silu(acc)
                if mode != "v":
                    y = y * lax.rsqrt(jnp.sum(y * y, axis=-1, keepdims=True) + 1e-6)
                if mode == "q":
                    y = y * (GDN_HEAD_DIM ** -0.5)
                dst[hd, pl.ds(r0, slab), :] = y

    lane = lax.broadcasted_iota(I32, (c, 2 * c), 1)
    row = lax.broadcasted_iota(I32, (c, 2 * c), 0)
    is_b = lane >= c
    col = jnp.where(is_b, lane - c, lane)
    ahead = jnp.where(is_b, col - row, row - col)
    incl = ahead >= 0
    strict = ahead > 0
    eye2 = jnp.where(row == col, 1.0, 0.0)
    shifts = [lax.rem(2 * V7X_LANES - first_head - hd, V7X_LANES) for hd in range(hps)]

    group = min(GDN_PREP_GROUP, n_chunks)
    n_groups = n_chunks // group

    def pair_rows(t):
        if isinstance(t, int):
            return pl.ds(t * c, c), pl.ds((n_chunks - 1 - t) * c, c)
        return pl.ds(pl.multiple_of(t * c, c), c), pl.ds(pl.multiple_of((n_chunks - 1 - t) * c, c), c)

    def prep_stages(hd, pg, slot):
        rows = [pair_rows(pg * group + i) for i in range(group)]
        q_f, k_f, v_f = ([ref[hd, r[0], :] for r in rows] for ref in (qn_ref, kn_ref, vv_ref))
        q_b, k_b, v_b = ([ref[hd, r[1], :] for r in rows] for ref in (qn_ref, kn_ref, vv_ref))
        gt_f = [pltpu.roll(gt_ref[r[0], :], shifts[hd], axis=1) for r in rows]
        gt_b = [pltpu.roll(gt_ref[r[1], :], shifts[hd], axis=1) for r in rows]
        g_f = [t[:, 0:1] for t in gt_f]
        b_f = [t[:, 2 * GDN_HEADS:2 * GDN_HEADS + 1] for t in gt_f]
        g_b = [t[:, GDN_HEADS:GDN_HEADS + 1] for t in gt_b]
        b_b = [t[:, 3 * GDN_HEADS:3 * GDN_HEADS + 1] for t in gt_b]
        dmat, prod_f, prod_b = [], [], []
        for i in range(group):
            a_mat = jnp.where(lane == 0, g_f[i], jnp.where(lane == 2, g_b[i],
                                                           jnp.where((lane == 1) | (lane == 3), 1.0, 0.0)))
            b_top = jnp.where(lane == 0, 1.0, jnp.where(lane == 1, -g_f[i], 0.0))
            b_bot = jnp.where(lane == 2, 1.0, jnp.where(lane == 3, -g_b[i], 0.0))
            dmat.append(_dot_nt(a_mat, jnp.concatenate([b_top, b_bot], axis=0), precision=HIGHEST))
            yield
        for i in range(group):
            kfb, kbb = k_f[i].astype(BF16), k_b[i].astype(BF16)
            keys = jnp.concatenate([kfb, kbb], axis=0)
            prod_f.append(_dot_nt(jnp.concatenate([q_f[i].astype(BF16), kfb], axis=0), keys))
            yield
            prod_b.append(_dot_nt(jnp.concatenate([q_b[i].astype(BF16), kbb], axis=0), keys))
            yield
        qk2 = [jnp.where(is_b, prod_b[i][:c], prod_f[i][:c]) for i in range(group)]
        kk2 = [jnp.where(is_b, prod_b[i][c:], prod_f[i][c:]) for i in range(group)]
        gamma = [jnp.exp(jnp.where(incl, d_i, NEG_BIG)) for d_i in dmat]
        m = [jnp.where(strict, kk2[i] * gamma[i], 0.0) * jnp.where(is_b, b_b[i], b_f[i]) for i in range(group)]
        x_inv = [eye2 - m_i for m_i in m]
        p = []
        for m_i in m:
            p.append(_dot(m_i.astype(BF16), _block_diag2(m_i, is_b).astype(BF16)))
            yield
        for _ in range(5):
            y = []
            for i in range(group):
                y.append(_dot(jnp.concatenate([x_inv[i], p[i]], axis=0).astype(BF16),
                              _block_diag2(p[i], is_b).astype(BF16)))
                yield
            x_inv = [x_inv[i] + y[i][:c] for i in range(group)]
            p = [y_i[c:] for y_i in y]
        eg_f = [jnp.exp(g) for g in g_f]
        eg_b = [jnp.exp(g) for g in g_b]
        sol = []
        for i in range(group):
            rhs = jnp.concatenate([jnp.concatenate([v_f[i] * b_f[i], k_f[i] * (b_f[i] * eg_f[i])], axis=1),
                                   jnp.concatenate([v_b[i] * b_b[i], k_b[i] * (b_b[i] * eg_b[i])], axis=1)], axis=0)
            sol.append(_dot(_block_diag2(x_inv[i], is_b).astype(BF16), rhs.astype(BF16)))
            yield
        for i in range(group):
            local = pl.ds(i * c, c)
            gl_f, gl_b = g_f[i][c - 1:c, :], g_b[i][0:1, :]
            for d, (qc, kc, eg, gl, g_d) in enumerate(((q_f[i], k_f[i], eg_f[i], gl_f, g_f[i]),
                                                       (q_b[i], k_b[i], eg_b[i], gl_b, g_b[i]))):
                u_ref[hd, slot, d, local, :] = sol[i][d * c:(d + 1) * c, :dk]
                w_ref[hd, slot, d, local, :] = sol[i][d * c:(d + 1) * c, dk:]
                qd_ref[hd, slot, d, local, :] = qc * eg
                kd_ref[hd, slot, d, local, :] = kc * jnp.exp(gl - g_d)
            aqk_ref[hd, slot, local, :] = jnp.where(incl, qk2[i] * gamma[i], 0.0)
            dend_ref[hd, slot, pl.ds(i * V7X_SUBLANES, V7X_SUBLANES), :] = jnp.where(
                is_b[:V7X_SUBLANES], jnp.exp(gl_b), jnp.exp(gl_f))

    lanes = (slice(0, c), slice(c, 2 * c))
    chains = [(hd, d) for hd in range(hps) for d in range(2)]

    def run_group(pg, slot, prep_gens, states):
        per_gap = -(-GDN_PREP_MATMULS // 2)

        def advance():
            for _ in range(per_gap):
                for gen in prep_gens:
                    next(gen, None)

        for i in range(group):
            rows = pair_rows(pg * group + i)
            local = pl.ds(i * c, c)
            stb = [st.astype(BF16) for st in states]
            ws = [_dot(w_ref[hd, slot, d, local, :].astype(BF16), stb[n]) for n, (hd, d) in enumerate(chains)]
            qs = [_dot(qd_ref[hd, slot, d, local, :].astype(BF16), stb[n]) for n, (hd, d) in enumerate(chains)]
            advance()
            vnb = [(u_ref[hd, slot, d, local, :] - ws[n]).astype(BF16) for n, (hd, d) in enumerate(chains)]
            av = [_dot(aqk_ref[hd, slot, local, :][:, lanes[d]].astype(BF16), vnb[n])
                  for n, (hd, d) in enumerate(chains)]
            kv = [_dot_tn(kd_ref[hd, slot, d, local, :].astype(BF16), vnb[n]) for n, (hd, d) in enumerate(chains)]
            advance()
            new_states = []
            for n, (hd, d) in enumerate(chains):
                (of_ref, ob_ref)[d][hd, rows[d], :] = qs[n] + av[n]
                dend = dend_ref[hd, slot, pl.ds(i * V7X_SUBLANES, 1), :][:, d * c:d * c + 1]
                new_states.append(states[n] * dend + kv[n])
            states = tuple(new_states)
        for gen in prep_gens:
            for _ in gen:
                pass
        return states

    for hd in range(hps):
        for _ in prep_stages(hd, 0, 0):
            pass

    def fused(tg, states):
        slot = lax.rem(tg, 2)
        return run_group(tg, slot, [prep_stages(hd, tg + 1, 1 - slot) for hd in range(hps)], states)

    zero_state = jnp.zeros((dk, dk), F32)
    states = lax.fori_loop(0, n_groups - 1, fused, (zero_state,) * len(chains))
    run_group(n_groups - 1, (n_groups - 1) % 2, [], states)

    for hd in range(hps):
        cols = slice(hd * dk, (hd + 1) * dk)
        for r0 in range(0, s, slab):
            rs = pl.ds(r0, slab)
            o = of_ref[hd, rs, :] + ob_ref[hd, rs, :]
            o = o * lax.rsqrt(jnp.mean(o * o, axis=-1, keepdims=True) + 1e-6) * nw_ref[...]
            o_ref[rs, cols] = (o * _silu(z_ref[rs, cols].astype(F32))).astype(o_ref.dtype)


def _gdn_core(proj, gates, conv_w, norm_w, b, s):
    dk = GDN_HEAD_DIM
    hps = GDN_HEADS_PER_STEP
    nhb = GDN_HEADS // hps
    blk = lambda kind: pl.BlockSpec((s, hps * dk), lambda i, h: (i, kind * nhb + h))
    cblk = lambda kind: pl.BlockSpec((GDN_CONV, hps * dk), lambda i, h: (0, kind * nhb + h))
    group_rows = min(GDN_PREP_GROUP, s // GDN_CHUNK) * GDN_CHUNK
    seq = lambda: pltpu.VMEM((hps, s, dk), F32)
    slots = lambda: pltpu.VMEM((hps, 2, 2, group_rows, dk), F32)
    return pl.pallas_call(
        _gdn_core_kernel,
        out_shape=jax.ShapeDtypeStruct((b * s, GDN_WIDTH), BF16),
        grid=(b, nhb),
        in_specs=[blk(0), blk(1), blk(2), blk(3),
                  pl.BlockSpec((s, GDN_GATE_LANES), lambda i, h: (i, 0)),
                  cblk(0), cblk(1), cblk(2),
                  pl.BlockSpec((1, dk), lambda i, h: (0, 0))],
        out_specs=pl.BlockSpec((s, hps * dk), lambda i, h: (i, h)),
        scratch_shapes=[pltpu.VMEM((s + 2 * GDN_CONV_PAD, dk), F32), seq(), seq(), seq(),
                        slots(), slots(), slots(), slots(),
                        pltpu.VMEM((hps, 2, group_rows, 2 * GDN_CHUNK), F32),
                        pltpu.VMEM((hps, 2, group_rows // GDN_CHUNK * V7X_SUBLANES, 2 * GDN_CHUNK), F32),
                        seq(), seq()],
        compiler_params=_cparams("parallel", "parallel"),
        name="gdn_core",
    )(proj, proj, proj, proj, gates, conv_w, conv_w, conv_w, norm_w.reshape(1, dk))


def _gdn_mixer_core(xf, w_in, conv_w, a_log, dt_bias, norm_w, b, s):
    n_main = 4 * GDN_WIDTH
    n_gate = 4 * GDN_HEADS
    proj = _linear(xf, w_in[:, :n_main].astype(BF16), BF16)
    w_gate = jnp.pad(w_in[:, n_main:], ((0, 0), (0, GDN_GATE_LANES - n_gate))).astype(BF16)
    lane_pad = lambda v: jnp.pad(v.astype(F32).reshape(1, 2 * GDN_HEADS), ((0, 0), (0, GDN_GATE_LANES - 2 * GDN_HEADS)))
    gates = _gdn_gates(xf, w_gate, lane_pad(a_log), lane_pad(dt_bias), b, s)
    return _gdn_core(proj, gates, conv_w.astype(F32), norm_w.astype(F32), b, s)


def kernel(x, a_w_in, a_conv, a_A_log, a_dt_bias, a_norm_w, a_w_out, b_w_in, b_sink, b_w_out, rel_bias, c_w_in, c_w_group, c_scale, c_w_out, router_w, router_b, moe_w_gate, moe_w_up, moe_w_down, ln_g, ln_b):
    b, s, d = x.shape
    t = b * s
    xf = x.reshape(t, d)
    router_wt = router_w.T
    router_b_col = router_b.reshape(N_EXPERTS, 1).astype(F32)
    bias_tbl = _attn_bias_table(rel_bias)
    for i in range(DEPTH):
        kind, j = i % N_MIXERS, i // N_MIXERS
        if kind == 0:
            h = _gdn_mixer_core(xf, a_w_in[j], a_conv[j], a_A_log[j], a_dt_bias[j], a_norm_w[j], b, s)
            w_out = a_w_out[j]
        elif kind == 1:
            proj = _linear(xf, b_w_in[j].astype(BF16), BF16)
            h = _attn_core(proj, bias_tbl, b_sink[j], b, s)
            w_out = b_w_out[j]
        else:
            u = _linear(xf, c_w_in[j].astype(BF16), F32)
            h = _pool_core(u.reshape(b, s, d), c_w_group[j].astype(BF16), c_scale[j]).reshape(t, d)
            w_out = c_w_out[j]
        x1, route_i, gate_cols, counts = _post_mixer(h, w_out.astype(BF16), xf, ln_g[i, 0], ln_b[i, 0],
                                                     router_wt, router_b_col)
        xf = _moe_layer(x1, route_i, gate_cols, counts[:, 0], moe_w_gate[i], moe_w_up[i], moe_w_down[i],
                        ln_g[i, 1], ln_b[i, 1])
    return xf.reshape(b, s, d)
```

```python
import functools
import math

import jax
import jax.numpy as jnp
from jax import lax
from jax.experimental import pallas as pl
from jax.experimental.pallas import tpu as pltpu

F32 = jnp.float32
BF16 = jnp.bfloat16
I32 = jnp.int32

D_MODEL = 1024
DEPTH = 4
N_MIXERS = 3
DEEPNORM_ALPHA = (2.0 * DEPTH) ** 0.25
LN_EPS = 1e-5

GDN_HEADS = 8
GDN_HEAD_DIM = 128
GDN_WIDTH = GDN_HEADS * GDN_HEAD_DIM
GDN_CONV = 5
GDN_CHUNK = 64

ATT_HEADS = 16
ATT_KV_HEADS = 4
ATT_HEAD_DIM = 64
ATT_GROUP = ATT_HEADS // ATT_KV_HEADS
ATT_WINDOW = 128
ATT_QBLOCK = 128
ATT_KSPAN = ATT_QBLOCK + 2 * ATT_WINDOW
REL_BUCKETS = 32
REL_MAX_DIST = 128

POOL_WINDOWS = (2, 4, 8, 16)
POOL_GROUP_DIM = D_MODEL // len(POOL_WINDOWS)

N_EXPERTS = 16
N_EXPERT_GROUPS = 4
EXPERTS_PER_GROUP = N_EXPERTS // N_EXPERT_GROUPS
TOP_K = 2
EXPERT_DIM = 512

V7X_LANES = 128
V7X_SUBLANES = 8
V7X_VMEM_LIMIT_BYTES = 56 * 1024 * 1024

ROW_TILE = 512
MOE_BLOCK = 512
NEG_BIG = -1e30

HIGHEST = lax.Precision.HIGHEST


def _cparams(*sem):
    return pltpu.CompilerParams(dimension_semantics=tuple(sem), vmem_limit_bytes=V7X_VMEM_LIMIT_BYTES)


def _dot(a, b, **kw):
    return jnp.dot(a, b, preferred_element_type=F32, **kw)


def _dot_nt(a, b, **kw):
    return lax.dot_general(a, b, (((1,), (1,)), ((), ())), preferred_element_type=F32, **kw)


def _dot_tn(a, b, **kw):
    return lax.dot_general(a, b, (((0,), (0,)), ((), ())), preferred_element_type=F32, **kw)


def _silu(x):
    return x * jax.nn.sigmoid(x)


def _linear_kernel(x_ref, w_ref, o_ref):
    o_ref[...] = _dot(x_ref[...].astype(BF16), w_ref[...]).astype(o_ref.dtype)


def _linear(x, w_bf16, out_dtype, tm=ROW_TILE):
    m, k = x.shape
    n = w_bf16.shape[1]
    return pl.pallas_call(
        _linear_kernel,
        out_shape=jax.ShapeDtypeStruct((m, n), out_dtype),
        grid=(m // tm,),
        in_specs=[pl.BlockSpec((tm, k), lambda i: (i, 0)), pl.BlockSpec((k, n), lambda i: (0, 0))],
        out_specs=pl.BlockSpec((tm, n), lambda i: (i, 0)),
        compiler_params=_cparams("parallel"),
        name="linear",
    )(x, w_bf16)


def _layer_norm_rows(s, g, b):
    mu = jnp.mean(s, axis=-1, keepdims=True)
    xc = s - mu
    var = jnp.mean(xc * xc, axis=-1, keepdims=True)
    return xc * lax.rsqrt(var + LN_EPS) * g + b


def _top2_of4(a, b, c, d):
    hi1, lo1 = jnp.maximum(a, b), jnp.minimum(a, b)
    hi2, lo2 = jnp.maximum(c, d), jnp.minimum(c, d)
    return jnp.maximum(hi1, hi2) + jnp.maximum(jnp.minimum(hi1, hi2), jnp.maximum(lo1, lo2))


def _route_rows(scores, biased):
    bi = [biased[e:e + 1, :] for e in range(N_EXPERTS)]
    sc = [scores[e:e + 1, :] for e in range(N_EXPERTS)]
    gs = [_top2_of4(*bi[4 * g:4 * g + 4]) for g in range(N_EXPERT_GROUPS)]
    group = jnp.zeros_like(gs[0], dtype=I32)
    best = gs[0]
    for g in range(1, N_EXPERT_GROUPS):
        upd = gs[g] > best
        group = jnp.where(upd, g, group)
        best = jnp.where(upd, gs[g], best)

    def pick(rows, j):
        out = rows[j]
        for g in range(1, N_EXPERT_GROUPS):
            out = jnp.where(group == g, rows[4 * g + j], out)
        return out

    v = [pick(bi, j) for j in range(EXPERTS_PER_GROUP)]
    s = [pick(sc, j) for j in range(EXPERTS_PER_GROUP)]
    i1 = jnp.zeros_like(group)
    b1 = v[0]
    for j in range(1, EXPERTS_PER_GROUP):
        upd = v[j] > b1
        i1 = jnp.where(upd, j, i1)
        b1 = jnp.where(upd, v[j], b1)
    i2 = jnp.full_like(group, -1)
    b2 = jnp.full_like(b1, -jnp.inf)
    for j in range(EXPERTS_PER_GROUP):
        upd = (i1 != j) & ((v[j] > b2) | (i2 < 0))
        i2 = jnp.where(upd, j, i2)
        b2 = jnp.where(upd, v[j], b2)
    s1 = s[0]
    s2 = s[0]
    for j in range(1, EXPERTS_PER_GROUP):
        s1 = jnp.where(i1 == j, s[j], s1)
        s2 = jnp.where(i2 == j, s[j], s2)
    den = s1 + s2
    return group * EXPERTS_PER_GROUP + i1, group * EXPERTS_PER_GROUP + i2, s1 / den, s2 / den


def _post_mixer_kernel(h_ref, w_ref, x_ref, g_ref, b_ref, rwt_ref, rb_ref, ustrict_ref, eye_ref,
                       x1_ref, ri_ref, gc_ref, cnt_ref, base_ref):
    tm = x_ref.shape[0]

    @pl.when(pl.program_id(0) == 0)
    def _():
        base_ref[...] = jnp.zeros_like(base_ref)

    y = _dot(h_ref[...], w_ref[...])
    x1 = _layer_norm_rows(DEEPNORM_ALPHA * x_ref[...] + y, g_ref[...], b_ref[...])
    x1_ref[...] = x1

    logits = _dot_nt(rwt_ref[...], x1.astype(BF16))
    scores = jax.nn.sigmoid(logits)
    e1, e2, g1, g2 = _route_rows(scores, scores + rb_ref[...])

    eidx = lax.broadcasted_iota(I32, (N_EXPERTS, tm), 0)
    hit1 = eidx == e1
    hit2 = eidx == e2
    onehot = jnp.where(hit1 | hit2, 1.0, 0.0)
    base = base_ref[:, 0:1]
    before = _dot(onehot.astype(BF16), ustrict_ref[...]) + base
    r1 = jnp.sum(jnp.where(hit1, before, 0.0), axis=0, keepdims=True)
    r2 = jnp.sum(jnp.where(hit2, before, 0.0), axis=0, keepdims=True)
    new_base = base + jnp.sum(onehot, axis=1, keepdims=True)
    base_ref[...] = jnp.broadcast_to(new_base, base_ref.shape)
    cnt_ref[...] = jnp.broadcast_to(new_base, cnt_ref.shape)

    ri_ref[...] = jnp.concatenate(
        [e1, e2, r1.astype(I32), r2.astype(I32), jnp.zeros((4, tm), I32)], axis=0)
    gates = jnp.concatenate([g1, g2, jnp.zeros((6, tm), F32)], axis=0)
    hi = gates.astype(BF16)
    lo = (gates - hi.astype(F32)).astype(BF16)
    gc_ref[...] = _dot_nt(eye_ref[...], hi) + _dot_nt(eye_ref[...], lo)


def _post_mixer(h_bf16, w_out_bf16, x, ln_g, ln_b, router_wt, router_b_col, tm=ROW_TILE):
    t, d = x.shape
    ustrict = (jnp.arange(tm)[:, None] < jnp.arange(tm)[None, :]).astype(BF16)
    eye = jnp.eye(tm, dtype=BF16)
    full = lambda shape: pl.BlockSpec(shape, lambda i: (0,) * len(shape))
    return pl.pallas_call(
        _post_mixer_kernel,
        out_shape=(jax.ShapeDtypeStruct((t, d), F32),
                   jax.ShapeDtypeStruct((8, t), I32),
                   jax.ShapeDtypeStruct((t, 8), F32),
                   jax.ShapeDtypeStruct((N_EXPERTS, V7X_LANES), F32)),
        grid=(t // tm,),
        in_specs=[pl.BlockSpec((tm, h_bf16.shape[1]), lambda i: (i, 0)),
                  full(w_out_bf16.shape),
                  pl.BlockSpec((tm, d), lambda i: (i, 0)),
                  full((1, d)), full((1, d)),
                  full((N_EXPERTS, d)), full((N_EXPERTS, 1)),
                  full((tm, tm)), full((tm, tm))],
        out_specs=(pl.BlockSpec((tm, d), lambda i: (i, 0)),
                   pl.BlockSpec((8, tm), lambda i: (0, i)),
                   pl.BlockSpec((tm, 8), lambda i: (i, 0)),
                   pl.BlockSpec((N_EXPERTS, V7X_LANES), lambda i: (0, 0))),
        scratch_shapes=[pltpu.VMEM((N_EXPERTS, V7X_LANES), F32)],
        compiler_params=_cparams("arbitrary"),
        name="post_mixer_router",
    )(h_bf16, w_out_bf16, x, ln_g.reshape(1, d), ln_b.reshape(1, d), router_wt, router_b_col, ustrict, eye)


def _row_copy(src_ref, src_row, dst_ref, dst_row, sem):
    return pltpu.make_async_copy(src_ref.at[pl.ds(src_row, 1), :], dst_ref.at[pl.ds(dst_row, 1), :], sem)


def _dispatch_kernel(pad_lo_ref, pad_hi_ref, x_ref, dest_hbm, xs_hbm, idx_smem, zero_ref, idx_sem, row_sem, pad_sem):
    tm = x_ref.shape[0]
    i = pl.program_id(0)
    idx_copy = pltpu.make_async_copy(dest_hbm.at[i], idx_smem, idx_sem)
    idx_copy.start()
    idx_copy.wait()

    def issue(j, carry):
        for k in range(TOP_K):
            _row_copy(x_ref, j, xs_hbm, idx_smem[k, j], row_sem).start(priority=k)
        return carry

    lax.fori_loop(0, tm, issue, 0, unroll=8)

    def drain(j, carry):
        for k in range(TOP_K):
            _row_copy(x_ref, j, xs_hbm, idx_smem[k, j], row_sem).wait()
        return carry

    lax.fori_loop(0, tm, drain, 0, unroll=8)

    @pl.when(i == pl.num_programs(0) - 1)
    def _():
        zero_ref[...] = jnp.zeros_like(zero_ref)
        for e in range(N_EXPERTS + 1):
            def fill(r, carry):
                _row_copy(zero_ref, 0, xs_hbm, r, pad_sem).start()
                return carry

            lax.fori_loop(pad_lo_ref[e], pad_hi_ref[e], fill, 0)
        for e in range(N_EXPERTS + 1):
            def filled(r, carry):
                _row_copy(zero_ref, 0, xs_hbm, r, pad_sem).wait()
                return carry

            lax.fori_loop(pad_lo_ref[e], pad_hi_ref[e], filled, 0)


def _dispatch(x1, dest_tiles, pad_lo, pad_hi, n_rows, tm=ROW_TILE):
    t, d = x1.shape
    grid_spec = pltpu.PrefetchScalarGridSpec(
        num_scalar_prefetch=2,
        grid=(t // tm,),
        in_specs=[pl.BlockSpec((tm, d), lambda i, lo, hi: (i, 0)),
                  pl.BlockSpec(memory_space=pl.ANY)],
        out_specs=pl.BlockSpec(memory_space=pl.ANY),
        scratch_shapes=[pltpu.SMEM((TOP_K, tm), I32), pltpu.VMEM((V7X_SUBLANES, d), F32),
                        pltpu.SemaphoreType.DMA, pltpu.SemaphoreType.DMA, pltpu.SemaphoreType.DMA],
    )
    return pl.pallas_call(
        _dispatch_kernel,
        out_shape=jax.ShapeDtypeStruct((n_rows, d), F32),
        grid_spec=grid_spec,
        compiler_params=_cparams("arbitrary"),
        name="moe_dispatch",
    )(pad_lo, pad_hi, x1, dest_tiles)


def _expert_kernel(be_ref, xs_ref, wg_ref, wu_ref, wd_ref, ys_ref):
    del be_ref
    x = xs_ref[...].astype(BF16)
    hidden = _silu(_dot(x, wg_ref[0].astype(BF16))) * _dot(x, wu_ref[0].astype(BF16))
    ys_ref[...] = _dot(hidden.astype(BF16), wd_ref[0].astype(BF16))


def _experts(block_expert, xs, wg, wu, wd):
    n_rows, d = xs.shape
    n_blocks = n_rows // MOE_BLOCK
    f = wg.shape[2]
    grid_spec = pltpu.PrefetchScalarGridSpec(
        num_scalar_prefetch=1,
        grid=(n_blocks,),
        in_specs=[pl.BlockSpec((MOE_BLOCK, d), lambda i, be: (i, 0)),
                  pl.BlockSpec((1, d, f), lambda i, be: (be[i], 0, 0)),
                  pl.BlockSpec((1, d, f), lambda i, be: (be[i], 0, 0)),
                  pl.BlockSpec((1, f, d), lambda i, be: (be[i], 0, 0))],
        out_specs=pl.BlockSpec((MOE_BLOCK, d), lambda i, be: (i, 0)),
    )
    return pl.pallas_call(
        _expert_kernel,
        out_shape=jax.ShapeDtypeStruct((n_rows, d), F32),
        grid_spec=grid_spec,
        compiler_params=_cparams("parallel"),
        name="moe_experts",
    )(block_expert, xs, wg, wu, wd)


def _combine_kernel(x1_ref, gc_ref, g_ref, b_ref, dest_hbm, ys_hbm, o_ref, idx_smem, ybuf, idx_sem, row_sem):
    tm = x1_ref.shape[0]
    i = pl.program_id(0)
    idx_copy = pltpu.make_async_copy(dest_hbm.at[i], idx_smem, idx_sem)
    idx_copy.start()
    idx_copy.wait()

    def issue(j, carry):
        for k in range(TOP_K):
            _row_copy(ys_hbm, idx_smem[k, j], ybuf.at[k], j, row_sem).start(priority=k)
        return carry

    lax.fori_loop(0, tm, issue, 0, unroll=8)

    def drain(j, carry):
        for k in range(TOP_K):
            _row_copy(ys_hbm, idx_smem[k, j], ybuf.at[k], j, row_sem).wait()
        return carry

    lax.fori_loop(0, tm, drain, 0, unroll=8)

    gc = gc_ref[...]
    f = ybuf[0] * gc[:, 0:1] + ybuf[1] * gc[:, 1:2]
    o_ref[...] = _layer_norm_rows(DEEPNORM_ALPHA * x1_ref[...] + f, g_ref[...], b_ref[...])


def _combine(x1, gate_cols, ln_g, ln_b, dest_tiles, ys, tm=ROW_TILE):
    t, d = x1.shape
    full = lambda shape: pl.BlockSpec(shape, lambda i: (0,) * len(shape))
    return pl.pallas_call(
        _combine_kernel,
        out_shape=jax.ShapeDtypeStruct((t, d), F32),
        grid=(t // tm,),
        in_specs=[pl.BlockSpec((tm, d), lambda i: (i, 0)),
                  pl.BlockSpec((tm, 8), lambda i: (i, 0)),
                  full((1, d)), full((1, d)),
                  pl.BlockSpec(memory_space=pl.ANY),
                  pl.BlockSpec(memory_space=pl.ANY)],
        out_specs=pl.BlockSpec((tm, d), lambda i: (i, 0)),
        scratch_shapes=[pltpu.SMEM((TOP_K, tm), I32), pltpu.VMEM((TOP_K, tm, d), F32),
                        pltpu.SemaphoreType.DMA, pltpu.SemaphoreType.DMA],
        compiler_params=_cparams("arbitrary"),
        name="moe_combine_ln",
    )(x1, gate_cols, ln_g.reshape(1, d), ln_b.reshape(1, d), dest_tiles, ys)


def _moe_layer(x1, route_i, gate_cols, counts, wg, wu, wd, ln_g, ln_b, tm=ROW_TILE):
    t, d = x1.shape
    n_tiles = t // tm
    n_rows = t * TOP_K + N_EXPERTS * MOE_BLOCK
    counts = counts.astype(I32)
    padded = (counts + MOE_BLOCK - 1) // MOE_BLOCK * MOE_BLOCK
    pends = jnp.cumsum(padded)
    pstarts = pends - padded
    eids = route_i[0:2]
    start_of = jnp.sum(jnp.where(eids[None] == jnp.arange(N_EXPERTS, dtype=I32)[:, None, None],
                                 pstarts[:, None, None], 0), axis=0)
    dest = start_of + route_i[2:4]
    dest_tiles = dest.reshape(TOP_K, n_tiles, tm).transpose(1, 0, 2)
    n_blocks = n_rows // MOE_BLOCK
    block_expert = jnp.minimum(
        jnp.sum((jnp.arange(n_blocks, dtype=I32) * MOE_BLOCK)[:, None] >= pends[None, :], axis=-1),
        N_EXPERTS - 1).astype(I32)
    pad_lo = jnp.concatenate([pstarts + counts, pends[-1:]]).astype(I32)
    pad_hi = jnp.concatenate([pends, jnp.full((1,), n_rows, I32)]).astype(I32)
    xs = _dispatch(x1, dest_tiles, pad_lo, pad_hi, n_rows, tm)
    ys = _experts(block_expert, xs, wg, wu, wd)
    return _combine(x1, gate_cols, ln_g, ln_b, dest_tiles, ys, tm)


POOL_PAD = 16


def _pool_kernel(u_ref, wg_ref, scale_ref, o_ref, pad_ref):
    s, c = u_ref.shape[1], u_ref.shape[2]
    u = u_ref[0]
    zeros = jnp.zeros((POOL_PAD, c), F32)
    pad_ref[pl.ds(0, POOL_PAD), :] = zeros
    pad_ref[pl.ds(POOL_PAD + s, POOL_PAD), :] = zeros
    pad_ref[pl.ds(POOL_PAD, s), :] = u
    pos = lax.broadcasted_iota(I32, (s, c), 0)
    group = pl.program_id(1)
    for gi, win in enumerate(POOL_WINDOWS):
        @pl.when(group == gi)
        def _(win=win):
            half = win // 2
            total = pad_ref[pl.ds(POOL_PAD - half, s), :]
            for j in range(1 - half, half):
                total = total + pad_ref[pl.ds(POOL_PAD + j, s), :]
            count = (jnp.minimum(pos + half, s) - jnp.maximum(pos - half, 0)).astype(F32)
            mixed = total / count - u
            y = _dot(mixed.astype(BF16), wg_ref[0]) * scale_ref[...]
            o_ref[0] = y.astype(o_ref.dtype)


def _pool_core(u, w_group_bf16, scale):
    b, s, d = u.shape
    c = POOL_GROUP_DIM
    return pl.pallas_call(
        _pool_kernel,
        out_shape=jax.ShapeDtypeStruct((b, s, d), BF16),
        grid=(b, d // c),
        in_specs=[pl.BlockSpec((1, s, c), lambda i, g: (i, 0, g)),
                  pl.BlockSpec((1, c, c), lambda i, g: (g, 0, 0)),
                  pl.BlockSpec((1, c), lambda i, g: (0, g))],
        out_specs=pl.BlockSpec((1, s, c), lambda i, g: (i, 0, g)),
        scratch_shapes=[pltpu.VMEM((s + 2 * POOL_PAD, c), F32)],
        compiler_params=_cparams("parallel", "parallel"),
        name="pool_core",
    )(u, w_group_bf16, scale.reshape(1, d))


def _t5_bucket(rel):
    half = REL_BUCKETS // 2
    max_exact = half // 2
    n = jnp.abs(rel)
    large = max_exact + (jnp.log(jnp.maximum(n, 1).astype(F32) / max_exact)
                         / math.log(REL_MAX_DIST / max_exact) * (half - max_exact)).astype(I32)
    large = jnp.minimum(large, half - 1)
    return (rel > 0).astype(I32) * half + jnp.where(n < max_exact, n, large)


def _attn_bias_table(rel_bias):
    rel = jnp.arange(ATT_KSPAN)[None, :] - ATT_WINDOW - jnp.arange(ATT_QBLOCK)[:, None]
    bias = jnp.transpose(rel_bias[_t5_bucket(rel)], (2, 0, 1)).astype(F32)
    return jnp.where((jnp.abs(rel) <= ATT_WINDOW)[None], bias, NEG_BIG)


def _attn_kernel(q_ref, kp_ref, kc_ref, kn_ref, vp_ref, vc_ref, vn_ref, bias_ref, sink_ref, o_ref):
    j = pl.program_id(1)
    nb = pl.num_programs(1)
    qb = ATT_QBLOCK
    col = lax.broadcasted_iota(I32, (qb, ATT_KSPAN), 1)
    edge = jnp.where(((j == 0) & (col < qb)) | ((j == nb - 1) & (col >= 2 * qb)), NEG_BIG, 0.0)
    k_all = jnp.concatenate([kp_ref[...], kc_ref[...], kn_ref[...]], axis=0)
    v_all = jnp.concatenate([vp_ref[...], vc_ref[...], vn_ref[...]], axis=0)
    q = q_ref[...]
    heads = lambda g: range(g * ATT_GROUP, (g + 1) * ATT_GROUP)
    cols = lambda i: slice(i * ATT_HEAD_DIM, (i + 1) * ATT_HEAD_DIM)

    def scores(g):
        return [_dot_nt(q[:, cols(hd)], k_all[:, cols(g)]) for hd in heads(g)]

    def attend(g, raw):
        outs = []
        probs, dens = [], []
        for hd, qk in zip(heads(g), raw):
            logits = qk * (ATT_HEAD_DIM ** -0.5) + bias_ref[hd] + edge
            sink = sink_ref[hd]
            mx = jnp.maximum(jnp.max(logits, axis=-1, keepdims=True), sink)
            p = jnp.exp(logits - mx)
            dens.append(jnp.sum(p, axis=-1, keepdims=True) + jnp.exp(sink - mx))
            probs.append(p.astype(BF16))
        for p, den in zip(probs, dens):
            outs.append(_dot(p, v_all[:, cols(g)]) / den)
        return outs

    outs = []
    raw = scores(0)
    for g in range(ATT_KV_HEADS):
        nxt = scores(g + 1) if g + 1 < ATT_KV_HEADS else None
        outs += attend(g, raw)
        raw = nxt
    o_ref[...] = jnp.concatenate(outs, axis=-1).astype(o_ref.dtype)


def _attn_core(proj, bias_tbl, sink, b, s):
    qw = ATT_HEADS * ATT_HEAD_DIM
    kw = ATT_KV_HEADS * ATT_HEAD_DIM
    nb = s // ATT_QBLOCK
    kcol, vcol = qw // kw, qw // kw + 1
    row = lambda i, j: i * nb + j
    prev = lambda i, j: i * nb + jnp.maximum(j - 1, 0)
    nxt = lambda i, j: i * nb + jnp.minimum(j + 1, nb - 1)
    kv_spec = lambda rowf, c: pl.BlockSpec((ATT_QBLOCK, kw), lambda i, j: (rowf(i, j), c))
    return pl.pallas_call(
        _attn_kernel,
        out_shape=jax.ShapeDtypeStruct((b * s, qw), BF16),
        grid=(b, nb),
        in_specs=[pl.BlockSpec((ATT_QBLOCK, qw), lambda i, j: (row(i, j), 0)),
                  kv_spec(prev, kcol), kv_spec(row, kcol), kv_spec(nxt, kcol),
                  kv_spec(prev, vcol), kv_spec(row, vcol), kv_spec(nxt, vcol),
                  pl.BlockSpec((ATT_HEADS, ATT_QBLOCK, ATT_KSPAN), lambda i, j: (0, 0, 0)),
                  pl.BlockSpec(memory_space=pltpu.SMEM)],
        out_specs=pl.BlockSpec((ATT_QBLOCK, qw), lambda i, j: (row(i, j), 0)),
        compiler_params=_cparams("parallel", "parallel"),
        name="attn_core",
    )(proj, proj, proj, proj, proj, proj, proj, bias_tbl, sink.astype(F32))


GDN_GATE_LANES = V7X_LANES
GDN_CONV_PAD = 8
GDN_PREP_GROUP = 8
GDN_HEADS_PER_STEP = 2
GDN_PREP_MATMULS = 10


def _softplus(x):
    return jnp.maximum(x, 0.0) + jnp.log(1.0 + jnp.exp(-jnp.abs(x)))


def _gdn_gates_kernel(x_ref, w_ref, alog_ref, dtb_ref, o_ref):
    s = x_ref.shape[0]
    c = GDN_CHUNK
    gl = _dot(x_ref[...].astype(BF16), w_ref[...])
    decay = -jnp.exp(alog_ref[...]) * _softplus(gl + dtb_ref[...])
    beta = jax.nn.sigmoid(gl)
    r = lax.broadcasted_iota(I32, (c, c), 0)
    q = lax.broadcasted_iota(I32, (c, c), 1)
    lower = jnp.where(r >= q, 1.0, 0.0)
    upper = jnp.where(r <= q, 1.0, 0.0)
    lane = lax.broadcasted_iota(I32, (c, GDN_GATE_LANES), 1)
    for n in range(s // c):
        d_c = decay[n * c:(n + 1) * c]
        pre = _dot(lower, d_c, precision=HIGHEST)
        suf = _dot(upper, d_c, precision=HIGHEST)
        o_ref[pl.ds(n * c, c), :] = jnp.where(lane < GDN_HEADS, pre,
                                              jnp.where(lane < 2 * GDN_HEADS, suf, beta[n * c:(n + 1) * c]))


def _gdn_gates(xf, w_gate_bf16, alog_vec, dtb_vec, b, s):
    d = xf.shape[1]
    full = lambda shape: pl.BlockSpec(shape, lambda i: (0,) * len(shape))
    return pl.pallas_call(
        _gdn_gates_kernel,
        out_shape=jax.ShapeDtypeStruct((b * s, GDN_GATE_LANES), F32),
        grid=(b,),
        in_specs=[pl.BlockSpec((s, d), lambda i: (i, 0)), full((d, GDN_GATE_LANES)),
                  full((1, GDN_GATE_LANES)), full((1, GDN_GATE_LANES))],
        out_specs=pl.BlockSpec((s, GDN_GATE_LANES), lambda i: (i, 0)),
        compiler_params=_cparams("parallel"),
        name="gdn_gates",
    )(xf, w_gate_bf16, alog_vec, dtb_vec)


def _block_diag2(r, is_b):
    return jnp.concatenate([jnp.where(is_b, 0.0, r), jnp.where(is_b, r, 0.0)], axis=0)


def _gdn_core_kernel(q_ref, k_ref, v_ref, z_ref, gt_ref, cq_ref, ck_ref, cv_ref, nw_ref, o_ref,
                     pad_ref, qn_ref, kn_ref, vv_ref, u_ref, w_ref, qd_ref, kd_ref, aqk_ref, dend_ref, of_ref, ob_ref):
    s = q_ref.shape[0]
    dk = GDN_HEAD_DIM
    hps = q_ref.shape[1] // dk
    c = GDN_CHUNK
    n_chunks = s // c
    first_head = pl.program_id(1) * hps

    zeros = jnp.zeros((GDN_CONV_PAD, dk), F32)
    pad_ref[pl.ds(0, GDN_CONV_PAD), :] = zeros
    pad_ref[pl.ds(GDN_CONV_PAD + s, GDN_CONV_PAD), :] = zeros
    slab = min(s, 256)
    for hd in range(hps):
        cols = slice(hd * dk, (hd + 1) * dk)
        for src, cw_ref, dst, mode in ((q_ref, cq_ref, qn_ref, "q"), (k_ref, ck_ref, kn_ref, "k"), (v_ref, cv_ref, vv_ref, "v")):
            pad_ref[pl.ds(GDN_CONV_PAD, s), :] = src[:, cols].astype(F32)
            cw = cw_ref[:, cols]
            for r0 in range(0, s, slab):
                acc = None
                for j in range(GDN_CONV):
                    term = pad_ref[pl.ds(GDN_CONV_PAD + r0 + j - GDN_CONV // 2, slab), :] * cw[j:j + 1, :]
                    acc = term if acc is None else acc + term
                y = _silu(acc)
                if mode != "v":
                    y = y * lax.rsqrt(jnp.sum(y * y, axis=-1, keepdims=True) + 1e-6)
                if mode == "q":
                    y = y * (GDN_HEAD_DIM ** -0.5)
                dst[hd, pl.ds(r0, slab), :] = y

    lane = lax.broadcasted_iota(I32, (c, 2 * c), 1)
    row = lax.broadcasted_iota(I32, (c, 2 * c), 0)
    is_b = lane >= c
    col = jnp.where(is_b, lane - c, lane)
    ahead = jnp.where(is_b, col - row, row - col)
    incl = ahead >= 0
    strict = ahead > 0
    eye2 = jnp.where(row == col, 1.0, 0.0)
    shifts = [lax.rem(2 * V7X_LANES - first_head - hd, V7X_LANES) for hd in range(hps)]

    group = min(GDN_PREP_GROUP, n_chunks)
    n_groups = n_chunks // group

    def pair_rows(t):
        if isinstance(t, int):
            return pl.ds(t * c, c), pl.ds((n_chunks - 1 - t) * c, c)
        return pl.ds(pl.multiple_of(t * c, c), c), pl.ds(pl.multiple_of((n_chunks - 1 - t) * c, c), c)

    def prep_stages(hd, pg, slot):
        rows = [pair_rows(pg * group + i) for i in range(group)]
        q_f, k_f, v_f = ([ref[hd, r[0], :] for r in rows] for ref in (qn_ref, kn_ref, vv_ref))
        q_b, k_b, v_b = ([ref[hd, r[1], :] for r in rows] for ref in (qn_ref, kn_ref, vv_ref))
        gt_f = [pltpu.roll(gt_ref[r[0], :], shifts[hd], axis=1) for r in rows]
        gt_b = [pltpu.roll(gt_ref[r[1], :], shifts[hd], axis=1) for r in rows]
        g_f = [t[:, 0:1] for t in gt_f]
        b_f = [t[:, 2 * GDN_HEADS:2 * GDN_HEADS + 1] for t in gt_f]
        g_b = [t[:, GDN_HEADS:GDN_HEADS + 1] for t in gt_b]
        b_b = [t[:, 3 * GDN_HEADS:3 * GDN_HEADS + 1] for t in gt_b]
        dmat, prod_f, prod_b = [], [], []
        for i in range(group):
            a_mat = jnp.where(lane == 0, g_f[i], jnp.where(lane == 2, g_b[i],
                                                           jnp.where((lane == 1) | (lane == 3), 1.0, 0.0)))
            b_top = jnp.where(lane == 0, 1.0, jnp.where(lane == 1, -g_f[i], 0.0))
            b_bot = jnp.where(lane == 2, 1.0, jnp.where(lane == 3, -g_b[i], 0.0))
            dmat.append(_dot_nt(a_mat, jnp.concatenate([b_top, b_bot], axis=0), precision=HIGHEST))
            yield
        for i in range(group):
            kfb, kbb = k_f[i].astype(BF16), k_b[i].astype(BF16)
            keys = jnp.concatenate([kfb, kbb], axis=0)
            prod_f.append(_dot_nt(jnp.concatenate([q_f[i].astype(BF16), kfb], axis=0), keys))
            yield
            prod_b.append(_dot_nt(jnp.concatenate([q_b[i].astype(BF16), kbb], axis=0), keys))
            yield
        qk2 = [jnp.where(is_b, prod_b[i][:c], prod_f[i][:c]) for i in range(group)]
        kk2 = [jnp.where(is_b, prod_b[i][c:], prod_f[i][c:]) for i in range(group)]
        gamma = [jnp.exp(jnp.where(incl, d_i, NEG_BIG)) for d_i in dmat]
        m = [jnp.where(strict, kk2[i] * gamma[i], 0.0) * jnp.where(is_b, b_b[i], b_f[i]) for i in range(group)]
        x_inv = [eye2 - m_i for m_i in m]
        p = []
        for m_i in m:
            p.append(_dot(m_i.astype(BF16), _block_diag2(m_i, is_b).astype(BF16)))
            yield
        for _ in range(5):
            y = []
            for i in range(group):
                y.append(_dot(jnp.concatenate([x_inv[i], p[i]], axis=0).astype(BF16),
                              _block_diag2(p[i], is_b).astype(BF16)))
                yield
            x_inv = [x_inv[i] + y[i][:c] for i in range(group)]
            p = [y_i[c:] for y_i in y]
        eg_f = [jnp.exp(g) for g in g_f]
        eg_b = [jnp.exp(g) for g in g_b]
        sol = []
        for i in range(group):
            rhs = jnp.concatenate([jnp.concatenate([v_f[i] * b_f[i], k_f[i] * (b_f[i] * eg_f[i])], axis=1),
                                   jnp.concatenate([v_b[i] * b_b[i], k_b[i] * (b_b[i] * eg_b[i])], axis=1)], axis=0)
            sol.append(_dot(_block_diag2(x_inv[i], is_b).astype(BF16), rhs.astype(BF16)))
            yield
        for i in range(group):
            local = pl.ds(i * c, c)
            gl_f, gl_b = g_f[i][c - 1:c, :], g_b[i][0:1, :]
            for d, (qc, kc, eg, gl, g_d) in enumerate(((q_f[i], k_f[i], eg_f[i], gl_f, g_f[i]),
                                                       (q_b[i], k_b[i], eg_b[i], gl_b, g_b[i]))):
                u_ref[hd, slot, d, local, :] = sol[i][d * c:(d + 1) * c, :dk]
                w_ref[hd, slot, d, local, :] = sol[i][d * c:(d + 1) * c, dk:]
                qd_ref[hd, slot, d, local, :] = qc * eg
                kd_ref[hd, slot, d, local, :] = kc * jnp.exp(gl - g_d)
            aqk_ref[hd, slot, local, :] = jnp.where(incl, qk2[i] * gamma[i], 0.0)
            dend_ref[hd, slot, pl.ds(i * V7X_SUBLANES, V7X_SUBLANES), :] = jnp.where(
                is_b[:V7X_SUBLANES], jnp.exp(gl_b), jnp.exp(gl_f))

    lanes = (slice(0, c), slice(c, 2 * c))
    chains = [(hd, d) for hd in range(hps) for d in range(2)]

    def run_group(pg, slot, prep_gens, states):
        per_gap = -(-GDN_PREP_MATMULS // 2)

        def advance():
            for _ in range(per_gap):
                for gen in prep_gens:
                    next(gen, None)

        for i in range(group):
            rows = pair_rows(pg * group + i)
            local = pl.ds(i * c, c)
            stb = [st.astype(BF16) for st in states]
            ws = [_dot(w_ref[hd, slot, d, local, :].astype(BF16), stb[n]) for n, (hd, d) in enumerate(chains)]
            qs = [_dot(qd_ref[hd, slot, d, local, :].astype(BF16), stb[n]) for n, (hd, d) in enumerate(chains)]
            advance()
            vnb = [(u_ref[hd, slot, d, local, :] - ws[n]).astype(BF16) for n, (hd, d) in enumerate(chains)]
            av = [_dot(aqk_ref[hd, slot, local, :][:, lanes[d]].astype(BF16), vnb[n])
                  for n, (hd, d) in enumerate(chains)]
            kv = [_dot_tn(kd_ref[hd, slot, d, local, :].astype(BF16), vnb[n]) for n, (hd, d) in enumerate(chains)]
            advance()
            new_states = []
            for n, (hd, d) in enumerate(chains):
                (of_ref, ob_ref)[d][hd, rows[d], :] = qs[n] + av[n]
                dend = dend_ref[hd, slot, pl.ds(i * V7X_SUBLANES, 1), :][:, d * c:d * c + 1]
                new_states.append(states[n] * dend + kv[n])
            states = tuple(new_states)
        for gen in prep_gens:
            for _ in gen:
                pass
        return states

    for hd in range(hps):
        for _ in prep_stages(hd, 0, 0):
            pass

    def fused(tg, states):
        slot = lax.rem(tg, 2)
        return run_group(tg, slot, [prep_stages(hd, tg + 1, 1 - slot) for hd in range(hps)], states)

    zero_state = jnp.zeros((dk, dk), F32)
    states = lax.fori_loop(0, n_groups - 1, fused, (zero_state,) * len(chains))
    run_group(n_groups - 1, (n_groups - 1) % 2, [], states)

    for hd in range(hps):
        cols = slice(hd * dk, (hd + 1) * dk)
        for r0 in range(0, s, slab):
            rs = pl.ds(r0, slab)
            o = of_ref[hd, rs, :] + ob_ref[hd, rs, :]
            o = o * lax.rsqrt(jnp.mean(o * o, axis=-1, keepdims=True) + 1e-6) * nw_ref[...]
            o_ref[rs, cols] = (o * _silu(z_ref[rs, cols].astype(F32))).astype(o_ref.dtype)


def _gdn_core(proj, gates, conv_w, norm_w, b, s):
    dk = GDN_HEAD_DIM
    hps = GDN_HEADS_PER_STEP
    nhb = GDN_HEADS // hps
    blk = lambda kind: pl.BlockSpec((s, hps * dk), lambda i, h: (i, kind * nhb + h))
    cblk = lambda kind: pl.BlockSpec((GDN_CONV, hps * dk), lambda i, h: (0, kind * nhb + h))
    group_rows = min(GDN_PREP_GROUP, s // GDN_CHUNK) * GDN_CHUNK
    seq = lambda: pltpu.VMEM((hps, s, dk), F32)
    slots = lambda: pltpu.VMEM((hps, 2, 2, group_rows, dk), F32)
    return pl.pallas_call(
        _gdn_core_kernel,
        out_shape=jax.ShapeDtypeStruct((b * s, GDN_WIDTH), BF16),
        grid=(b, nhb),
        in_specs=[blk(0), blk(1), blk(2), blk(3),
                  pl.BlockSpec((s, GDN_GATE_LANES), lambda i, h: (i, 0)),
                  cblk(0), cblk(1), cblk(2),
                  pl.BlockSpec((1, dk), lambda i, h: (0, 0))],
        out_specs=pl.BlockSpec((s, hps * dk), lambda i, h: (i, h)),
        scratch_shapes=[pltpu.VMEM((s + 2 * GDN_CONV_PAD, dk), F32), seq(), seq(), seq(),
                        slots(), slots(), slots(), slots(),
                        pltpu.VMEM((hps, 2, group_rows, 2 * GDN_CHUNK), F32),
                        pltpu.VMEM((hps, 2, group_rows // GDN_CHUNK * V7X_SUBLANES, 2 * GDN_CHUNK), F32),
                        seq(), seq()],
        compiler_params=_cparams("parallel", "parallel"),
        name="gdn_core",
    )(proj, proj, proj, proj, gates, conv_w, conv_w, conv_w, norm_w.reshape(1, dk))


def _gdn_mixer_core(xf, w_in, conv_w, a_log, dt_bias, norm_w, b, s):
    n_main = 4 * GDN_WIDTH
    n_gate = 4 * GDN_HEADS
    proj = _linear(xf, w_in[:, :n_main].astype(BF16), BF16)
    w_gate = jnp.pad(w_in[:, n_main:], ((0, 0), (0, GDN_GATE_LANES - n_gate))).astype(BF16)
    lane_pad = lambda v: jnp.pad(v.astype(F32).reshape(1, 2 * GDN_HEADS), ((0, 0), (0, GDN_GATE_LANES - 2 * GDN_HEADS)))
    gates = _gdn_gates(xf, w_gate, lane_pad(a_log), lane_pad(dt_bias), b, s)
    return _gdn_core(proj, gates, conv_w.astype(F32), norm_w.astype(F32), b, s)


def kernel(x, a_w_in, a_conv, a_A_log, a_dt_bias, a_norm_w, a_w_out, b_w_in, b_sink, b_w_out, rel_bias, c_w_in, c_w_group, c_scale, c_w_out, router_w, router_b, moe_w_gate, moe_w_up, moe_w_down, ln_g, ln_b):
    b, s, d = x.shape
    t = b * s
    xf = x.reshape(t, d)
    router_wt = router_w.T.astype(BF16)
    router_b_col = router_b.reshape(N_EXPERTS, 1).astype(F32)
    bias_tbl = _attn_bias_table(rel_bias)
    for i in range(DEPTH):
        kind, j = i % N_MIXERS, i // N_MIXERS
        if kind == 0:
            h = _gdn_mixer_core(xf, a_w_in[j], a_conv[j], a_A_log[j], a_dt_bias[j], a_norm_w[j], b, s)
            w_out = a_w_out[j]
        elif kind == 1:
            proj = _linear(xf, b_w_in[j].astype(BF16), BF16)
            h = _attn_core(proj, bias_tbl, b_sink[j], b, s)
            w_out = b_w_out[j]
        else:
            u = _linear(xf, c_w_in[j].astype(BF16), F32)
            h = _pool_core(u.reshape(b, s, d), c_w_group[j].astype(BF16), c_scale[j]).reshape(t, d)
            w_out = c_w_out[j]
        x1, route_i, gate_cols, counts = _post_mixer(h, w_out.astype(BF16), xf, ln_g[i, 0], ln_b[i, 0],
                                                     router_wt, router_b_col)
        xf = _moe_layer(x1, route_i, gate_cols, counts[:, 0], moe_w_gate[i], moe_w_up[i], moe_w_down[i],
                        ln_g[i, 1], ln_b[i, 1])
    return xf.reshape(b, s, d)
```

```python
import functools
import math

import jax
import jax.numpy as jnp
from jax import lax
from jax.experimental import pallas as pl
from jax.experimental.pallas import tpu as pltpu

F32 = jnp.float32
BF16 = jnp.bfloat16
I32 = jnp.int32

D_MODEL = 1024
DEPTH = 4
N_MIXERS = 3
DEEPNORM_ALPHA = (2.0 * DEPTH) ** 0.25
LN_EPS = 1e-5

GDN_HEADS = 8
GDN_HEAD_DIM = 128
GDN_WIDTH = GDN_HEADS * GDN_HEAD_DIM
GDN_CONV = 5
GDN_CHUNK = 64

ATT_HEADS = 16
ATT_KV_HEADS = 4
ATT_HEAD_DIM = 64
ATT_GROUP = ATT_HEADS // ATT_KV_HEADS
ATT_WINDOW = 128
ATT_QBLOCK = 128
ATT_KSPAN = ATT_QBLOCK + 2 * ATT_WINDOW
REL_BUCKETS = 32
REL_MAX_DIST = 128

POOL_WINDOWS = (2, 4, 8, 16)
POOL_GROUP_DIM = D_MODEL // len(POOL_WINDOWS)

N_EXPERTS = 16
N_EXPERT_GROUPS = 4
EXPERTS_PER_GROUP = N_EXPERTS // N_EXPERT_GROUPS
TOP_K = 2
EXPERT_DIM = 512

V7X_LANES = 128
V7X_SUBLANES = 8
V7X_VMEM_LIMIT_BYTES = 56 * 1024 * 1024

ROW_TILE = 512
MOE_BLOCK = 512
NEG_BIG = -1e30

HIGHEST = lax.Precision.HIGHEST


def _cparams(*sem):
    return pltpu.CompilerParams(dimension_semantics=tuple(sem), vmem_limit_bytes=V7X_VMEM_LIMIT_BYTES)


def _dot(a, b, **kw):
    return jnp.dot(a, b, preferred_element_type=F32, **kw)


def _dot_nt(a, b, **kw):
    return lax.dot_general(a, b, (((1,), (1,)), ((), ())), preferred_element_type=F32, **kw)


def _dot_tn(a, b, **kw):
    return lax.dot_general(a, b, (((0,), (0,)), ((), ())), preferred_element_type=F32, **kw)


def _silu(x):
    return x * jax.nn.sigmoid(x)


def _linear_kernel(x_ref, w_ref, o_ref):
    o_ref[...] = _dot(x_ref[...].astype(BF16), w_ref[...]).astype(o_ref.dtype)


def _linear(x, w_bf16, out_dtype, tm=ROW_TILE):
    m, k = x.shape
    n = w_bf16.shape[1]
    return pl.pallas_call(
        _linear_kernel,
        out_shape=jax.ShapeDtypeStruct((m, n), out_dtype),
        grid=(m // tm,),
        in_specs=[pl.BlockSpec((tm, k), lambda i: (i, 0)), pl.BlockSpec((k, n), lambda i: (0, 0))],
        out_specs=pl.BlockSpec((tm, n), lambda i: (i, 0)),
        compiler_params=_cparams("parallel"),
        name="linear",
    )(x, w_bf16)


def _layer_norm_rows(s, g, b):
    mu = jnp.mean(s, axis=-1, keepdims=True)
    xc = s - mu
    var = jnp.mean(xc * xc, axis=-1, keepdims=True)
    return xc * lax.rsqrt(var + LN_EPS) * g + b


def _top2_of4(a, b, c, d):
    hi1, lo1 = jnp.maximum(a, b), jnp.minimum(a, b)
    hi2, lo2 = jnp.maximum(c, d), jnp.minimum(c, d)
    return jnp.maximum(hi1, hi2) + jnp.maximum(jnp.minimum(hi1, hi2), jnp.maximum(lo1, lo2))


def _route_rows(scores, biased):
    bi = [biased[e:e + 1, :] for e in range(N_EXPERTS)]
    sc = [scores[e:e + 1, :] for e in range(N_EXPERTS)]
    gs = [_top2_of4(*bi[4 * g:4 * g + 4]) for g in range(N_EXPERT_GROUPS)]
    group = jnp.zeros_like(gs[0], dtype=I32)
    best = gs[0]
    for g in range(1, N_EXPERT_GROUPS):
        upd = gs[g] > best
        group = jnp.where(upd, g, group)
        best = jnp.where(upd, gs[g], best)

    def pick(rows, j):
        out = rows[j]
        for g in range(1, N_EXPERT_GROUPS):
            out = jnp.where(group == g, rows[4 * g + j], out)
        return out

    v = [pick(bi, j) for j in range(EXPERTS_PER_GROUP)]
    s = [pick(sc, j) for j in range(EXPERTS_PER_GROUP)]
    i1 = jnp.zeros_like(group)
    b1 = v[0]
    for j in range(1, EXPERTS_PER_GROUP):
        upd = v[j] > b1
        i1 = jnp.where(upd, j, i1)
        b1 = jnp.where(upd, v[j], b1)
    i2 = jnp.full_like(group, -1)
    b2 = jnp.full_like(b1, -jnp.inf)
    for j in range(EXPERTS_PER_GROUP):
        upd = (i1 != j) & ((v[j] > b2) | (i2 < 0))
        i2 = jnp.where(upd, j, i2)
        b2 = jnp.where(upd, v[j], b2)
    s1 = s[0]
    s2 = s[0]
    for j in range(1, EXPERTS_PER_GROUP):
        s1 = jnp.where(i1 == j, s[j], s1)
        s2 = jnp.where(i2 == j, s[j], s2)
    den = s1 + s2
    return group * EXPERTS_PER_GROUP + i1, group * EXPERTS_PER_GROUP + i2, s1 / den, s2 / den


def _post_mixer_kernel(h_ref, w_ref, x_ref, g_ref, b_ref, rwt_ref, rb_ref, ustrict_ref, eye_ref,
                       x1_ref, ri_ref, gc_ref, cnt_ref, base_ref):
    tm = x_ref.shape[0]

    @pl.when(pl.program_id(0) == 0)
    def _():
        base_ref[...] = jnp.zeros_like(base_ref)

    y = _dot(h_ref[...], w_ref[...])
    x1 = _layer_norm_rows(DEEPNORM_ALPHA * x_ref[...] + y, g_ref[...], b_ref[...])
    x1_ref[...] = x1

    logits = _dot_nt(rwt_ref[...], x1.astype(BF16))
    scores = jax.nn.sigmoid(logits)
    e1, e2, g1, g2 = _route_rows(scores, scores + rb_ref[...])

    eidx = lax.broadcasted_iota(I32, (N_EXPERTS, tm), 0)
    hit1 = eidx == e1
    hit2 = eidx == e2
    onehot = jnp.where(hit1 | hit2, 1.0, 0.0)
    base = base_ref[:, 0:1]
    before = _dot(onehot.astype(BF16), ustrict_ref[...]) + base
    r1 = jnp.sum(jnp.where(hit1, before, 0.0), axis=0, keepdims=True)
    r2 = jnp.sum(jnp.where(hit2, before, 0.0), axis=0, keepdims=True)
    new_base = base + jnp.sum(onehot, axis=1, keepdims=True)
    base_ref[...] = jnp.broadcast_to(new_base, base_ref.shape)
    cnt_ref[...] = jnp.broadcast_to(new_base, cnt_ref.shape)

    ri_ref[...] = jnp.concatenate(
        [e1, e2, r1.astype(I32), r2.astype(I32), jnp.zeros((4, tm), I32)], axis=0)
    gates = jnp.concatenate([g1, g2, jnp.zeros((6, tm), F32)], axis=0)
    hi = gates.astype(BF16)
    lo = (gates - hi.astype(F32)).astype(BF16)
    gc_ref[...] = _dot_nt(eye_ref[...], hi) + _dot_nt(eye_ref[...], lo)


def _post_mixer(h_bf16, w_out_bf16, x, ln_g, ln_b, router_wt, router_b_col, tm=ROW_TILE):
    t, d = x.shape
    ustrict = (jnp.arange(tm)[:, None] < jnp.arange(tm)[None, :]).astype(BF16)
    eye = jnp.eye(tm, dtype=BF16)
    full = lambda shape: pl.BlockSpec(shape, lambda i: (0,) * len(shape))
    return pl.pallas_call(
        _post_mixer_kernel,
        out_shape=(jax.ShapeDtypeStruct((t, d), F32),
                   jax.ShapeDtypeStruct((8, t), I32),
                   jax.ShapeDtypeStruct((t, 8), F32),
                   jax.ShapeDtypeStruct((N_EXPERTS, V7X_LANES), F32)),
        grid=(t // tm,),
        in_specs=[pl.BlockSpec((tm, h_bf16.shape[1]), lambda i: (i, 0)),
                  full(w_out_bf16.shape),
                  pl.BlockSpec((tm, d), lambda i: (i, 0)),
                  full((1, d)), full((1, d)),
                  full((N_EXPERTS, d)), full((N_EXPERTS, 1)),
                  full((tm, tm)), full((tm, tm))],
        out_specs=(pl.BlockSpec((tm, d), lambda i: (i, 0)),
                   pl.BlockSpec((8, tm), lambda i: (0, i)),
                   pl.BlockSpec((tm, 8), lambda i: (i, 0)),
                   pl.BlockSpec((N_EXPERTS, V7X_LANES), lambda i: (0, 0))),
        scratch_shapes=[pltpu.VMEM((N_EXPERTS, V7X_LANES), F32)],
        compiler_params=_cparams("arbitrary"),
        name="post_mixer_router",
    )(h_bf16, w_out_bf16, x, ln_g.reshape(1, d), ln_b.reshape(1, d), router_wt, router_b_col, ustrict, eye)


def _row_copy(src_ref, src_row, dst_ref, dst_row, sem):
    return pltpu.make_async_copy(src_ref.at[pl.ds(src_row, 1), :], dst_ref.at[pl.ds(dst_row, 1), :], sem)


def _dispatch_kernel(n_tiles, pad_lo_ref, pad_hi_ref, x_ref, dest_hbm, xs_hbm,
                     idx_smem, zero_ref, idx_sem, row_sem, pad_sem):
    tm = x_ref.shape[0]
    i = pl.program_id(0)
    slot = lax.rem(i, 2)

    def idx_copy(tile, sl):
        return pltpu.make_async_copy(dest_hbm.at[tile], idx_smem.at[sl], idx_sem.at[sl])

    @pl.when(i == 0)
    def _():
        idx_copy(0, 0).start()

    idx_copy(i, slot).wait()

    @pl.when(i + 1 < n_tiles)
    def _():
        idx_copy(i + 1, 1 - slot).start()

    def issue(j, carry):
        for k in range(TOP_K):
            _row_copy(x_ref, j, xs_hbm, idx_smem[slot, k, j], row_sem).start(priority=k)
        return carry

    lax.fori_loop(0, tm, issue, 0, unroll=8)

    def drain(j, carry):
        for k in range(TOP_K):
            _row_copy(x_ref, j, xs_hbm, idx_smem[slot, k, j], row_sem).wait()
        return carry

    lax.fori_loop(0, tm, drain, 0, unroll=8)

    @pl.when(i == pl.num_programs(0) - 1)
    def _():
        zero_ref[...] = jnp.zeros_like(zero_ref)
        for e in range(N_EXPERTS + 1):
            def fill(r, carry):
                _row_copy(zero_ref, 0, xs_hbm, r, pad_sem).start()
                return carry

            lax.fori_loop(pad_lo_ref[e], pad_hi_ref[e], fill, 0)
        for e in range(N_EXPERTS + 1):
            def filled(r, carry):
                _row_copy(zero_ref, 0, xs_hbm, r, pad_sem).wait()
                return carry

            lax.fori_loop(pad_lo_ref[e], pad_hi_ref[e], filled, 0)


def _dispatch(x1, dest_tiles, pad_lo, pad_hi, n_rows, tm=ROW_TILE):
    t, d = x1.shape
    grid_spec = pltpu.PrefetchScalarGridSpec(
        num_scalar_prefetch=2,
        grid=(t // tm,),
        in_specs=[pl.BlockSpec((tm, d), lambda i, lo, hi: (i, 0)),
                  pl.BlockSpec(memory_space=pl.ANY)],
        out_specs=pl.BlockSpec(memory_space=pl.ANY),
        scratch_shapes=[pltpu.SMEM((2, TOP_K, tm), I32), pltpu.VMEM((V7X_SUBLANES, d), F32),
                        pltpu.SemaphoreType.DMA((2,)), pltpu.SemaphoreType.DMA, pltpu.SemaphoreType.DMA],
    )
    return pl.pallas_call(
        functools.partial(_dispatch_kernel, t // tm),
        out_shape=jax.ShapeDtypeStruct((n_rows, d), F32),
        grid_spec=grid_spec,
        compiler_params=_cparams("arbitrary"),
        name="moe_dispatch",
    )(pad_lo, pad_hi, x1, dest_tiles)


def _expert_kernel(be_ref, xs_ref, wg_ref, wu_ref, wd_ref, ys_ref):
    del be_ref
    x = xs_ref[...].astype(BF16)
    hidden = _silu(_dot(x, wg_ref[0].astype(BF16))) * _dot(x, wu_ref[0].astype(BF16))
    ys_ref[...] = _dot(hidden.astype(BF16), wd_ref[0].astype(BF16))


def _experts(block_expert, xs, wg, wu, wd):
    n_rows, d = xs.shape
    n_blocks = n_rows // MOE_BLOCK
    f = wg.shape[2]
    grid_spec = pltpu.PrefetchScalarGridSpec(
        num_scalar_prefetch=1,
        grid=(n_blocks,),
        in_specs=[pl.BlockSpec((MOE_BLOCK, d), lambda i, be: (i, 0)),
                  pl.BlockSpec((1, d, f), lambda i, be: (be[i], 0, 0)),
                  pl.BlockSpec((1, d, f), lambda i, be: (be[i], 0, 0)),
                  pl.BlockSpec((1, f, d), lambda i, be: (be[i], 0, 0))],
        out_specs=pl.BlockSpec((MOE_BLOCK, d), lambda i, be: (i, 0)),
    )
    return pl.pallas_call(
        _expert_kernel,
        out_shape=jax.ShapeDtypeStruct((n_rows, d), F32),
        grid_spec=grid_spec,
        compiler_params=_cparams("parallel"),
        name="moe_experts",
    )(block_expert, xs, wg, wu, wd)


def _combine_kernel(n_tiles, x1_ref, gc_ref, g_ref, b_ref, dest_hbm, ys_hbm, o_ref, ob_ref,
                    idx_smem, ybuf, idx_sem, row_sem):
    tm = x1_ref.shape[0]
    i = pl.program_id(0)
    slot = lax.rem(i, 2)
    other = 1 - slot

    def idx_copy(tile, sl):
        return pltpu.make_async_copy(dest_hbm.at[tile], idx_smem.at[sl], idx_sem.at[sl])

    def gather(sl, start):
        def body(j, carry):
            for k in range(TOP_K):
                cp = _row_copy(ys_hbm, idx_smem[sl, k, j], ybuf.at[sl, k], j, row_sem.at[sl])
                if start:
                    cp.start(priority=k)
                else:
                    cp.wait()
            return carry

        lax.fori_loop(0, tm, body, 0, unroll=8)

    @pl.when(i == 0)
    def _():
        first = idx_copy(0, 0)
        first.start()
        first.wait()
        gather(0, True)
        if n_tiles > 1:
            idx_copy(1, 1).start()

    @pl.when(i + 1 < n_tiles)
    def _():
        idx_copy(i + 1, other).wait()
        gather(other, True)

    gather(slot, False)

    @pl.when(i + 2 < n_tiles)
    def _():
        idx_copy(i + 2, slot).start()

    gc = gc_ref[...]
    f = ybuf[slot, 0] * gc[:, 0:1] + ybuf[slot, 1] * gc[:, 1:2]
    x2 = _layer_norm_rows(DEEPNORM_ALPHA * x1_ref[...] + f, g_ref[...], b_ref[...])
    o_ref[...] = x2
    ob_ref[...] = x2.astype(ob_ref.dtype)


def _combine(x1, gate_cols, ln_g, ln_b, dest_tiles, ys, tm=ROW_TILE):
    t, d = x1.shape
    full = lambda shape: pl.BlockSpec(shape, lambda i: (0,) * len(shape))
    return pl.pallas_call(
        functools.partial(_combine_kernel, t // tm),
        out_shape=(jax.ShapeDtypeStruct((t, d), F32), jax.ShapeDtypeStruct((t, d), BF16)),
        grid=(t // tm,),
        in_specs=[pl.BlockSpec((tm, d), lambda i: (i, 0)),
                  pl.BlockSpec((tm, 8), lambda i: (i, 0)),
                  full((1, d)), full((1, d)),
                  pl.BlockSpec(memory_space=pl.ANY),
                  pl.BlockSpec(memory_space=pl.ANY)],
        out_specs=(pl.BlockSpec((tm, d), lambda i: (i, 0)), pl.BlockSpec((tm, d), lambda i: (i, 0))),
        scratch_shapes=[pltpu.SMEM((2, TOP_K, tm), I32), pltpu.VMEM((2, TOP_K, tm, d), F32),
                        pltpu.SemaphoreType.DMA((2,)), pltpu.SemaphoreType.DMA((2,))],
        compiler_params=_cparams("arbitrary"),
        name="moe_combine_ln",
    )(x1, gate_cols, ln_g.reshape(1, d), ln_b.reshape(1, d), dest_tiles, ys)


def _moe_layer(x1, route_i, gate_cols, counts, wg, wu, wd, ln_g, ln_b, tm=ROW_TILE):
    t, d = x1.shape
    n_tiles = t // tm
    n_rows = t * TOP_K + N_EXPERTS * MOE_BLOCK
    counts = counts.astype(I32)
    padded = (counts + MOE_BLOCK - 1) // MOE_BLOCK * MOE_BLOCK
    pends = jnp.cumsum(padded)
    pstarts = pends - padded
    eids = route_i[0:2]
    start_of = jnp.sum(jnp.where(eids[None] == jnp.arange(N_EXPERTS, dtype=I32)[:, None, None],
                                 pstarts[:, None, None], 0), axis=0)
    dest = start_of + route_i[2:4]
    dest_tiles = dest.reshape(TOP_K, n_tiles, tm).transpose(1, 0, 2)
    n_blocks = n_rows // MOE_BLOCK
    block_expert = jnp.minimum(
        jnp.sum((jnp.arange(n_blocks, dtype=I32) * MOE_BLOCK)[:, None] >= pends[None, :], axis=-1),
        N_EXPERTS - 1).astype(I32)
    pad_lo = jnp.concatenate([pstarts + counts, pends[-1:]]).astype(I32)
    pad_hi = jnp.concatenate([pends, jnp.full((1,), n_rows, I32)]).astype(I32)
    xs = _dispatch(x1, dest_tiles, pad_lo, pad_hi, n_rows, tm)
    ys = _experts(block_expert, xs, wg, wu, wd)
    return _combine(x1, gate_cols, ln_g, ln_b, dest_tiles, ys, tm)


POOL_PAD = 16


def _pool_kernel(u_ref, wg_ref, scale_ref, o_ref, pad_ref):
    s, c = u_ref.shape[1], u_ref.shape[2]
    u = u_ref[0]
    zeros = jnp.zeros((POOL_PAD, c), F32)
    pad_ref[pl.ds(0, POOL_PAD), :] = zeros
    pad_ref[pl.ds(POOL_PAD + s, POOL_PAD), :] = zeros
    pad_ref[pl.ds(POOL_PAD, s), :] = u
    pos = lax.broadcasted_iota(I32, (s, c), 0)
    group = pl.program_id(1)
    for gi, win in enumerate(POOL_WINDOWS):
        @pl.when(group == gi)
        def _(win=win):
            half = win // 2
            total = pad_ref[pl.ds(POOL_PAD - half, s), :]
            for j in range(1 - half, half):
                total = total + pad_ref[pl.ds(POOL_PAD + j, s), :]
            count = (jnp.minimum(pos + half, s) - jnp.maximum(pos - half, 0)).astype(F32)
            mixed = total / count - u
            y = _dot(mixed.astype(BF16), wg_ref[0]) * scale_ref[...]
            o_ref[0] = y.astype(o_ref.dtype)


def _pool_core(u, w_group_bf16, scale):
    b, s, d = u.shape
    c = POOL_GROUP_DIM
    return pl.pallas_call(
        _pool_kernel,
        out_shape=jax.ShapeDtypeStruct((b, s, d), BF16),
        grid=(b, d // c),
        in_specs=[pl.BlockSpec((1, s, c), lambda i, g: (i, 0, g)),
                  pl.BlockSpec((1, c, c), lambda i, g: (g, 0, 0)),
                  pl.BlockSpec((1, c), lambda i, g: (0, g))],
        out_specs=pl.BlockSpec((1, s, c), lambda i, g: (i, 0, g)),
        scratch_shapes=[pltpu.VMEM((s + 2 * POOL_PAD, c), F32)],
        compiler_params=_cparams("parallel", "parallel"),
        name="pool_core",
    )(u, w_group_bf16, scale.reshape(1, d))


def _t5_bucket(rel):
    half = REL_BUCKETS // 2
    max_exact = half // 2
    n = jnp.abs(rel)
    large = max_exact + (jnp.log(jnp.maximum(n, 1).astype(F32) / max_exact)
                         / math.log(REL_MAX_DIST / max_exact) * (half - max_exact)).astype(I32)
    large = jnp.minimum(large, half - 1)
    return (rel > 0).astype(I32) * half + jnp.where(n < max_exact, n, large)


def _attn_bias_table(rel_bias):
    rel = jnp.arange(ATT_KSPAN)[None, :] - ATT_WINDOW - jnp.arange(ATT_QBLOCK)[:, None]
    bias = jnp.transpose(rel_bias[_t5_bucket(rel)], (2, 0, 1)).astype(F32)
    return jnp.where((jnp.abs(rel) <= ATT_WINDOW)[None], bias, NEG_BIG)


def _attn_kernel(q_ref, kp_ref, kc_ref, kn_ref, vp_ref, vc_ref, vn_ref, bias_ref, sink_ref, o_ref):
    j = pl.program_id(1)
    nb = pl.num_programs(1)
    qb = ATT_QBLOCK
    col = lax.broadcasted_iota(I32, (qb, ATT_KSPAN), 1)
    edge = jnp.where(((j == 0) & (col < qb)) | ((j == nb - 1) & (col >= 2 * qb)), NEG_BIG, 0.0)
    k_all = jnp.concatenate([kp_ref[...], kc_ref[...], kn_ref[...]], axis=0)
    v_all = jnp.concatenate([vp_ref[...], vc_ref[...], vn_ref[...]], axis=0)
    q = q_ref[...]
    heads = lambda g: range(g * ATT_GROUP, (g + 1) * ATT_GROUP)
    cols = lambda i: slice(i * ATT_HEAD_DIM, (i + 1) * ATT_HEAD_DIM)

    def scores(g):
        return [_dot_nt(q[:, cols(hd)], k_all[:, cols(g)]) for hd in heads(g)]

    def attend(g, raw):
        outs = []
        probs, dens = [], []
        for hd, qk in zip(heads(g), raw):
            logits = qk * (ATT_HEAD_DIM ** -0.5) + bias_ref[hd] + edge
            sink = sink_ref[hd]
            mx = jnp.maximum(jnp.max(logits, axis=-1, keepdims=True), sink)
            p = jnp.exp(logits - mx)
            dens.append(jnp.sum(p, axis=-1, keepdims=True) + jnp.exp(sink - mx))
            probs.append(p.astype(BF16))
        for p, den in zip(probs, dens):
            outs.append(_dot(p, v_all[:, cols(g)]) / den)
        return outs

    outs = []
    raw = scores(0)
    for g in range(ATT_KV_HEADS):
        nxt = scores(g + 1) if g + 1 < ATT_KV_HEADS else None
        outs += attend(g, raw)
        raw = nxt
    o_ref[...] = jnp.concatenate(outs, axis=-1).astype(o_ref.dtype)


def _attn_core(proj, bias_tbl, sink, b, s):
    qw = ATT_HEADS * ATT_HEAD_DIM
    kw = ATT_KV_HEADS * ATT_HEAD_DIM
    nb = s // ATT_QBLOCK
    kcol, vcol = qw // kw, qw // kw + 1
    row = lambda i, j: i * nb + j
    prev = lambda i, j: i * nb + jnp.maximum(j - 1, 0)
    nxt = lambda i, j: i * nb + jnp.minimum(j + 1, nb - 1)
    kv_spec = lambda rowf, c: pl.BlockSpec((ATT_QBLOCK, kw), lambda i, j: (rowf(i, j), c))
    return pl.pallas_call(
        _attn_kernel,
        out_shape=jax.ShapeDtypeStruct((b * s, qw), BF16),
        grid=(b, nb),
        in_specs=[pl.BlockSpec((ATT_QBLOCK, qw), lambda i, j: (row(i, j), 0)),
                  kv_spec(prev, kcol), kv_spec(row, kcol), kv_spec(nxt, kcol),
                  kv_spec(prev, vcol), kv_spec(row, vcol), kv_spec(nxt, vcol),
                  pl.BlockSpec((ATT_HEADS, ATT_QBLOCK, ATT_KSPAN), lambda i, j: (0, 0, 0)),
                  pl.BlockSpec(memory_space=pltpu.SMEM)],
        out_specs=pl.BlockSpec((ATT_QBLOCK, qw), lambda i, j: (row(i, j), 0)),
        compiler_params=_cparams("parallel", "parallel"),
        name="attn_core",
    )(proj, proj, proj, proj, proj, proj, proj, bias_tbl, sink.astype(F32))


GDN_GATE_LANES = V7X_LANES
GDN_CONV_PAD = 8
GDN_PREP_GROUP = 8
GDN_INPROJ_COLS = 512
GDN_INPROJ_SLAB = 512
GDN_HEADS_PER_STEP = 2
GDN_PREP_MATMULS = 10


def _softplus(x):
    return jnp.maximum(x, 0.0) + jnp.log(1.0 + jnp.exp(-jnp.abs(x)))


def _gdn_gates_kernel(x_ref, w_ref, alog_ref, dtb_ref, o_ref):
    s = x_ref.shape[0]
    c = GDN_CHUNK
    gl = _dot(x_ref[...].astype(BF16), w_ref[...])
    decay = -jnp.exp(alog_ref[...]) * _softplus(gl + dtb_ref[...])
    beta = jax.nn.sigmoid(gl)
    r = lax.broadcasted_iota(I32, (c, c), 0)
    q = lax.broadcasted_iota(I32, (c, c), 1)
    lower = jnp.where(r >= q, 1.0, 0.0)
    upper = jnp.where(r <= q, 1.0, 0.0)
    lane = lax.broadcasted_iota(I32, (c, GDN_GATE_LANES), 1)
    for n in range(s // c):
        d_c = decay[n * c:(n + 1) * c]
        pre = _dot(lower, d_c, precision=HIGHEST)
        suf = _dot(upper, d_c, precision=HIGHEST)
        o_ref[pl.ds(n * c, c), :] = jnp.where(lane < GDN_HEADS, pre,
                                              jnp.where(lane < 2 * GDN_HEADS, suf, beta[n * c:(n + 1) * c]))


def _gdn_gates(xf, w_gate_bf16, alog_vec, dtb_vec, b, s):
    d = xf.shape[1]
    full = lambda shape: pl.BlockSpec(shape, lambda i: (0,) * len(shape))
    return pl.pallas_call(
        _gdn_gates_kernel,
        out_shape=jax.ShapeDtypeStruct((b * s, GDN_GATE_LANES), F32),
        grid=(b,),
        in_specs=[pl.BlockSpec((s, d), lambda i: (i, 0)), full((d, GDN_GATE_LANES)),
                  full((1, GDN_GATE_LANES)), full((1, GDN_GATE_LANES))],
        out_specs=pl.BlockSpec((s, GDN_GATE_LANES), lambda i: (i, 0)),
        compiler_params=_cparams("parallel"),
        name="gdn_gates",
    )(xf, w_gate_bf16, alog_vec, dtb_vec)


def _block_diag2(r, is_b):
    return jnp.concatenate([jnp.where(is_b, 0.0, r), jnp.where(is_b, r, 0.0)], axis=0)


def _gdn_inproj_kernel(x_ref, w_ref, cw_ref, o_ref, pad_ref):
    s = x_ref.shape[0]
    n = w_ref.shape[1]
    dk = GDN_HEAD_DIM
    kind = pl.program_id(1) // (GDN_WIDTH // n)
    slab = min(s, GDN_INPROJ_SLAB)
    n_slabs = s // slab
    project = lambda k: _dot(x_ref[pl.ds(k * slab, slab), :].astype(BF16), w_ref[...])

    @pl.when(kind == 3)
    def _():
        for k in range(n_slabs):
            o_ref[pl.ds(k * slab, slab), :] = project(k).astype(o_ref.dtype)

    @pl.when(kind < 3)
    def _():
        norm_on = kind < 2
        q_scale = jnp.where(kind == 0, GDN_HEAD_DIM ** -0.5, 1.0)
        zeros = jnp.zeros((GDN_CONV_PAD, n), F32)
        pad_ref[pl.ds(0, GDN_CONV_PAD), :] = zeros
        pad_ref[pl.ds(GDN_CONV_PAD + s, GDN_CONV_PAD), :] = zeros
        cw = cw_ref[...]

        def finish(k):
            r0 = k * slab
            acc = None
            for j in range(GDN_CONV):
                term = pad_ref[pl.ds(GDN_CONV_PAD + r0 + j - GDN_CONV // 2, slab), :] * cw[j:j + 1, :]
                acc = term if acc is None else acc + term
            act = _silu(acc)
            outs = []
            for hd in range(n // dk):
                a = act[:, hd * dk:(hd + 1) * dk]
                inv = lax.rsqrt(jnp.sum(a * a, axis=-1, keepdims=True) + 1e-6) * q_scale
                outs.append(a * jnp.where(norm_on, inv, 1.0))
            o_ref[pl.ds(r0, slab), :] = jnp.concatenate(outs, axis=1).astype(o_ref.dtype)

        for k in range(n_slabs):
            pad_ref[pl.ds(GDN_CONV_PAD + k * slab, slab), :] = project(k)
            if k >= 1:
                finish(k - 1)
        finish(n_slabs - 1)


def _gdn_inproj(xf, w_main_bf16, conv_w, b, s):
    d = xf.shape[1]
    n = GDN_INPROJ_COLS
    n_conv_blocks = 3 * GDN_WIDTH // n
    return pl.pallas_call(
        _gdn_inproj_kernel,
        out_shape=jax.ShapeDtypeStruct((b * s, 4 * GDN_WIDTH), BF16),
        grid=(b, 4 * GDN_WIDTH // n),
        in_specs=[pl.BlockSpec((s, d), lambda i, j: (i, 0)),
                  pl.BlockSpec((d, n), lambda i, j: (0, j)),
                  pl.BlockSpec((GDN_CONV, n), lambda i, j: (0, jnp.minimum(j, n_conv_blocks - 1)))],
        out_specs=pl.BlockSpec((s, n), lambda i, j: (i, j)),
        scratch_shapes=[pltpu.VMEM((s + 2 * GDN_CONV_PAD, n), F32)],
        compiler_params=_cparams("parallel", "parallel"),
        name="gdn_inproj",
    )(xf, w_main_bf16, conv_w)


def _gdn_core_kernel(q_ref, k_ref, v_ref, z_ref, gt_ref, nw_ref, o_ref,
                     u_ref, w_ref, qd_ref, kd_ref, aqk_ref, dend_ref, of_ref, ob_ref):
    s = q_ref.shape[0]
    dk = GDN_HEAD_DIM
    hps = q_ref.shape[1] // dk
    c = GDN_CHUNK
    n_chunks = s // c
    first_head = pl.program_id(1) * hps
    slab = min(s, 256)
    head_cols = [slice(hd * dk, (hd + 1) * dk) for hd in range(hps)]

    lane = lax.broadcasted_iota(I32, (c, 2 * c), 1)
    row = lax.broadcasted_iota(I32, (c, 2 * c), 0)
    is_b = lane >= c
    col = jnp.where(is_b, lane - c, lane)
    ahead = jnp.where(is_b, col - row, row - col)
    incl = ahead >= 0
    strict = ahead > 0
    eye2 = jnp.where(row == col, 1.0, 0.0)
    shifts = [lax.rem(2 * V7X_LANES - first_head - hd, V7X_LANES) for hd in range(hps)]

    group = min(GDN_PREP_GROUP, n_chunks)
    n_groups = n_chunks // group

    def pair_rows(t):
        if isinstance(t, int):
            return pl.ds(t * c, c), pl.ds((n_chunks - 1 - t) * c, c)
        return pl.ds(pl.multiple_of(t * c, c), c), pl.ds(pl.multiple_of((n_chunks - 1 - t) * c, c), c)

    def prep_stages(hd, pg, slot):
        rows = [pair_rows(pg * group + i) for i in range(group)]
        cols = head_cols[hd]
        q_f, k_f, v_f = ([ref[r[0], cols].astype(F32) for r in rows] for ref in (q_ref, k_ref, v_ref))
        q_b, k_b, v_b = ([ref[r[1], cols].astype(F32) for r in rows] for ref in (q_ref, k_ref, v_ref))
        gt_f = [pltpu.roll(gt_ref[r[0], :], shifts[hd], axis=1) for r in rows]
        gt_b = [pltpu.roll(gt_ref[r[1], :], shifts[hd], axis=1) for r in rows]
        g_f = [t[:, 0:1] for t in gt_f]
        b_f = [t[:, 2 * GDN_HEADS:2 * GDN_HEADS + 1] for t in gt_f]
        g_b = [t[:, GDN_HEADS:GDN_HEADS + 1] for t in gt_b]
        b_b = [t[:, 3 * GDN_HEADS:3 * GDN_HEADS + 1] for t in gt_b]
        dmat, prod_f, prod_b = [], [], []
        for i in range(group):
            a_mat = jnp.where(lane == 0, g_f[i], jnp.where(lane == 2, g_b[i],
                                                           jnp.where((lane == 1) | (lane == 3), 1.0, 0.0)))
            b_top = jnp.where(lane == 0, 1.0, jnp.where(lane == 1, -g_f[i], 0.0))
            b_bot = jnp.where(lane == 2, 1.0, jnp.where(lane == 3, -g_b[i], 0.0))
            dmat.append(_dot_nt(a_mat, jnp.concatenate([b_top, b_bot], axis=0), precision=HIGHEST))
            yield
        for i in range(group):
            kfb, kbb = k_f[i].astype(BF16), k_b[i].astype(BF16)
            keys = jnp.concatenate([kfb, kbb], axis=0)
            prod_f.append(_dot_nt(jnp.concatenate([q_f[i].astype(BF16), kfb], axis=0), keys))
            yield
            prod_b.append(_dot_nt(jnp.concatenate([q_b[i].astype(BF16), kbb], axis=0), keys))
            yield
        qk2 = [jnp.where(is_b, prod_b[i][:c], prod_f[i][:c]) for i in range(group)]
        kk2 = [jnp.where(is_b, prod_b[i][c:], prod_f[i][c:]) for i in range(group)]
        gamma = [jnp.exp(jnp.where(incl, d_i, NEG_BIG)) for d_i in dmat]
        m = [jnp.where(strict, kk2[i] * gamma[i], 0.0) * jnp.where(is_b, b_b[i], b_f[i]) for i in range(group)]
        x_inv = [eye2 - m_i for m_i in m]
        p = []
        for m_i in m:
            p.append(_dot(m_i.astype(BF16), _block_diag2(m_i, is_b).astype(BF16)))
            yield
        for _ in range(5):
            y = []
            for i in range(group):
                y.append(_dot(jnp.concatenate([x_inv[i], p[i]], axis=0).astype(BF16),
                              _block_diag2(p[i], is_b).astype(BF16)))
                yield
            x_inv = [x_inv[i] + y[i][:c] for i in range(group)]
            p = [y_i[c:] for y_i in y]
        eg_f = [jnp.exp(g) for g in g_f]
        eg_b = [jnp.exp(g) for g in g_b]
        sol = []
        for i in range(group):
            rhs = jnp.concatenate([jnp.concatenate([v_f[i] * b_f[i], k_f[i] * (b_f[i] * eg_f[i])], axis=1),
                                   jnp.concatenate([v_b[i] * b_b[i], k_b[i] * (b_b[i] * eg_b[i])], axis=1)], axis=0)
            sol.append(_dot(_block_diag2(x_inv[i], is_b).astype(BF16), rhs.astype(BF16)))
            yield
        for i in range(group):
            local = pl.ds(i * c, c)
            gl_f, gl_b = g_f[i][c - 1:c, :], g_b[i][0:1, :]
            for d, (qc, kc, eg, gl, g_d) in enumerate(((q_f[i], k_f[i], eg_f[i], gl_f, g_f[i]),
                                                       (q_b[i], k_b[i], eg_b[i], gl_b, g_b[i]))):
                u_ref[hd, slot, d, local, :] = sol[i][d * c:(d + 1) * c, :dk]
                w_ref[hd, slot, d, local, :] = sol[i][d * c:(d + 1) * c, dk:]
                qd_ref[hd, slot, d, local, :] = qc * eg
                kd_ref[hd, slot, d, local, :] = kc * jnp.exp(gl - g_d)
            aqk_ref[hd, slot, local, :] = jnp.where(incl, qk2[i] * gamma[i], 0.0)
            dend_ref[hd, slot, pl.ds(i * V7X_SUBLANES, V7X_SUBLANES), :] = jnp.where(
                is_b[:V7X_SUBLANES], jnp.exp(gl_b), jnp.exp(gl_f))

    lanes = (slice(0, c), slice(c, 2 * c))
    chains = [(hd, d) for hd in range(hps) for d in range(2)]

    def run_group(pg, slot, prep_gens, states):
        per_gap = -(-GDN_PREP_MATMULS // 2)

        def advance():
            for _ in range(per_gap):
                for gen in prep_gens:
                    next(gen, None)

        for i in range(group):
            rows = pair_rows(pg * group + i)
            local = pl.ds(i * c, c)
            stb = [st.astype(BF16) for st in states]
            ws = [_dot(w_ref[hd, slot, d, local, :].astype(BF16), stb[n]) for n, (hd, d) in enumerate(chains)]
            qs = [_dot(qd_ref[hd, slot, d, local, :].astype(BF16), stb[n]) for n, (hd, d) in enumerate(chains)]
            advance()
            vnb = [(u_ref[hd, slot, d, local, :] - ws[n]).astype(BF16) for n, (hd, d) in enumerate(chains)]
            av = [_dot(aqk_ref[hd, slot, local, :][:, lanes[d]].astype(BF16), vnb[n])
                  for n, (hd, d) in enumerate(chains)]
            kv = [_dot_tn(kd_ref[hd, slot, d, local, :].astype(BF16), vnb[n]) for n, (hd, d) in enumerate(chains)]
            advance()
            new_states = []
            for n, (hd, d) in enumerate(chains):
                (of_ref, ob_ref)[d][hd, rows[d], :] = qs[n] + av[n]
                dend = dend_ref[hd, slot, pl.ds(i * V7X_SUBLANES, 1), :][:, d * c:d * c + 1]
                new_states.append(states[n] * dend + kv[n])
            states = tuple(new_states)
        for gen in prep_gens:
            for _ in gen:
                pass
        return states

    for hd in range(hps):
        for _ in prep_stages(hd, 0, 0):
            pass

    def fused(tg, states):
        slot = lax.rem(tg, 2)
        return run_group(tg, slot, [prep_stages(hd, tg + 1, 1 - slot) for hd in range(hps)], states)

    zero_state = jnp.zeros((dk, dk), F32)
    states = lax.fori_loop(0, n_groups - 1, fused, (zero_state,) * len(chains))
    run_group(n_groups - 1, (n_groups - 1) % 2, [], states)

    for hd in range(hps):
        cols = slice(hd * dk, (hd + 1) * dk)
        for r0 in range(0, s, slab):
            rs = pl.ds(r0, slab)
            o = of_ref[hd, rs, :] + ob_ref[hd, rs, :]
            o = o * lax.rsqrt(jnp.mean(o * o, axis=-1, keepdims=True) + 1e-6) * nw_ref[...]
            o_ref[rs, cols] = (o * _silu(z_ref[rs, cols].astype(F32))).astype(o_ref.dtype)


def _gdn_core(proj, gates, norm_w, b, s):
    dk = GDN_HEAD_DIM
    hps = GDN_HEADS_PER_STEP
    nhb = GDN_HEADS // hps
    blk = lambda kind: pl.BlockSpec((s, hps * dk), lambda i, h: (i, kind * nhb + h))
    group_rows = min(GDN_PREP_GROUP, s // GDN_CHUNK) * GDN_CHUNK
    seq = lambda: pltpu.VMEM((hps, s, dk), F32)
    slots = lambda: pltpu.VMEM((hps, 2, 2, group_rows, dk), F32)
    return pl.pallas_call(
        _gdn_core_kernel,
        out_shape=jax.ShapeDtypeStruct((b * s, GDN_WIDTH), BF16),
        grid=(b, nhb),
        in_specs=[blk(0), blk(1), blk(2), blk(3),
                  pl.BlockSpec((s, GDN_GATE_LANES), lambda i, h: (i, 0)),
                  pl.BlockSpec((1, dk), lambda i, h: (0, 0))],
        out_specs=pl.BlockSpec((s, hps * dk), lambda i, h: (i, h)),
        scratch_shapes=[slots(), slots(), slots(), slots(),
                        pltpu.VMEM((hps, 2, group_rows, 2 * GDN_CHUNK), F32),
                        pltpu.VMEM((hps, 2, group_rows // GDN_CHUNK * V7X_SUBLANES, 2 * GDN_CHUNK), F32),
                        seq(), seq()],
        compiler_params=_cparams("parallel", "parallel"),
        name="gdn_core",
    )(proj, proj, proj, proj, gates, norm_w.reshape(1, dk))


def _gdn_mixer_core(xf, w_in, conv_w, a_log, dt_bias, norm_w, b, s):
    n_main = 4 * GDN_WIDTH
    n_gate = 4 * GDN_HEADS
    proj = _gdn_inproj(xf, w_in[:, :n_main].astype(BF16), conv_w.astype(F32), b, s)
    w_gate = jnp.pad(w_in[:, n_main:], ((0, 0), (0, GDN_GATE_LANES - n_gate))).astype(BF16)
    lane_pad = lambda v: jnp.pad(v.astype(F32).reshape(1, 2 * GDN_HEADS), ((0, 0), (0, GDN_GATE_LANES - 2 * GDN_HEADS)))
    gates = _gdn_gates(xf, w_gate, lane_pad(a_log), lane_pad(dt_bias), b, s)
    return _gdn_core(proj, gates, norm_w.astype(F32), b, s)


def kernel(x, a_w_in, a_conv, a_A_log, a_dt_bias, a_norm_w, a_w_out, b_w_in, b_sink, b_w_out, rel_bias, c_w_in, c_w_group, c_scale, c_w_out, router_w, router_b, moe_w_gate, moe_w_up, moe_w_down, ln_g, ln_b):
    b, s, d = x.shape
    t = b * s
    xf = x.reshape(t, d)
    xb = xf.astype(BF16)
    router_wt = router_w.T.astype(BF16)
    router_b_col = router_b.reshape(N_EXPERTS, 1).astype(F32)
    bias_tbl = _attn_bias_table(rel_bias)
    for i in range(DEPTH):
        kind, j = i % N_MIXERS, i // N_MIXERS
        if kind == 0:
            h = _gdn_mixer_core(xb, a_w_in[j], a_conv[j], a_A_log[j], a_dt_bias[j], a_norm_w[j], b, s)
            w_out = a_w_out[j]
        elif kind == 1:
            proj = _linear(xb, b_w_in[j].astype(BF16), BF16)
            h = _attn_core(proj, bias_tbl, b_sink[j], b, s)
            w_out = b_w_out[j]
        else:
            u = _linear(xb, c_w_in[j].astype(BF16), F32)
            h = _pool_core(u.reshape(b, s, d), c_w_group[j].astype(BF16), c_scale[j]).reshape(t, d)
            w_out = c_w_out[j]
        x1, route_i, gate_cols, counts = _post_mixer(h, w_out.astype(BF16), xf, ln_g[i, 0], ln_b[i, 0],
                                                     router_wt, router_b_col)
        xf, xb = _moe_layer(x1, route_i, gate_cols, counts[:, 0], moe_w_gate[i], moe_w_up[i], moe_w_down[i],
                            ln_g[i, 1], ln_b[i, 1])
    return xf.reshape(b, s, d)
```

```python
import functools
import math

import jax
import jax.numpy as jnp
from jax import lax
from jax.experimental import pallas as pl
from jax.experimental.pallas import tpu as pltpu

F32 = jnp.float32
BF16 = jnp.bfloat16
I32 = jnp.int32

D_MODEL = 1024
DEPTH = 4
N_MIXERS = 3
DEEPNORM_ALPHA = (2.0 * DEPTH) ** 0.25
LN_EPS = 1e-5

GDN_HEADS = 8
GDN_HEAD_DIM = 128
GDN_WIDTH = GDN_HEADS * GDN_HEAD_DIM
GDN_CONV = 5
GDN_CHUNK = 64

ATT_HEADS = 16
ATT_KV_HEADS = 4
ATT_HEAD_DIM = 64
ATT_GROUP = ATT_HEADS // ATT_KV_HEADS
ATT_WINDOW = 128
ATT_QBLOCK = 128
ATT_KSPAN = ATT_QBLOCK + 2 * ATT_WINDOW
REL_BUCKETS = 32
REL_MAX_DIST = 128

POOL_WINDOWS = (2, 4, 8, 16)
POOL_GROUP_DIM = D_MODEL // len(POOL_WINDOWS)

N_EXPERTS = 16
N_EXPERT_GROUPS = 4
EXPERTS_PER_GROUP = N_EXPERTS // N_EXPERT_GROUPS
TOP_K = 2
EXPERT_DIM = 512

V7X_LANES = 128
V7X_SUBLANES = 8
V7X_VMEM_LIMIT_BYTES = 56 * 1024 * 1024

ROW_TILE = 512
MOE_BLOCK = 512
NEG_BIG = -1e30

HIGHEST = lax.Precision.HIGHEST


def _cparams(*sem):
    return pltpu.CompilerParams(dimension_semantics=tuple(sem), vmem_limit_bytes=V7X_VMEM_LIMIT_BYTES)


def _dot(a, b, **kw):
    return jnp.dot(a, b, preferred_element_type=F32, **kw)


def _dot_nt(a, b, **kw):
    return lax.dot_general(a, b, (((1,), (1,)), ((), ())), preferred_element_type=F32, **kw)


def _dot_tn(a, b, **kw):
    return lax.dot_general(a, b, (((0,), (0,)), ((), ())), preferred_element_type=F32, **kw)


def _silu(x):
    return x * jax.nn.sigmoid(x)


def _linear_kernel(x_ref, w_ref, o_ref):
    o_ref[...] = _dot(x_ref[...].astype(BF16), w_ref[...]).astype(o_ref.dtype)


def _linear(x, w_bf16, out_dtype, tm=ROW_TILE):
    m, k = x.shape
    n = w_bf16.shape[1]
    return pl.pallas_call(
        _linear_kernel,
        out_shape=jax.ShapeDtypeStruct((m, n), out_dtype),
        grid=(m // tm,),
        in_specs=[pl.BlockSpec((tm, k), lambda i: (i, 0)), pl.BlockSpec((k, n), lambda i: (0, 0))],
        out_specs=pl.BlockSpec((tm, n), lambda i: (i, 0)),
        compiler_params=_cparams("parallel"),
        name="linear",
    )(x, w_bf16)


def _layer_norm_rows(s, g, b):
    mu = jnp.mean(s, axis=-1, keepdims=True)
    xc = s - mu
    var = jnp.mean(xc * xc, axis=-1, keepdims=True)
    return xc * lax.rsqrt(var + LN_EPS) * g + b


def _top2_of4(a, b, c, d):
    hi1, lo1 = jnp.maximum(a, b), jnp.minimum(a, b)
    hi2, lo2 = jnp.maximum(c, d), jnp.minimum(c, d)
    return jnp.maximum(hi1, hi2) + jnp.maximum(jnp.minimum(hi1, hi2), jnp.maximum(lo1, lo2))


def _route_rows(scores, biased):
    bi = [biased[e:e + 1, :] for e in range(N_EXPERTS)]
    sc = [scores[e:e + 1, :] for e in range(N_EXPERTS)]
    gs = [_top2_of4(*bi[4 * g:4 * g + 4]) for g in range(N_EXPERT_GROUPS)]
    group = jnp.zeros_like(gs[0], dtype=I32)
    best = gs[0]
    for g in range(1, N_EXPERT_GROUPS):
        upd = gs[g] > best
        group = jnp.where(upd, g, group)
        best = jnp.where(upd, gs[g], best)

    def pick(rows, j):
        out = rows[j]
        for g in range(1, N_EXPERT_GROUPS):
            out = jnp.where(group == g, rows[4 * g + j], out)
        return out

    v = [pick(bi, j) for j in range(EXPERTS_PER_GROUP)]
    s = [pick(sc, j) for j in range(EXPERTS_PER_GROUP)]
    i1 = jnp.zeros_like(group)
    b1 = v[0]
    for j in range(1, EXPERTS_PER_GROUP):
        upd = v[j] > b1
        i1 = jnp.where(upd, j, i1)
        b1 = jnp.where(upd, v[j], b1)
    i2 = jnp.full_like(group, -1)
    b2 = jnp.full_like(b1, -jnp.inf)
    for j in range(EXPERTS_PER_GROUP):
        upd = (i1 != j) & ((v[j] > b2) | (i2 < 0))
        i2 = jnp.where(upd, j, i2)
        b2 = jnp.where(upd, v[j], b2)
    s1 = s[0]
    s2 = s[0]
    for j in range(1, EXPERTS_PER_GROUP):
        s1 = jnp.where(i1 == j, s[j], s1)
        s2 = jnp.where(i2 == j, s[j], s2)
    den = s1 + s2
    return group * EXPERTS_PER_GROUP + i1, group * EXPERTS_PER_GROUP + i2, s1 / den, s2 / den


def _post_mixer_kernel(h_ref, w_ref, x_ref, g_ref, b_ref, rwt_ref, rb_ref, ustrict_ref, eye_ref,
                       x1_ref, ri_ref, gc_ref, cnt_ref, base_ref):
    tm = x_ref.shape[0]

    @pl.when(pl.program_id(0) == 0)
    def _():
        base_ref[...] = jnp.zeros_like(base_ref)

    y = _dot(h_ref[...], w_ref[...])
    x1 = _layer_norm_rows(DEEPNORM_ALPHA * x_ref[...] + y, g_ref[...], b_ref[...])
    x1_ref[...] = x1

    logits = _dot_nt(rwt_ref[...], x1.astype(BF16))
    scores = jax.nn.sigmoid(logits)
    e1, e2, g1, g2 = _route_rows(scores, scores + rb_ref[...])

    eidx = lax.broadcasted_iota(I32, (N_EXPERTS, tm), 0)
    hit1 = eidx == e1
    hit2 = eidx == e2
    onehot = jnp.where(hit1 | hit2, 1.0, 0.0)
    base = base_ref[:, 0:1]
    before = _dot(onehot.astype(BF16), ustrict_ref[...]) + base
    r1 = jnp.sum(jnp.where(hit1, before, 0.0), axis=0, keepdims=True)
    r2 = jnp.sum(jnp.where(hit2, before, 0.0), axis=0, keepdims=True)
    new_base = base + jnp.sum(onehot, axis=1, keepdims=True)
    base_ref[...] = jnp.broadcast_to(new_base, base_ref.shape)
    cnt_ref[...] = jnp.broadcast_to(new_base, cnt_ref.shape)

    ri_ref[...] = jnp.concatenate(
        [e1, e2, r1.astype(I32), r2.astype(I32), jnp.zeros((4, tm), I32)], axis=0)
    gates = jnp.concatenate([g1, g2, jnp.zeros((6, tm), F32)], axis=0)
    hi = gates.astype(BF16)
    lo = (gates - hi.astype(F32)).astype(BF16)
    gc_ref[...] = _dot_nt(eye_ref[...], hi) + _dot_nt(eye_ref[...], lo)


def _post_mixer(h_bf16, w_out_bf16, x, ln_g, ln_b, router_wt, router_b_col, tm=ROW_TILE):
    t, d = x.shape
    ustrict = (jnp.arange(tm)[:, None] < jnp.arange(tm)[None, :]).astype(BF16)
    eye = jnp.eye(tm, dtype=BF16)
    full = lambda shape: pl.BlockSpec(shape, lambda i: (0,) * len(shape))
    return pl.pallas_call(
        _post_mixer_kernel,
        out_shape=(jax.ShapeDtypeStruct((t, d), F32),
                   jax.ShapeDtypeStruct((8, t), I32),
                   jax.ShapeDtypeStruct((t, 8), F32),
                   jax.ShapeDtypeStruct((N_EXPERTS, V7X_LANES), F32)),
        grid=(t // tm,),
        in_specs=[pl.BlockSpec((tm, h_bf16.shape[1]), lambda i: (i, 0)),
                  full(w_out_bf16.shape),
                  pl.BlockSpec((tm, d), lambda i: (i, 0)),
                  full((1, d)), full((1, d)),
                  full((N_EXPERTS, d)), full((N_EXPERTS, 1)),
                  full((tm, tm)), full((tm, tm))],
        out_specs=(pl.BlockSpec((tm, d), lambda i: (i, 0)),
                   pl.BlockSpec((8, tm), lambda i: (0, i)),
                   pl.BlockSpec((tm, 8), lambda i: (i, 0)),
                   pl.BlockSpec((N_EXPERTS, V7X_LANES), lambda i: (0, 0))),
        scratch_shapes=[pltpu.VMEM((N_EXPERTS, V7X_LANES), F32)],
        compiler_params=_cparams("arbitrary"),
        name="post_mixer_router",
    )(h_bf16, w_out_bf16, x, ln_g.reshape(1, d), ln_b.reshape(1, d), router_wt, router_b_col, ustrict, eye)


def _row_copy(src_ref, src_row, dst_ref, dst_row, sem):
    return pltpu.make_async_copy(src_ref.at[pl.ds(src_row, 1), :], dst_ref.at[pl.ds(dst_row, 1), :], sem)


def _dispatch_kernel(n_tiles, pad_lo_ref, pad_hi_ref, x_ref, dest_hbm, xs_hbm,
                     idx_smem, zero_ref, idx_sem, row_sem, pad_sem):
    tm = x_ref.shape[0]
    i = pl.program_id(0)
    slot = lax.rem(i, 2)

    def idx_copy(tile, sl):
        return pltpu.make_async_copy(dest_hbm.at[tile], idx_smem.at[sl], idx_sem.at[sl])

    @pl.when(i == 0)
    def _():
        idx_copy(0, 0).start()

    idx_copy(i, slot).wait()

    @pl.when(i + 1 < n_tiles)
    def _():
        idx_copy(i + 1, 1 - slot).start()

    def issue(j, carry):
        for k in range(TOP_K):
            _row_copy(x_ref, j, xs_hbm, idx_smem[slot, k, j], row_sem).start(priority=k)
        return carry

    lax.fori_loop(0, tm, issue, 0, unroll=8)

    def drain(j, carry):
        for k in range(TOP_K):
            _row_copy(x_ref, j, xs_hbm, idx_smem[slot, k, j], row_sem).wait()
        return carry

    lax.fori_loop(0, tm, drain, 0, unroll=8)

    @pl.when(i == pl.num_programs(0) - 1)
    def _():
        zero_ref[...] = jnp.zeros_like(zero_ref)
        for e in range(N_EXPERTS + 1):
            def fill(r, carry):
                _row_copy(zero_ref, 0, xs_hbm, r, pad_sem).start()
                return carry

            lax.fori_loop(pad_lo_ref[e], pad_hi_ref[e], fill, 0)
        for e in range(N_EXPERTS + 1):
            def filled(r, carry):
                _row_copy(zero_ref, 0, xs_hbm, r, pad_sem).wait()
                return carry

            lax.fori_loop(pad_lo_ref[e], pad_hi_ref[e], filled, 0)


def _dispatch(x1, dest_tiles, pad_lo, pad_hi, n_rows, tm=ROW_TILE):
    t, d = x1.shape
    grid_spec = pltpu.PrefetchScalarGridSpec(
        num_scalar_prefetch=2,
        grid=(t // tm,),
        in_specs=[pl.BlockSpec((tm, d), lambda i, lo, hi: (i, 0)),
                  pl.BlockSpec(memory_space=pl.ANY)],
        out_specs=pl.BlockSpec(memory_space=pl.ANY),
        scratch_shapes=[pltpu.SMEM((2, TOP_K, tm), I32), pltpu.VMEM((V7X_SUBLANES, d), F32),
                        pltpu.SemaphoreType.DMA((2,)), pltpu.SemaphoreType.DMA, pltpu.SemaphoreType.DMA],
    )
    return pl.pallas_call(
        functools.partial(_dispatch_kernel, t // tm),
        out_shape=jax.ShapeDtypeStruct((n_rows, d), F32),
        grid_spec=grid_spec,
        compiler_params=_cparams("arbitrary"),
        name="moe_dispatch",
    )(pad_lo, pad_hi, x1, dest_tiles)


def _expert_kernel(be_ref, xs_ref, wg_ref, wu_ref, wd_ref, ys_ref):
    del be_ref
    x = xs_ref[...].astype(BF16)
    hidden = _silu(_dot(x, wg_ref[0].astype(BF16))) * _dot(x, wu_ref[0].astype(BF16))
    ys_ref[...] = _dot(hidden.astype(BF16), wd_ref[0].astype(BF16))


def _experts(block_expert, xs, wg, wu, wd):
    n_rows, d = xs.shape
    n_blocks = n_rows // MOE_BLOCK
    f = wg.shape[2]
    grid_spec = pltpu.PrefetchScalarGridSpec(
        num_scalar_prefetch=1,
        grid=(n_blocks,),
        in_specs=[pl.BlockSpec((MOE_BLOCK, d), lambda i, be: (i, 0)),
                  pl.BlockSpec((1, d, f), lambda i, be: (be[i], 0, 0)),
                  pl.BlockSpec((1, d, f), lambda i, be: (be[i], 0, 0)),
                  pl.BlockSpec((1, f, d), lambda i, be: (be[i], 0, 0))],
        out_specs=pl.BlockSpec((MOE_BLOCK, d), lambda i, be: (i, 0)),
    )
    return pl.pallas_call(
        _expert_kernel,
        out_shape=jax.ShapeDtypeStruct((n_rows, d), F32),
        grid_spec=grid_spec,
        compiler_params=_cparams("parallel"),
        name="moe_experts",
    )(block_expert, xs, wg, wu, wd)


def _combine_kernel(n_tiles, x1_ref, gc_ref, g_ref, b_ref, dest_hbm, ys_hbm, o_ref, ob_ref,
                    idx_smem, ybuf, idx_sem, row_sem):
    tm = x1_ref.shape[0]
    i = pl.program_id(0)
    slot = lax.rem(i, 2)
    other = 1 - slot

    def idx_copy(tile, sl):
        return pltpu.make_async_copy(dest_hbm.at[tile], idx_smem.at[sl], idx_sem.at[sl])

    def gather(sl, start):
        def body(j, carry):
            for k in range(TOP_K):
                cp = _row_copy(ys_hbm, idx_smem[sl, k, j], ybuf.at[sl, k], j, row_sem.at[sl])
                if start:
                    cp.start(priority=k)
                else:
                    cp.wait()
            return carry

        lax.fori_loop(0, tm, body, 0, unroll=8)

    @pl.when(i == 0)
    def _():
        first = idx_copy(0, 0)
        first.start()
        first.wait()
        gather(0, True)
        if n_tiles > 1:
            idx_copy(1, 1).start()

    @pl.when(i + 1 < n_tiles)
    def _():
        idx_copy(i + 1, other).wait()
        gather(other, True)

    gather(slot, False)

    @pl.when(i + 2 < n_tiles)
    def _():
        idx_copy(i + 2, slot).start()

    gc = gc_ref[...]
    f = ybuf[slot, 0] * gc[:, 0:1] + ybuf[slot, 1] * gc[:, 1:2]
    x2 = _layer_norm_rows(DEEPNORM_ALPHA * x1_ref[...] + f, g_ref[...], b_ref[...])
    o_ref[...] = x2
    ob_ref[...] = x2.astype(ob_ref.dtype)


def _combine(x1, gate_cols, ln_g, ln_b, dest_tiles, ys, tm=ROW_TILE):
    t, d = x1.shape
    full = lambda shape: pl.BlockSpec(shape, lambda i: (0,) * len(shape))
    return pl.pallas_call(
        functools.partial(_combine_kernel, t // tm),
        out_shape=(jax.ShapeDtypeStruct((t, d), F32), jax.ShapeDtypeStruct((t, d), BF16)),
        grid=(t // tm,),
        in_specs=[pl.BlockSpec((tm, d), lambda i: (i, 0)),
                  pl.BlockSpec((tm, 8), lambda i: (i, 0)),
                  full((1, d)), full((1, d)),
                  pl.BlockSpec(memory_space=pl.ANY),
                  pl.BlockSpec(memory_space=pl.ANY)],
        out_specs=(pl.BlockSpec((tm, d), lambda i: (i, 0)), pl.BlockSpec((tm, d), lambda i: (i, 0))),
        scratch_shapes=[pltpu.SMEM((2, TOP_K, tm), I32), pltpu.VMEM((2, TOP_K, tm, d), F32),
                        pltpu.SemaphoreType.DMA((2,)), pltpu.SemaphoreType.DMA((2,))],
        compiler_params=_cparams("arbitrary"),
        name="moe_combine_ln",
    )(x1, gate_cols, ln_g.reshape(1, d), ln_b.reshape(1, d), dest_tiles, ys)


def _moe_layer(x1, route_i, gate_cols, counts, wg, wu, wd, ln_g, ln_b, tm=ROW_TILE):
    t, d = x1.shape
    n_tiles = t // tm
    n_rows = t * TOP_K + N_EXPERTS * MOE_BLOCK
    counts = counts.astype(I32)
    padded = (counts + MOE_BLOCK - 1) // MOE_BLOCK * MOE_BLOCK
    pends = jnp.cumsum(padded)
    pstarts = pends - padded
    eids = route_i[0:2]
    start_of = jnp.sum(jnp.where(eids[None] == jnp.arange(N_EXPERTS, dtype=I32)[:, None, None],
                                 pstarts[:, None, None], 0), axis=0)
    dest = start_of + route_i[2:4]
    dest_tiles = dest.reshape(TOP_K, n_tiles, tm).transpose(1, 0, 2)
    n_blocks = n_rows // MOE_BLOCK
    block_expert = jnp.minimum(
        jnp.sum((jnp.arange(n_blocks, dtype=I32) * MOE_BLOCK)[:, None] >= pends[None, :], axis=-1),
        N_EXPERTS - 1).astype(I32)
    pad_lo = jnp.concatenate([pstarts + counts, pends[-1:]]).astype(I32)
    pad_hi = jnp.concatenate([pends, jnp.full((1,), n_rows, I32)]).astype(I32)
    xs = _dispatch(x1, dest_tiles, pad_lo, pad_hi, n_rows, tm)
    ys = _experts(block_expert, xs, wg, wu, wd)
    return _combine(x1, gate_cols, ln_g, ln_b, dest_tiles, ys, tm)


POOL_PAD = 16


def _pool_kernel(u_ref, wg_ref, scale_ref, o_ref, pad_ref):
    s, c = u_ref.shape[1], u_ref.shape[2]
    u = u_ref[0]
    zeros = jnp.zeros((POOL_PAD, c), F32)
    pad_ref[pl.ds(0, POOL_PAD), :] = zeros
    pad_ref[pl.ds(POOL_PAD + s, POOL_PAD), :] = zeros
    pad_ref[pl.ds(POOL_PAD, s), :] = u
    pos = lax.broadcasted_iota(I32, (s, c), 0)
    group = pl.program_id(1)
    for gi, win in enumerate(POOL_WINDOWS):
        @pl.when(group == gi)
        def _(win=win):
            half = win // 2
            total = pad_ref[pl.ds(POOL_PAD - half, s), :]
            for j in range(1 - half, half):
                total = total + pad_ref[pl.ds(POOL_PAD + j, s), :]
            count = (jnp.minimum(pos + half, s) - jnp.maximum(pos - half, 0)).astype(F32)
            mixed = total / count - u
            y = _dot(mixed.astype(BF16), wg_ref[0]) * scale_ref[...]
            o_ref[0] = y.astype(o_ref.dtype)


def _pool_core(u, w_group_bf16, scale):
    b, s, d = u.shape
    c = POOL_GROUP_DIM
    return pl.pallas_call(
        _pool_kernel,
        out_shape=jax.ShapeDtypeStruct((b, s, d), BF16),
        grid=(b, d // c),
        in_specs=[pl.BlockSpec((1, s, c), lambda i, g: (i, 0, g)),
                  pl.BlockSpec((1, c, c), lambda i, g: (g, 0, 0)),
                  pl.BlockSpec((1, c), lambda i, g: (0, g))],
        out_specs=pl.BlockSpec((1, s, c), lambda i, g: (i, 0, g)),
        scratch_shapes=[pltpu.VMEM((s + 2 * POOL_PAD, c), F32)],
        compiler_params=_cparams("parallel", "parallel"),
        name="pool_core",
    )(u, w_group_bf16, scale.reshape(1, d))


def _t5_bucket(rel):
    half = REL_BUCKETS // 2
    max_exact = half // 2
    n = jnp.abs(rel)
    large = max_exact + (jnp.log(jnp.maximum(n, 1).astype(F32) / max_exact)
                         / math.log(REL_MAX_DIST / max_exact) * (half - max_exact)).astype(I32)
    large = jnp.minimum(large, half - 1)
    return (rel > 0).astype(I32) * half + jnp.where(n < max_exact, n, large)


def _attn_bias_table(rel_bias):
    rel = jnp.arange(ATT_KSPAN)[None, :] - ATT_WINDOW - jnp.arange(ATT_QBLOCK)[:, None]
    bias = jnp.transpose(rel_bias[_t5_bucket(rel)], (2, 0, 1)).astype(F32)
    return jnp.where((jnp.abs(rel) <= ATT_WINDOW)[None], bias, NEG_BIG)


def _attn_kernel(q_ref, kp_ref, kc_ref, kn_ref, vp_ref, vc_ref, vn_ref, bias_ref, sink_ref, o_ref):
    j = pl.program_id(1)
    nb = pl.num_programs(1)
    qb = ATT_QBLOCK
    col = lax.broadcasted_iota(I32, (qb, ATT_KSPAN), 1)
    edge = jnp.where(((j == 0) & (col < qb)) | ((j == nb - 1) & (col >= 2 * qb)), NEG_BIG, 0.0)
    k_all = jnp.concatenate([kp_ref[...], kc_ref[...], kn_ref[...]], axis=0)
    v_all = jnp.concatenate([vp_ref[...], vc_ref[...], vn_ref[...]], axis=0)
    q = q_ref[...]
    heads = lambda g: range(g * ATT_GROUP, (g + 1) * ATT_GROUP)
    cols = lambda i: slice(i * ATT_HEAD_DIM, (i + 1) * ATT_HEAD_DIM)

    def scores(g):
        return [_dot_nt(q[:, cols(hd)], k_all[:, cols(g)]) for hd in heads(g)]

    def attend(g, raw):
        outs = []
        probs, dens = [], []
        for hd, qk in zip(heads(g), raw):
            logits = qk * (ATT_HEAD_DIM ** -0.5) + bias_ref[hd] + edge
            sink = sink_ref[hd]
            mx = jnp.maximum(jnp.max(logits, axis=-1, keepdims=True), sink)
            p = jnp.exp(logits - mx)
            dens.append(jnp.sum(p, axis=-1, keepdims=True) + jnp.exp(sink - mx))
            probs.append(p.astype(BF16))
        for p, den in zip(probs, dens):
            outs.append(_dot(p, v_all[:, cols(g)]) / den)
        return outs

    outs = []
    raw = scores(0)
    for g in range(ATT_KV_HEADS):
        nxt = scores(g + 1) if g + 1 < ATT_KV_HEADS else None
        outs += attend(g, raw)
        raw = nxt
    o_ref[...] = jnp.concatenate(outs, axis=-1).astype(o_ref.dtype)


def _attn_core(proj, bias_tbl, sink, b, s):
    qw = ATT_HEADS * ATT_HEAD_DIM
    kw = ATT_KV_HEADS * ATT_HEAD_DIM
    nb = s // ATT_QBLOCK
    kcol, vcol = qw // kw, qw // kw + 1
    row = lambda i, j: i * nb + j
    prev = lambda i, j: i * nb + jnp.maximum(j - 1, 0)
    nxt = lambda i, j: i * nb + jnp.minimum(j + 1, nb - 1)
    kv_spec = lambda rowf, c: pl.BlockSpec((ATT_QBLOCK, kw), lambda i, j: (rowf(i, j), c))
    return pl.pallas_call(
        _attn_kernel,
        out_shape=jax.ShapeDtypeStruct((b * s, qw), BF16),
        grid=(b, nb),
        in_specs=[pl.BlockSpec((ATT_QBLOCK, qw), lambda i, j: (row(i, j), 0)),
                  kv_spec(prev, kcol), kv_spec(row, kcol), kv_spec(nxt, kcol),
                  kv_spec(prev, vcol), kv_spec(row, vcol), kv_spec(nxt, vcol),
                  pl.BlockSpec((ATT_HEADS, ATT_QBLOCK, ATT_KSPAN), lambda i, j: (0, 0, 0)),
                  pl.BlockSpec(memory_space=pltpu.SMEM)],
        out_specs=pl.BlockSpec((ATT_QBLOCK, qw), lambda i, j: (row(i, j), 0)),
        compiler_params=_cparams("parallel", "parallel"),
        name="attn_core",
    )(proj, proj, proj, proj, proj, proj, proj, bias_tbl, sink.astype(F32))


GDN_GATE_LANES = V7X_LANES
GDN_CONV_PAD = 8
GDN_PREP_GROUP = 8
GDN_INPROJ_COLS = 512
GDN_INPROJ_SLAB = 512
GDN_HEADS_PER_STEP = 2
GDN_PREP_MATMULS = 10


def _softplus(x):
    return jnp.maximum(x, 0.0) + jnp.log(1.0 + jnp.exp(-jnp.abs(x)))


def _gdn_gates_kernel(x_ref, w_ref, alog_ref, dtb_ref, o_ref, rows_ref):
    s = x_ref.shape[0]
    c = GDN_CHUNK
    gl = _dot(x_ref[...].astype(BF16), w_ref[...])
    decay = -jnp.exp(alog_ref[...]) * _softplus(gl + dtb_ref[...])
    beta = jax.nn.sigmoid(gl)
    r = lax.broadcasted_iota(I32, (c, c), 0)
    q = lax.broadcasted_iota(I32, (c, c), 1)
    lower = jnp.where(r >= q, 1.0, 0.0)
    upper = jnp.where(r <= q, 1.0, 0.0)
    lane = lax.broadcasted_iota(I32, (c, GDN_GATE_LANES), 1)
    eye = jnp.where(lax.broadcasted_iota(I32, (GDN_GATE_LANES, GDN_GATE_LANES), 0)
                    == lax.broadcasted_iota(I32, (GDN_GATE_LANES, GDN_GATE_LANES), 1), 1.0, 0.0)
    hh = GDN_HEADS
    for n in range(s // c):
        d_c = decay[n * c:(n + 1) * c]
        pre = _dot(lower, d_c, precision=HIGHEST)
        suf = _dot(upper, d_c, precision=HIGHEST)
        tile = jnp.where(lane < hh, pre, jnp.where(lane < 2 * hh, suf, beta[n * c:(n + 1) * c]))
        o_ref[pl.ds(n * c, c), :] = tile
        t_rows = _dot_nt(eye, tile, precision=HIGHEST)
        g_rows = jnp.concatenate([t_rows[0:hh], t_rows[hh:2 * hh]], axis=1)
        b_rows = jnp.concatenate([t_rows[2 * hh:3 * hh], t_rows[3 * hh:4 * hh]], axis=1)
        rows_ref[n] = jnp.concatenate([g_rows, b_rows, b_rows * jnp.exp(g_rows)], axis=1)


def _gdn_gates(xf, w_gate_bf16, alog_vec, dtb_vec, b, s):
    d = xf.shape[1]
    n_chunks = s // GDN_CHUNK
    full = lambda shape: pl.BlockSpec(shape, lambda i: (0,) * len(shape))
    return pl.pallas_call(
        _gdn_gates_kernel,
        out_shape=(jax.ShapeDtypeStruct((b * s, GDN_GATE_LANES), F32),
                   jax.ShapeDtypeStruct((b * n_chunks, GDN_HEADS, 3 * GDN_GATE_LANES), F32)),
        grid=(b,),
        in_specs=[pl.BlockSpec((s, d), lambda i: (i, 0)), full((d, GDN_GATE_LANES)),
                  full((1, GDN_GATE_LANES)), full((1, GDN_GATE_LANES))],
        out_specs=(pl.BlockSpec((s, GDN_GATE_LANES), lambda i: (i, 0)),
                   pl.BlockSpec((n_chunks, GDN_HEADS, 3 * GDN_GATE_LANES), lambda i: (i, 0, 0))),
        compiler_params=_cparams("parallel"),
        name="gdn_gates",
    )(xf, w_gate_bf16, alog_vec, dtb_vec)


def _block_diag2(r, is_b):
    return jnp.concatenate([jnp.where(is_b, 0.0, r), jnp.where(is_b, r, 0.0)], axis=0)


def _gdn_inproj_kernel(x_ref, w_ref, cw_ref, o_ref, pad_ref):
    s = x_ref.shape[0]
    n = w_ref.shape[1]
    dk = GDN_HEAD_DIM
    kind = pl.program_id(1) // (GDN_WIDTH // n)
    slab = min(s, GDN_INPROJ_SLAB)
    n_slabs = s // slab
    project = lambda k: _dot(x_ref[pl.ds(k * slab, slab), :].astype(BF16), w_ref[...])

    @pl.when(kind == 3)
    def _():
        for k in range(n_slabs):
            o_ref[pl.ds(k * slab, slab), :] = project(k).astype(o_ref.dtype)

    @pl.when(kind < 3)
    def _():
        norm_on = kind < 2
        q_scale = jnp.where(kind == 0, GDN_HEAD_DIM ** -0.5, 1.0)
        zeros = jnp.zeros((GDN_CONV_PAD, n), F32)
        pad_ref[pl.ds(0, GDN_CONV_PAD), :] = zeros
        pad_ref[pl.ds(GDN_CONV_PAD + s, GDN_CONV_PAD), :] = zeros
        cw = cw_ref[...]

        def finish(k):
            r0 = k * slab
            acc = None
            for j in range(GDN_CONV):
                term = pad_ref[pl.ds(GDN_CONV_PAD + r0 + j - GDN_CONV // 2, slab), :] * cw[j:j + 1, :]
                acc = term if acc is None else acc + term
            act = _silu(acc)
            outs = []
            for hd in range(n // dk):
                a = act[:, hd * dk:(hd + 1) * dk]
                inv = lax.rsqrt(jnp.sum(a * a, axis=-1, keepdims=True) + 1e-6) * q_scale
                outs.append(a * jnp.where(norm_on, inv, 1.0))
            o_ref[pl.ds(r0, slab), :] = jnp.concatenate(outs, axis=1).astype(o_ref.dtype)

        for k in range(n_slabs):
            pad_ref[pl.ds(GDN_CONV_PAD + k * slab, slab), :] = project(k)
            if k >= 1:
                finish(k - 1)
        finish(n_slabs - 1)


def _gdn_inproj(xf, w_main_bf16, conv_w, b, s):
    d = xf.shape[1]
    n = GDN_INPROJ_COLS
    n_conv_blocks = 3 * GDN_WIDTH // n
    return pl.pallas_call(
        _gdn_inproj_kernel,
        out_shape=jax.ShapeDtypeStruct((b * s, 4 * GDN_WIDTH), BF16),
        grid=(b, 4 * GDN_WIDTH // n),
        in_specs=[pl.BlockSpec((s, d), lambda i, j: (i, 0)),
                  pl.BlockSpec((d, n), lambda i, j: (0, j)),
                  pl.BlockSpec((GDN_CONV, n), lambda i, j: (0, jnp.minimum(j, n_conv_blocks - 1)))],
        out_specs=pl.BlockSpec((s, n), lambda i, j: (i, j)),
        scratch_shapes=[pltpu.VMEM((s + 2 * GDN_CONV_PAD, n), F32)],
        compiler_params=_cparams("parallel", "parallel"),
        name="gdn_inproj",
    )(xf, w_main_bf16, conv_w)


def _gdn_core_kernel(q_ref, k_ref, v_ref, z_ref, gt_ref, rows_ref, nw_ref, o_ref,
                     u_ref, w_ref, qd_ref, kd_ref, aqk_ref, dend_ref, of_ref, ob_ref):
    s = q_ref.shape[0]
    dk = GDN_HEAD_DIM
    hps = q_ref.shape[1] // dk
    c = GDN_CHUNK
    n_chunks = s // c
    first_head = pl.program_id(1) * hps
    slab = min(s, 256)
    head_cols = [slice(hd * dk, (hd + 1) * dk) for hd in range(hps)]

    lane = lax.broadcasted_iota(I32, (c, 2 * c), 1)
    row = lax.broadcasted_iota(I32, (c, 2 * c), 0)
    is_b = lane >= c
    col = jnp.where(is_b, lane - c, lane)
    ahead = jnp.where(is_b, col - row, row - col)
    incl = ahead >= 0
    strict = ahead > 0
    eye2 = jnp.where(row == col, 1.0, 0.0)
    shifts = [lax.rem(2 * V7X_LANES - first_head - hd, V7X_LANES) for hd in range(hps)]

    group = min(GDN_PREP_GROUP, n_chunks)
    n_groups = n_chunks // group

    def pair_rows(t):
        if isinstance(t, int):
            return pl.ds(t * c, c), pl.ds((n_chunks - 1 - t) * c, c)
        return pl.ds(pl.multiple_of(t * c, c), c), pl.ds(pl.multiple_of((n_chunks - 1 - t) * c, c), c)

    def prep_stages(hd, pg, slot):
        rows = [pair_rows(pg * group + i) for i in range(group)]
        cols = head_cols[hd]
        q_f, k_f, v_f = ([ref[r[0], cols].astype(F32) for r in rows] for ref in (q_ref, k_ref, v_ref))
        q_b, k_b, v_b = ([ref[r[1], cols].astype(F32) for r in rows] for ref in (q_ref, k_ref, v_ref))
        gt_f = [pltpu.roll(gt_ref[r[0], :], shifts[hd], axis=1) for r in rows]
        gt_b = [pltpu.roll(gt_ref[r[1], :], shifts[hd], axis=1) for r in rows]
        g_f = [t[:, 0:1] for t in gt_f]
        b_f = [t[:, 2 * GDN_HEADS:2 * GDN_HEADS + 1] for t in gt_f]
        g_b = [t[:, GDN_HEADS:GDN_HEADS + 1] for t in gt_b]
        b_b = [t[:, 3 * GDN_HEADS:3 * GDN_HEADS + 1] for t in gt_b]
        head = pl.ds(first_head + hd, 1)
        g_row, beta_row, be_row = [], [], []
        for i in range(group):
            t = pg * group + i
            row_f, row_b = rows_ref[t, head, :], rows_ref[n_chunks - 1 - t, head, :]
            pick = lambda j: jnp.where(is_b[0:1], row_b[:, j * 2 * c:(j + 1) * 2 * c], row_f[:, j * 2 * c:(j + 1) * 2 * c])
            g_row.append(pick(0))
            beta_row.append(pick(1))
            be_row.append(pick(2))
        dmat = [jnp.where(is_b, g_b[i], g_f[i]) - g_row[i] for i in range(group)]
        prod_f, prod_b = [], []
        for i in range(group):
            kfb, kbb = k_f[i].astype(BF16), k_b[i].astype(BF16)
            keys = jnp.concatenate([kfb, kbb], axis=0)
            prod_f.append(_dot_nt(jnp.concatenate([q_f[i].astype(BF16), kfb], axis=0), keys))
            yield
            prod_b.append(_dot_nt(jnp.concatenate([q_b[i].astype(BF16), kbb], axis=0), keys))
            yield
        qk2 = [jnp.where(is_b, prod_b[i][:c], prod_f[i][:c]) for i in range(group)]
        kk2 = [jnp.where(is_b, prod_b[i][c:], prod_f[i][c:]) for i in range(group)]
        gamma = [jnp.exp(jnp.where(incl, d_i, NEG_BIG)) for d_i in dmat]
        m = [jnp.where(strict, kk2[i] * gamma[i], 0.0) * jnp.where(is_b, b_b[i], b_f[i]) for i in range(group)]
        x_inv = [eye2 - m_i for m_i in m]
        p = []
        for m_i in m:
            p.append(_dot(m_i.astype(BF16), _block_diag2(m_i, is_b).astype(BF16)))
            yield
        for _ in range(5):
            y = []
            for i in range(group):
                y.append(_dot(jnp.concatenate([x_inv[i], p[i]], axis=0).astype(BF16),
                              _block_diag2(p[i], is_b).astype(BF16)))
                yield
            x_inv = [x_inv[i] + y[i][:c] for i in range(group)]
            p = [y_i[c:] for y_i in y]
        eg_f = [jnp.exp(g) for g in g_f]
        eg_b = [jnp.exp(g) for g in g_b]
        u_sol, w_sol = [], []
        for i in range(group):
            u_sol.append(_dot(_block_diag2(x_inv[i] * beta_row[i], is_b).astype(BF16),
                              jnp.concatenate([v_f[i], v_b[i]], axis=0).astype(BF16)))
            yield
            w_sol.append(_dot(_block_diag2(x_inv[i] * be_row[i], is_b).astype(BF16),
                              jnp.concatenate([k_f[i], k_b[i]], axis=0).astype(BF16)))
            yield
        for i in range(group):
            local = pl.ds(i * c, c)
            gl_f, gl_b = g_f[i][c - 1:c, :], g_b[i][0:1, :]
            for d, (qc, kc, eg, gl, g_d) in enumerate(((q_f[i], k_f[i], eg_f[i], gl_f, g_f[i]),
                                                       (q_b[i], k_b[i], eg_b[i], gl_b, g_b[i]))):
                u_ref[hd, slot, d, local, :] = u_sol[i][d * c:(d + 1) * c]
                w_ref[hd, slot, d, local, :] = w_sol[i][d * c:(d + 1) * c]
                qd_ref[hd, slot, d, local, :] = qc * eg
                kd_ref[hd, slot, d, local, :] = kc * jnp.exp(gl - g_d)
            aqk_ref[hd, slot, local, :] = jnp.where(incl, qk2[i] * gamma[i], 0.0)
            dend_ref[hd, slot, pl.ds(i * V7X_SUBLANES, V7X_SUBLANES), :] = jnp.where(
                is_b[:V7X_SUBLANES], jnp.exp(gl_b), jnp.exp(gl_f))

    lanes = (slice(0, c), slice(c, 2 * c))
    chains = [(hd, d) for hd in range(hps) for d in range(2)]

    def run_group(pg, slot, prep_gens, states):
        per_gap = -(-GDN_PREP_MATMULS // 2)

        def advance():
            for _ in range(per_gap):
                for gen in prep_gens:
                    next(gen, None)

        for i in range(group):
            rows = pair_rows(pg * group + i)
            local = pl.ds(i * c, c)
            stb = [st.astype(BF16) for st in states]
            ws = [_dot(w_ref[hd, slot, d, local, :].astype(BF16), stb[n]) for n, (hd, d) in enumerate(chains)]
            qs = [_dot(qd_ref[hd, slot, d, local, :].astype(BF16), stb[n]) for n, (hd, d) in enumerate(chains)]
            advance()
            vnb = [(u_ref[hd, slot, d, local, :] - ws[n]).astype(BF16) for n, (hd, d) in enumerate(chains)]
            av = [_dot(aqk_ref[hd, slot, local, :][:, lanes[d]].astype(BF16), vnb[n])
                  for n, (hd, d) in enumerate(chains)]
            kv = [_dot_tn(kd_ref[hd, slot, d, local, :].astype(BF16), vnb[n]) for n, (hd, d) in enumerate(chains)]
            advance()
            new_states = []
            for n, (hd, d) in enumerate(chains):
                (of_ref, ob_ref)[d][hd, rows[d], :] = qs[n] + av[n]
                dend = dend_ref[hd, slot, pl.ds(i * V7X_SUBLANES, 1), :][:, d * c:d * c + 1]
                new_states.append(states[n] * dend + kv[n])
            states = tuple(new_states)
        for gen in prep_gens:
            for _ in gen:
                pass
        return states

    for hd in range(hps):
        for _ in prep_stages(hd, 0, 0):
            pass

    def fused(tg, states):
        slot = lax.rem(tg, 2)
        return run_group(tg, slot, [prep_stages(hd, tg + 1, 1 - slot) for hd in range(hps)], states)

    zero_state = jnp.zeros((dk, dk), F32)
    states = lax.fori_loop(0, n_groups - 1, fused, (zero_state,) * len(chains))
    run_group(n_groups - 1, (n_groups - 1) % 2, [], states)

    for hd in range(hps):
        cols = slice(hd * dk, (hd + 1) * dk)
        for r0 in range(0, s, slab):
            rs = pl.ds(r0, slab)
            o = of_ref[hd, rs, :] + ob_ref[hd, rs, :]
            o = o * lax.rsqrt(jnp.mean(o * o, axis=-1, keepdims=True) + 1e-6) * nw_ref[...]
            o_ref[rs, cols] = (o * _silu(z_ref[rs, cols].astype(F32))).astype(o_ref.dtype)


def _gdn_core(proj, gates, gate_rows, norm_w, b, s):
    dk = GDN_HEAD_DIM
    hps = GDN_HEADS_PER_STEP
    nhb = GDN_HEADS // hps
    blk = lambda kind: pl.BlockSpec((s, hps * dk), lambda i, h: (i, kind * nhb + h))
    group_rows = min(GDN_PREP_GROUP, s // GDN_CHUNK) * GDN_CHUNK
    seq = lambda: pltpu.VMEM((hps, s, dk), F32)
    slots = lambda: pltpu.VMEM((hps, 2, 2, group_rows, dk), F32)
    return pl.pallas_call(
        _gdn_core_kernel,
        out_shape=jax.ShapeDtypeStruct((b * s, GDN_WIDTH), BF16),
        grid=(b, nhb),
        in_specs=[blk(0), blk(1), blk(2), blk(3),
                  pl.BlockSpec((s, GDN_GATE_LANES), lambda i, h: (i, 0)),
                  pl.BlockSpec((s // GDN_CHUNK, GDN_HEADS, 3 * GDN_GATE_LANES), lambda i, h: (i, 0, 0)),
                  pl.BlockSpec((1, dk), lambda i, h: (0, 0))],
        out_specs=pl.BlockSpec((s, hps * dk), lambda i, h: (i, h)),
        scratch_shapes=[slots(), slots(), slots(), slots(),
                        pltpu.VMEM((hps, 2, group_rows, 2 * GDN_CHUNK), F32),
                        pltpu.VMEM((hps, 2, group_rows // GDN_CHUNK * V7X_SUBLANES, 2 * GDN_CHUNK), F32),
                        seq(), seq()],
        compiler_params=_cparams("parallel", "parallel"),
        name="gdn_core",
    )(proj, proj, proj, proj, gates, gate_rows, norm_w.reshape(1, dk))


def _gdn_mixer_core(xf, w_in, conv_w, a_log, dt_bias, norm_w, b, s):
    n_main = 4 * GDN_WIDTH
    n_gate = 4 * GDN_HEADS
    proj = _gdn_inproj(xf, w_in[:, :n_main].astype(BF16), conv_w.astype(F32), b, s)
    w_gate = jnp.pad(w_in[:, n_main:], ((0, 0), (0, GDN_GATE_LANES - n_gate))).astype(BF16)
    lane_pad = lambda v: jnp.pad(v.astype(F32).reshape(1, 2 * GDN_HEADS), ((0, 0), (0, GDN_GATE_LANES - 2 * GDN_HEADS)))
    gates, gate_rows = _gdn_gates(xf, w_gate, lane_pad(a_log), lane_pad(dt_bias), b, s)
    return _gdn_core(proj, gates, gate_rows, norm_w.astype(F32), b, s)


def kernel(x, a_w_in, a_conv, a_A_log, a_dt_bias, a_norm_w, a_w_out, b_w_in, b_sink, b_w_out, rel_bias, c_w_in, c_w_group, c_scale, c_w_out, router_w, router_b, moe_w_gate, moe_w_up, moe_w_down, ln_g, ln_b):
    b, s, d = x.shape
    t = b * s
    xf = x.reshape(t, d)
    xb = xf.astype(BF16)
    router_wt = router_w.T.astype(BF16)
    router_b_col = router_b.reshape(N_EXPERTS, 1).astype(F32)
    bias_tbl = _attn_bias_table(rel_bias)
    for i in range(DEPTH):
        kind, j = i % N_MIXERS, i // N_MIXERS
        if kind == 0:
            h = _gdn_mixer_core(xb, a_w_in[j], a_conv[j], a_A_log[j], a_dt_bias[j], a_norm_w[j], b, s)
            w_out = a_w_out[j]
        elif kind == 1:
            proj = _linear(xb, b_w_in[j].astype(BF16), BF16)
            h = _attn_core(proj, bias_tbl, b_sink[j], b, s)
            w_out = b_w_out[j]
        else:
            u = _linear(xb, c_w_in[j].astype(BF16), F32)
            h = _pool_core(u.reshape(b, s, d), c_w_group[j].astype(BF16), c_scale[j]).reshape(t, d)
            w_out = c_w_out[j]
        x1, route_i, gate_cols, counts = _post_mixer(h, w_out.astype(BF16), xf, ln_g[i, 0], ln_b[i, 0],
                                                     router_wt, router_b_col)
        xf, xb = _moe_layer(x1, route_i, gate_cols, counts[:, 0], moe_w_gate[i], moe_w_up[i], moe_w_down[i],
                            ln_g[i, 1], ln_b[i, 1])
    return xf.reshape(b, s, d)
```

```python
import functools
import math

import jax
import jax.numpy as jnp
from jax import lax
from jax.experimental import pallas as pl
from jax.experimental.pallas import tpu as pltpu

F32 = jnp.float32
BF16 = jnp.bfloat16
I32 = jnp.int32

D_MODEL = 1024
DEPTH = 4
N_MIXERS = 3
DEEPNORM_ALPHA = (2.0 * DEPTH) ** 0.25
LN_EPS = 1e-5

GDN_HEADS = 8
GDN_HEAD_DIM = 128
GDN_WIDTH = GDN_HEADS * GDN_HEAD_DIM
GDN_CONV = 5
GDN_CHUNK = 64

ATT_HEADS = 16
ATT_KV_HEADS = 4
ATT_HEAD_DIM = 64
ATT_GROUP = ATT_HEADS // ATT_KV_HEADS
ATT_WINDOW = 128
ATT_QBLOCK = 128
ATT_KSPAN = ATT_QBLOCK + 2 * ATT_WINDOW
REL_BUCKETS = 32
REL_MAX_DIST = 128

POOL_WINDOWS = (2, 4, 8, 16)
POOL_GROUP_DIM = D_MODEL // len(POOL_WINDOWS)

N_EXPERTS = 16
N_EXPERT_GROUPS = 4
EXPERTS_PER_GROUP = N_EXPERTS // N_EXPERT_GROUPS
TOP_K = 2
EXPERT_DIM = 512
PAIRS_PER_GROUP = EXPERTS_PER_GROUP * (EXPERTS_PER_GROUP - 1) // 2
N_SEGMENTS = N_EXPERT_GROUPS * PAIRS_PER_GROUP

V7X_LANES = 128
V7X_SUBLANES = 8
V7X_VMEM_LIMIT_BYTES = 56 * 1024 * 1024

ROW_TILE = 512
MOE_BLOCK = 256
ROW_EXTRA = 128
NEG_BIG = -1e30

HIGHEST = lax.Precision.HIGHEST


def _cparams(*sem):
    return pltpu.CompilerParams(dimension_semantics=tuple(sem), vmem_limit_bytes=V7X_VMEM_LIMIT_BYTES)


def _dot(a, b, **kw):
    return jnp.dot(a, b, preferred_element_type=F32, **kw)


def _dot_nt(a, b, **kw):
    return lax.dot_general(a, b, (((1,), (1,)), ((), ())), preferred_element_type=F32, **kw)


def _dot_tn(a, b, **kw):
    return lax.dot_general(a, b, (((0,), (0,)), ((), ())), preferred_element_type=F32, **kw)


def _silu(x):
    return x * jax.nn.sigmoid(x)


def _linear_kernel(x_ref, w_ref, o_ref):
    o_ref[...] = _dot(x_ref[...].astype(BF16), w_ref[...]).astype(o_ref.dtype)


def _linear(x, w_bf16, out_dtype, tm=ROW_TILE):
    m, k = x.shape
    n = w_bf16.shape[1]
    return pl.pallas_call(
        _linear_kernel,
        out_shape=jax.ShapeDtypeStruct((m, n), out_dtype),
        grid=(m // tm,),
        in_specs=[pl.BlockSpec((tm, k), lambda i: (i, 0)), pl.BlockSpec((k, n), lambda i: (0, 0))],
        out_specs=pl.BlockSpec((tm, n), lambda i: (i, 0)),
        compiler_params=_cparams("parallel"),
        name="linear",
    )(x, w_bf16)


def _layer_norm_rows(s, g, b):
    mu = jnp.mean(s, axis=-1, keepdims=True)
    xc = s - mu
    var = jnp.mean(xc * xc, axis=-1, keepdims=True)
    return xc * lax.rsqrt(var + LN_EPS) * g + b


def _top2_of4(a, b, c, d):
    hi1, lo1 = jnp.maximum(a, b), jnp.minimum(a, b)
    hi2, lo2 = jnp.maximum(c, d), jnp.minimum(c, d)
    return jnp.maximum(hi1, hi2) + jnp.maximum(jnp.minimum(hi1, hi2), jnp.maximum(lo1, lo2))


def _route_rows(scores, biased):
    bi = [biased[e:e + 1, :] for e in range(N_EXPERTS)]
    sc = [scores[e:e + 1, :] for e in range(N_EXPERTS)]
    gs = [_top2_of4(*bi[4 * g:4 * g + 4]) for g in range(N_EXPERT_GROUPS)]
    group = jnp.zeros_like(gs[0], dtype=I32)
    best = gs[0]
    for g in range(1, N_EXPERT_GROUPS):
        upd = gs[g] > best
        group = jnp.where(upd, g, group)
        best = jnp.where(upd, gs[g], best)

    def pick(rows, j):
        out = rows[j]
        for g in range(1, N_EXPERT_GROUPS):
            out = jnp.where(group == g, rows[4 * g + j], out)
        return out

    v = [pick(bi, j) for j in range(EXPERTS_PER_GROUP)]
    s = [pick(sc, j) for j in range(EXPERTS_PER_GROUP)]
    i1 = jnp.zeros_like(group)
    b1 = v[0]
    for j in range(1, EXPERTS_PER_GROUP):
        upd = v[j] > b1
        i1 = jnp.where(upd, j, i1)
        b1 = jnp.where(upd, v[j], b1)
    i2 = jnp.full_like(group, -1)
    b2 = jnp.full_like(b1, -jnp.inf)
    for j in range(EXPERTS_PER_GROUP):
        upd = (i1 != j) & ((v[j] > b2) | (i2 < 0))
        i2 = jnp.where(upd, j, i2)
        b2 = jnp.where(upd, v[j], b2)
    s1 = s[0]
    s2 = s[0]
    for j in range(1, EXPERTS_PER_GROUP):
        s1 = jnp.where(i1 == j, s[j], s1)
        s2 = jnp.where(i2 == j, s[j], s2)
    den = s1 + s2
    first_is_lo = i1 < i2
    lo = jnp.where(first_is_lo, i1, i2)
    hi = jnp.where(first_is_lo, i2, i1)
    pair = lax.shift_right_logical(lo * (7 - lo), 1) + hi - lo - 1
    g1, g2 = s1 / den, s2 / den
    return group * PAIRS_PER_GROUP + pair, jnp.where(first_is_lo, g1, g2), jnp.where(first_is_lo, g2, g1)


def _post_mixer_kernel(h_ref, w_ref, x_ref, g_ref, b_ref, rwt_ref, rb_ref, ustrict_ref, eye_ref,
                       x1w_ref, ri_ref, cnt_ref, base_ref):
    tm, d = x_ref.shape

    @pl.when(pl.program_id(0) == 0)
    def _():
        base_ref[...] = jnp.zeros_like(base_ref)

    y = _dot(h_ref[...], w_ref[...])
    x1 = _layer_norm_rows(DEEPNORM_ALPHA * x_ref[...] + y, g_ref[...], b_ref[...])
    x1w_ref[:, 0:d] = x1

    logits = _dot_nt(rwt_ref[...], x1.astype(BF16))
    scores = jax.nn.sigmoid(logits)
    seg, g_lo, g_hi = _route_rows(scores, scores + rb_ref[...])

    sidx = lax.broadcasted_iota(I32, (N_SEGMENTS, tm), 0)
    hit = sidx == seg
    onehot = jnp.where(hit, 1.0, 0.0)
    base = base_ref[:, 0:1]
    before = _dot(onehot.astype(BF16), ustrict_ref[...]) + base
    rank = jnp.sum(jnp.where(hit, before, 0.0), axis=0, keepdims=True)
    new_base = base + jnp.sum(onehot, axis=1, keepdims=True)
    base_ref[...] = jnp.broadcast_to(new_base, base_ref.shape)
    cnt_ref[...] = jnp.broadcast_to(new_base, cnt_ref.shape)

    ri_ref[...] = jnp.concatenate([seg, rank.astype(I32), jnp.zeros((6, tm), I32)], axis=0)
    gates = jnp.concatenate([g_lo, g_hi, jnp.zeros((ROW_EXTRA - 2, tm), F32)], axis=0)
    hi = gates.astype(BF16)
    lo = (gates - hi.astype(F32)).astype(BF16)
    x1w_ref[:, d:d + ROW_EXTRA] = _dot_nt(eye_ref[...], hi) + _dot_nt(eye_ref[...], lo)


def _post_mixer(h_bf16, w_out_bf16, x, ln_g, ln_b, router_wt, router_b_col, tm=ROW_TILE):
    t, d = x.shape
    ustrict = (jnp.arange(tm)[:, None] < jnp.arange(tm)[None, :]).astype(BF16)
    eye = jnp.eye(tm, dtype=BF16)
    full = lambda shape: pl.BlockSpec(shape, lambda i: (0,) * len(shape))
    return pl.pallas_call(
        _post_mixer_kernel,
        out_shape=(jax.ShapeDtypeStruct((t, d + ROW_EXTRA), F32),
                   jax.ShapeDtypeStruct((8, t), I32),
                   jax.ShapeDtypeStruct((N_SEGMENTS, V7X_LANES), F32)),
        grid=(t // tm,),
        in_specs=[pl.BlockSpec((tm, h_bf16.shape[1]), lambda i: (i, 0)),
                  full(w_out_bf16.shape),
                  pl.BlockSpec((tm, d), lambda i: (i, 0)),
                  full((1, d)), full((1, d)),
                  full((N_EXPERTS, d)), full((N_EXPERTS, 1)),
                  full((tm, tm)), full((tm, tm))],
        out_specs=(pl.BlockSpec((tm, d + ROW_EXTRA), lambda i: (i, 0)),
                   pl.BlockSpec((8, tm), lambda i: (0, i)),
                   pl.BlockSpec((N_SEGMENTS, V7X_LANES), lambda i: (0, 0))),
        scratch_shapes=[pltpu.VMEM((N_SEGMENTS, V7X_LANES), F32)],
        compiler_params=_cparams("arbitrary"),
        name="post_mixer_router",
    )(h_bf16, w_out_bf16, x, ln_g.reshape(1, d), ln_b.reshape(1, d), router_wt, router_b_col, ustrict, eye)


def _row_copy(src_ref, src_row, dst_ref, dst_row, sem):
    return pltpu.make_async_copy(src_ref.at[pl.ds(src_row, 1), :], dst_ref.at[pl.ds(dst_row, 1), :], sem)


def _dispatch_kernel(n_tiles, pad_lo_ref, pad_hi_ref, x_ref, dest_hbm, xs_hbm,
                     idx_smem, zero_ref, idx_sem, row_sem, pad_sem):
    tm = x_ref.shape[0]
    i = pl.program_id(0)
    slot = lax.rem(i, 2)

    def idx_copy(tile, sl):
        return pltpu.make_async_copy(dest_hbm.at[tile], idx_smem.at[sl], idx_sem.at[sl])

    @pl.when(i == 0)
    def _():
        idx_copy(0, 0).start()

    idx_copy(i, slot).wait()

    @pl.when(i + 1 < n_tiles)
    def _():
        idx_copy(i + 1, 1 - slot).start()

    def issue(j, carry):
        _row_copy(x_ref, j, xs_hbm, idx_smem[slot, 0, j], row_sem).start()
        return carry

    lax.fori_loop(0, tm, issue, 0, unroll=8)

    def drain(j, carry):
        _row_copy(x_ref, j, xs_hbm, idx_smem[slot, 0, j], row_sem).wait()
        return carry

    lax.fori_loop(0, tm, drain, 0, unroll=8)

    @pl.when(i == pl.num_programs(0) - 1)
    def _():
        zero_ref[...] = jnp.zeros_like(zero_ref)
        for e in range(N_SEGMENTS + 1):
            def fill(r, carry):
                _row_copy(zero_ref, 0, xs_hbm, r, pad_sem).start()
                return carry

            lax.fori_loop(pad_lo_ref[e], pad_hi_ref[e], fill, 0)
        for e in range(N_SEGMENTS + 1):
            def filled(r, carry):
                _row_copy(zero_ref, 0, xs_hbm, r, pad_sem).wait()
                return carry

            lax.fori_loop(pad_lo_ref[e], pad_hi_ref[e], filled, 0)


def _dispatch(x1w, dest_tiles, pad_lo, pad_hi, n_rows, tm=ROW_TILE):
    t, d = x1w.shape
    grid_spec = pltpu.PrefetchScalarGridSpec(
        num_scalar_prefetch=2,
        grid=(t // tm,),
        in_specs=[pl.BlockSpec((tm, d), lambda i, lo, hi: (i, 0)),
                  pl.BlockSpec(memory_space=pl.ANY)],
        out_specs=pl.BlockSpec(memory_space=pl.ANY),
        scratch_shapes=[pltpu.SMEM((2, 1, tm), I32), pltpu.VMEM((V7X_SUBLANES, d), F32),
                        pltpu.SemaphoreType.DMA((2,)), pltpu.SemaphoreType.DMA, pltpu.SemaphoreType.DMA],
    )
    return pl.pallas_call(
        functools.partial(_dispatch_kernel, t // tm),
        out_shape=jax.ShapeDtypeStruct((n_rows, d), F32),
        grid_spec=grid_spec,
        compiler_params=_cparams("arbitrary"),
        name="moe_dispatch",
    )(pad_lo, pad_hi, x1w, dest_tiles)


def _expert_kernel(ea_ref, eb_ref, xs_ref, wga_ref, wua_ref, wda_ref, wgb_ref, wub_ref, wdb_ref, ys_ref):
    del ea_ref, eb_ref
    d = ys_ref.shape[1]
    x = xs_ref[:, 0:d].astype(BF16)
    extra = xs_ref[:, d:d + ROW_EXTRA]
    y = None
    for col, (wg_ref, wu_ref, wd_ref) in enumerate(((wga_ref, wua_ref, wda_ref), (wgb_ref, wub_ref, wdb_ref))):
        hidden = _silu(_dot(x, wg_ref[0].astype(BF16))) * _dot(x, wu_ref[0].astype(BF16))
        part = _dot((hidden * extra[:, col:col + 1]).astype(BF16), wd_ref[0].astype(BF16))
        y = part if y is None else y + part
    ys_ref[...] = y


def _experts(block_ea, block_eb, xs, wg, wu, wd):
    n_rows = xs.shape[0]
    d = xs.shape[1] - ROW_EXTRA
    n_blocks = n_rows // MOE_BLOCK
    f = wg.shape[2]
    up = lambda which: pl.BlockSpec((1, d, f), lambda i, ea, eb: ((ea, eb)[which][i], 0, 0))
    down = lambda which: pl.BlockSpec((1, f, d), lambda i, ea, eb: ((ea, eb)[which][i], 0, 0))
    grid_spec = pltpu.PrefetchScalarGridSpec(
        num_scalar_prefetch=2,
        grid=(n_blocks,),
        in_specs=[pl.BlockSpec((MOE_BLOCK, d + ROW_EXTRA), lambda i, ea, eb: (i, 0)),
                  up(0), up(0), down(0), up(1), up(1), down(1)],
        out_specs=pl.BlockSpec((MOE_BLOCK, d), lambda i, ea, eb: (i, 0)),
    )
    return pl.pallas_call(
        _expert_kernel,
        out_shape=jax.ShapeDtypeStruct((n_rows, d), F32),
        grid_spec=grid_spec,
        compiler_params=_cparams("parallel"),
        name="moe_experts",
    )(block_ea, block_eb, xs, wg, wu, wd, wg, wu, wd)


def _combine_kernel(n_tiles, x1w_ref, g_ref, b_ref, dest_hbm, ys_hbm, o_ref, ob_ref,
                    idx_smem, ybuf, idx_sem, row_sem):
    tm, d = o_ref.shape
    i = pl.program_id(0)
    slot = lax.rem(i, 2)
    other = 1 - slot

    def idx_copy(tile, sl):
        return pltpu.make_async_copy(dest_hbm.at[tile], idx_smem.at[sl], idx_sem.at[sl])

    def gather(sl, start):
        def body(j, carry):
            cp = _row_copy(ys_hbm, idx_smem[sl, 0, j], ybuf.at[sl], j, row_sem.at[sl])
            if start:
                cp.start()
            else:
                cp.wait()
            return carry

        lax.fori_loop(0, tm, body, 0, unroll=8)

    @pl.when(i == 0)
    def _():
        first = idx_copy(0, 0)
        first.start()
        first.wait()
        gather(0, True)
        if n_tiles > 1:
            idx_copy(1, 1).start()

    @pl.when(i + 1 < n_tiles)
    def _():
        idx_copy(i + 1, other).wait()
        gather(other, True)

    gather(slot, False)

    @pl.when(i + 2 < n_tiles)
    def _():
        idx_copy(i + 2, slot).start()

    x2 = _layer_norm_rows(DEEPNORM_ALPHA * x1w_ref[:, 0:d] + ybuf[slot], g_ref[...], b_ref[...])
    o_ref[...] = x2
    ob_ref[...] = x2.astype(ob_ref.dtype)


def _combine(x1w, ln_g, ln_b, dest_tiles, ys, tm=ROW_TILE):
    t = x1w.shape[0]
    d = ys.shape[1]
    full = lambda shape: pl.BlockSpec(shape, lambda i: (0,) * len(shape))
    return pl.pallas_call(
        functools.partial(_combine_kernel, t // tm),
        out_shape=(jax.ShapeDtypeStruct((t, d), F32), jax.ShapeDtypeStruct((t, d), BF16)),
        grid=(t // tm,),
        in_specs=[pl.BlockSpec((tm, d + ROW_EXTRA), lambda i: (i, 0)),
                  full((1, d)), full((1, d)),
                  pl.BlockSpec(memory_space=pl.ANY),
                  pl.BlockSpec(memory_space=pl.ANY)],
        out_specs=(pl.BlockSpec((tm, d), lambda i: (i, 0)), pl.BlockSpec((tm, d), lambda i: (i, 0))),
        scratch_shapes=[pltpu.SMEM((2, 1, tm), I32), pltpu.VMEM((2, tm, d), F32),
                        pltpu.SemaphoreType.DMA((2,)), pltpu.SemaphoreType.DMA((2,))],
        compiler_params=_cparams("arbitrary"),
        name="moe_combine_ln",
    )(x1w, ln_g.reshape(1, d), ln_b.reshape(1, d), dest_tiles, ys)


def _moe_layer(x1w, route_i, counts, wg, wu, wd, ln_g, ln_b, tm=ROW_TILE):
    t = x1w.shape[0]
    n_tiles = t // tm
    n_rows = t + N_SEGMENTS * MOE_BLOCK
    counts = counts.astype(I32)
    padded = (counts + MOE_BLOCK - 1) // MOE_BLOCK * MOE_BLOCK
    pends = jnp.cumsum(padded)
    pstarts = pends - padded
    seg = route_i[0:1]
    start_of = jnp.sum(jnp.where(seg[None] == jnp.arange(N_SEGMENTS, dtype=I32)[:, None, None],
                                 pstarts[:, None, None], 0), axis=0)
    dest = start_of + route_i[1:2]
    dest_tiles = dest.reshape(1, n_tiles, tm).transpose(1, 0, 2)
    n_blocks = n_rows // MOE_BLOCK
    block_seg = jnp.minimum(
        jnp.sum((jnp.arange(n_blocks, dtype=I32) * MOE_BLOCK)[:, None] >= pends[None, :], axis=-1),
        N_SEGMENTS - 1).astype(I32)
    pair_lo = jnp.array([lo for lo in range(EXPERTS_PER_GROUP) for hi in range(lo + 1, EXPERTS_PER_GROUP)], I32)
    pair_hi = jnp.array([hi for lo in range(EXPERTS_PER_GROUP) for hi in range(lo + 1, EXPERTS_PER_GROUP)], I32)
    block_group, block_pair = block_seg // PAIRS_PER_GROUP, block_seg % PAIRS_PER_GROUP
    block_ea = block_group * EXPERTS_PER_GROUP + pair_lo[block_pair]
    block_eb = block_group * EXPERTS_PER_GROUP + pair_hi[block_pair]
    pad_lo = jnp.concatenate([pstarts + counts, pends[-1:]]).astype(I32)
    pad_hi = jnp.concatenate([pends, jnp.full((1,), n_rows, I32)]).astype(I32)
    xs = _dispatch(x1w, dest_tiles, pad_lo, pad_hi, n_rows, tm)
    ys = _experts(block_ea, block_eb, xs, wg, wu, wd)
    return _combine(x1w, ln_g, ln_b, dest_tiles, ys, tm)


POOL_PAD = 16


def _pool_kernel(u_ref, wg_ref, scale_ref, o_ref, pad_ref):
    s, c = u_ref.shape[1], u_ref.shape[2]
    u = u_ref[0]
    zeros = jnp.zeros((POOL_PAD, c), F32)
    pad_ref[pl.ds(0, POOL_PAD), :] = zeros
    pad_ref[pl.ds(POOL_PAD + s, POOL_PAD), :] = zeros
    pad_ref[pl.ds(POOL_PAD, s), :] = u
    pos = lax.broadcasted_iota(I32, (s, c), 0)
    group = pl.program_id(1)
    for gi, win in enumerate(POOL_WINDOWS):
        @pl.when(group == gi)
        def _(win=win):
            half = win // 2
            total = pad_ref[pl.ds(POOL_PAD - half, s), :]
            for j in range(1 - half, half):
                total = total + pad_ref[pl.ds(POOL_PAD + j, s), :]
            count = (jnp.minimum(pos + half, s) - jnp.maximum(pos - half, 0)).astype(F32)
            mixed = total / count - u
            y = _dot(mixed.astype(BF16), wg_ref[0]) * scale_ref[...]
            o_ref[0] = y.astype(o_ref.dtype)


def _pool_core(u, w_group_bf16, scale):
    b, s, d = u.shape
    c = POOL_GROUP_DIM
    return pl.pallas_call(
        _pool_kernel,
        out_shape=jax.ShapeDtypeStruct((b, s, d), BF16),
        grid=(b, d // c),
        in_specs=[pl.BlockSpec((1, s, c), lambda i, g: (i, 0, g)),
                  pl.BlockSpec((1, c, c), lambda i, g: (g, 0, 0)),
                  pl.BlockSpec((1, c), lambda i, g: (0, g))],
        out_specs=pl.BlockSpec((1, s, c), lambda i, g: (i, 0, g)),
        scratch_shapes=[pltpu.VMEM((s + 2 * POOL_PAD, c), F32)],
        compiler_params=_cparams("parallel", "parallel"),
        name="pool_core",
    )(u, w_group_bf16, scale.reshape(1, d))


def _t5_bucket(rel):
    half = REL_BUCKETS // 2
    max_exact = half // 2
    n = jnp.abs(rel)
    large = max_exact + (jnp.log(jnp.maximum(n, 1).astype(F32) / max_exact)
                         / math.log(REL_MAX_DIST / max_exact) * (half - max_exact)).astype(I32)
    large = jnp.minimum(large, half - 1)
    return (rel > 0).astype(I32) * half + jnp.where(n < max_exact, n, large)


def _attn_bias_table(rel_bias):
    rel = jnp.arange(ATT_KSPAN)[None, :] - ATT_WINDOW - jnp.arange(ATT_QBLOCK)[:, None]
    bias = jnp.transpose(rel_bias[_t5_bucket(rel)], (2, 0, 1)).astype(F32)
    return jnp.where((jnp.abs(rel) <= ATT_WINDOW)[None], bias, NEG_BIG)


def _attn_kernel(q_ref, kp_ref, kc_ref, kn_ref, vp_ref, vc_ref, vn_ref, bias_ref, sink_ref, o_ref):
    j = pl.program_id(1)
    nb = pl.num_programs(1)
    qb = ATT_QBLOCK
    col = lax.broadcasted_iota(I32, (qb, ATT_KSPAN), 1)
    edge = jnp.where(((j == 0) & (col < qb)) | ((j == nb - 1) & (col >= 2 * qb)), NEG_BIG, 0.0)
    k_all = jnp.concatenate([kp_ref[...], kc_ref[...], kn_ref[...]], axis=0)
    v_all = jnp.concatenate([vp_ref[...], vc_ref[...], vn_ref[...]], axis=0)
    q = q_ref[...]
    heads = lambda g: range(g * ATT_GROUP, (g + 1) * ATT_GROUP)
    cols = lambda i: slice(i * ATT_HEAD_DIM, (i + 1) * ATT_HEAD_DIM)

    def scores(g):
        return [_dot_nt(q[:, cols(hd)], k_all[:, cols(g)]) for hd in heads(g)]

    def attend(g, raw):
        outs = []
        probs, dens = [], []
        for hd, qk in zip(heads(g), raw):
            logits = qk * (ATT_HEAD_DIM ** -0.5) + bias_ref[hd] + edge
            sink = sink_ref[hd]
            mx = jnp.maximum(jnp.max(logits, axis=-1, keepdims=True), sink)
            p = jnp.exp(logits - mx)
            dens.append(jnp.sum(p, axis=-1, keepdims=True) + jnp.exp(sink - mx))
            probs.append(p.astype(BF16))
        for p, den in zip(probs, dens):
            outs.append(_dot(p, v_all[:, cols(g)]) / den)
        return outs

    outs = []
    raw = scores(0)
    for g in range(ATT_KV_HEADS):
        nxt = scores(g + 1) if g + 1 < ATT_KV_HEADS else None
        outs += attend(g, raw)
        raw = nxt
    o_ref[...] = jnp.concatenate(outs, axis=-1).astype(o_ref.dtype)


def _attn_core(proj, bias_tbl, sink, b, s):
    qw = ATT_HEADS * ATT_HEAD_DIM
    kw = ATT_KV_HEADS * ATT_HEAD_DIM
    nb = s // ATT_QBLOCK
    kcol, vcol = qw // kw, qw // kw + 1
    row = lambda i, j: i * nb + j
    prev = lambda i, j: i * nb + jnp.maximum(j - 1, 0)
    nxt = lambda i, j: i * nb + jnp.minimum(j + 1, nb - 1)
    kv_spec = lambda rowf, c: pl.BlockSpec((ATT_QBLOCK, kw), lambda i, j: (rowf(i, j), c))
    return pl.pallas_call(
        _attn_kernel,
        out_shape=jax.ShapeDtypeStruct((b * s, qw), BF16),
        grid=(b, nb),
        in_specs=[pl.BlockSpec((ATT_QBLOCK, qw), lambda i, j: (row(i, j), 0)),
                  kv_spec(prev, kcol), kv_spec(row, kcol), kv_spec(nxt, kcol),
                  kv_spec(prev, vcol), kv_spec(row, vcol), kv_spec(nxt, vcol),
                  pl.BlockSpec((ATT_HEADS, ATT_QBLOCK, ATT_KSPAN), lambda i, j: (0, 0, 0)),
                  pl.BlockSpec(memory_space=pltpu.SMEM)],
        out_specs=pl.BlockSpec((ATT_QBLOCK, qw), lambda i, j: (row(i, j), 0)),
        compiler_params=_cparams("parallel", "parallel"),
        name="attn_core",
    )(proj, proj, proj, proj, proj, proj, proj, bias_tbl, sink.astype(F32))


GDN_GATE_LANES = V7X_LANES
GDN_CONV_PAD = 8
GDN_PREP_GROUP = 8
GDN_INPROJ_COLS = 512
GDN_INPROJ_SLAB = 512
GDN_HEADS_PER_STEP = 2
GDN_PREP_MATMULS = 10


def _softplus(x):
    return jnp.maximum(x, 0.0) + jnp.log(1.0 + jnp.exp(-jnp.abs(x)))


def _gdn_gates_kernel(x_ref, w_ref, alog_ref, dtb_ref, o_ref, rows_ref):
    s = x_ref.shape[0]
    c = GDN_CHUNK
    gl = _dot(x_ref[...].astype(BF16), w_ref[...])
    decay = -jnp.exp(alog_ref[...]) * _softplus(gl + dtb_ref[...])
    beta = jax.nn.sigmoid(gl)
    r = lax.broadcasted_iota(I32, (c, c), 0)
    q = lax.broadcasted_iota(I32, (c, c), 1)
    lower = jnp.where(r >= q, 1.0, 0.0)
    upper = jnp.where(r <= q, 1.0, 0.0)
    lane = lax.broadcasted_iota(I32, (c, GDN_GATE_LANES), 1)
    eye = jnp.where(lax.broadcasted_iota(I32, (GDN_GATE_LANES, GDN_GATE_LANES), 0)
                    == lax.broadcasted_iota(I32, (GDN_GATE_LANES, GDN_GATE_LANES), 1), 1.0, 0.0)
    hh = GDN_HEADS
    for n in range(s // c):
        d_c = decay[n * c:(n + 1) * c]
        pre = _dot(lower, d_c, precision=HIGHEST)
        suf = _dot(upper, d_c, precision=HIGHEST)
        tile = jnp.where(lane < hh, pre, jnp.where(lane < 2 * hh, suf, beta[n * c:(n + 1) * c]))
        o_ref[pl.ds(n * c, c), :] = tile
        t_rows = _dot_nt(eye, tile, precision=HIGHEST)
        g_rows = jnp.concatenate([t_rows[0:hh], t_rows[hh:2 * hh]], axis=1)
        b_rows = jnp.concatenate([t_rows[2 * hh:3 * hh], t_rows[3 * hh:4 * hh]], axis=1)
        rows_ref[n] = jnp.concatenate([g_rows, b_rows, b_rows * jnp.exp(g_rows)], axis=1)


def _gdn_gates(xf, w_gate_bf16, alog_vec, dtb_vec, b, s):
    d = xf.shape[1]
    n_chunks = s // GDN_CHUNK
    full = lambda shape: pl.BlockSpec(shape, lambda i: (0,) * len(shape))
    return pl.pallas_call(
        _gdn_gates_kernel,
        out_shape=(jax.ShapeDtypeStruct((b * s, GDN_GATE_LANES), F32),
                   jax.ShapeDtypeStruct((b * n_chunks, GDN_HEADS, 3 * GDN_GATE_LANES), F32)),
        grid=(b,),
        in_specs=[pl.BlockSpec((s, d), lambda i: (i, 0)), full((d, GDN_GATE_LANES)),
                  full((1, GDN_GATE_LANES)), full((1, GDN_GATE_LANES))],
        out_specs=(pl.BlockSpec((s, GDN_GATE_LANES), lambda i: (i, 0)),
                   pl.BlockSpec((n_chunks, GDN_HEADS, 3 * GDN_GATE_LANES), lambda i: (i, 0, 0))),
        compiler_params=_cparams("parallel"),
        name="gdn_gates",
    )(xf, w_gate_bf16, alog_vec, dtb_vec)


def _block_diag2(r, is_b):
    return jnp.concatenate([jnp.where(is_b, 0.0, r), jnp.where(is_b, r, 0.0)], axis=0)


def _gdn_inproj_kernel(x_ref, w_ref, cw_ref, o_ref, pad_ref):
    s = x_ref.shape[0]
    n = w_ref.shape[1]
    dk = GDN_HEAD_DIM
    kind = pl.program_id(1) // (GDN_WIDTH // n)
    slab = min(s, GDN_INPROJ_SLAB)
    n_slabs = s // slab
    project = lambda k: _dot(x_ref[pl.ds(k * slab, slab), :].astype(BF16), w_ref[...])

    @pl.when(kind == 3)
    def _():
        for k in range(n_slabs):
            o_ref[pl.ds(k * slab, slab), :] = project(k).astype(o_ref.dtype)

    @pl.when(kind < 3)
    def _():
        norm_on = kind < 2
        q_scale = jnp.where(kind == 0, GDN_HEAD_DIM ** -0.5, 1.0)
        zeros = jnp.zeros((GDN_CONV_PAD, n), F32)
        pad_ref[pl.ds(0, GDN_CONV_PAD), :] = zeros
        pad_ref[pl.ds(GDN_CONV_PAD + s, GDN_CONV_PAD), :] = zeros
        cw = cw_ref[...]

        def finish(k):
            r0 = k * slab
            acc = None
            for j in range(GDN_CONV):
                term = pad_ref[pl.ds(GDN_CONV_PAD + r0 + j - GDN_CONV // 2, slab), :] * cw[j:j + 1, :]
                acc = term if acc is None else acc + term
            act = _silu(acc)
            outs = []
            for hd in range(n // dk):
                a = act[:, hd * dk:(hd + 1) * dk]
                inv = lax.rsqrt(jnp.sum(a * a, axis=-1, keepdims=True) + 1e-6) * q_scale
                outs.append(a * jnp.where(norm_on, inv, 1.0))
            o_ref[pl.ds(r0, slab), :] = jnp.concatenate(outs, axis=1).astype(o_ref.dtype)

        for k in range(n_slabs):
            pad_ref[pl.ds(GDN_CONV_PAD + k * slab, slab), :] = project(k)
            if k >= 1:
                finish(k - 1)
        finish(n_slabs - 1)


def _gdn_inproj(xf, w_main_bf16, conv_w, b, s):
    d = xf.shape[1]
    n = GDN_INPROJ_COLS
    n_conv_blocks = 3 * GDN_WIDTH // n
    return pl.pallas_call(
        _gdn_inproj_kernel,
        out_shape=jax.ShapeDtypeStruct((b * s, 4 * GDN_WIDTH), BF16),
        grid=(b, 4 * GDN_WIDTH // n),
        in_specs=[pl.BlockSpec((s, d), lambda i, j: (i, 0)),
                  pl.BlockSpec((d, n), lambda i, j: (0, j)),
                  pl.BlockSpec((GDN_CONV, n), lambda i, j: (0, jnp.minimum(j, n_conv_blocks - 1)))],
        out_specs=pl.BlockSpec((s, n), lambda i, j: (i, j)),
        scratch_shapes=[pltpu.VMEM((s + 2 * GDN_CONV_PAD, n), F32)],
        compiler_params=_cparams("parallel", "parallel"),
        name="gdn_inproj",
    )(xf, w_main_bf16, conv_w)


def _gdn_core_kernel(q_ref, k_ref, v_ref, z_ref, gt_ref, rows_ref, nw_ref, o_ref,
                     u_ref, w_ref, qd_ref, kd_ref, aqk_ref, dend_ref, of_ref, ob_ref):
    s = q_ref.shape[0]
    dk = GDN_HEAD_DIM
    hps = q_ref.shape[1] // dk
    c = GDN_CHUNK
    n_chunks = s // c
    first_head = pl.program_id(1) * hps
    slab = min(s, 256)
    head_cols = [slice(hd * dk, (hd + 1) * dk) for hd in range(hps)]

    lane = lax.broadcasted_iota(I32, (c, 2 * c), 1)
    row = lax.broadcasted_iota(I32, (c, 2 * c), 0)
    is_b = lane >= c
    col = jnp.where(is_b, lane - c, lane)
    ahead = jnp.where(is_b, col - row, row - col)
    incl = ahead >= 0
    strict = ahead > 0
    eye2 = jnp.where(row == col, 1.0, 0.0)
    shifts = [lax.rem(2 * V7X_LANES - first_head - hd, V7X_LANES) for hd in range(hps)]

    group = min(GDN_PREP_GROUP, n_chunks)
    n_groups = n_chunks // group

    def pair_rows(t):
        if isinstance(t, int):
            return pl.ds(t * c, c), pl.ds((n_chunks - 1 - t) * c, c)
        return pl.ds(pl.multiple_of(t * c, c), c), pl.ds(pl.multiple_of((n_chunks - 1 - t) * c, c), c)

    def prep_stages(hd, pg, slot):
        rows = [pair_rows(pg * group + i) for i in range(group)]
        cols = head_cols[hd]
        q_f, k_f, v_f = ([ref[r[0], cols].astype(F32) for r in rows] for ref in (q_ref, k_ref, v_ref))
        q_b, k_b, v_b = ([ref[r[1], cols].astype(F32) for r in rows] for ref in (q_ref, k_ref, v_ref))
        gt_f = [pltpu.roll(gt_ref[r[0], :], shifts[hd], axis=1) for r in rows]
        gt_b = [pltpu.roll(gt_ref[r[1], :], shifts[hd], axis=1) for r in rows]
        g_f = [t[:, 0:1] for t in gt_f]
        b_f = [t[:, 2 * GDN_HEADS:2 * GDN_HEADS + 1] for t in gt_f]
        g_b = [t[:, GDN_HEADS:GDN_HEADS + 1] for t in gt_b]
        b_b = [t[:, 3 * GDN_HEADS:3 * GDN_HEADS + 1] for t in gt_b]
        head = pl.ds(first_head + hd, 1)
        g_row, beta_row, be_row = [], [], []
        for i in range(group):
            t = pg * group + i
            row_f, row_b = rows_ref[t, head, :], rows_ref[n_chunks - 1 - t, head, :]
            pick = lambda j: jnp.where(is_b[0:1], row_b[:, j * 2 * c:(j + 1) * 2 * c], row_f[:, j * 2 * c:(j + 1) * 2 * c])
            g_row.append(pick(0))
            beta_row.append(pick(1))
            be_row.append(pick(2))
        dmat = [jnp.where(is_b, g_b[i], g_f[i]) - g_row[i] for i in range(group)]
        prod_f, prod_b = [], []
        for i in range(group):
            kfb, kbb = k_f[i].astype(BF16), k_b[i].astype(BF16)
            keys = jnp.concatenate([kfb, kbb], axis=0)
            prod_f.append(_dot_nt(jnp.concatenate([q_f[i].astype(BF16), kfb], axis=0), keys))
            yield
            prod_b.append(_dot_nt(jnp.concatenate([q_b[i].astype(BF16), kbb], axis=0), keys))
            yield
        qk2 = [jnp.where(is_b, prod_b[i][:c], prod_f[i][:c]) for i in range(group)]
        kk2 = [jnp.where(is_b, prod_b[i][c:], prod_f[i][c:]) for i in range(group)]
        gamma = [jnp.exp(jnp.where(incl, d_i, NEG_BIG)) for d_i in dmat]
        m = [jnp.where(strict, kk2[i] * gamma[i], 0.0) * jnp.where(is_b, b_b[i], b_f[i]) for i in range(group)]
        x_inv = [eye2 - m_i for m_i in m]
        p = []
        for m_i in m:
            p.append(_dot(m_i.astype(BF16), _block_diag2(m_i, is_b).astype(BF16)))
            yield
        for _ in range(5):
            y = []
            for i in range(group):
                y.append(_dot(jnp.concatenate([x_inv[i], p[i]], axis=0).astype(BF16),
                              _block_diag2(p[i], is_b).astype(BF16)))
                yield
            x_inv = [x_inv[i] + y[i][:c] for i in range(group)]
            p = [y_i[c:] for y_i in y]
        eg_f = [jnp.exp(g) for g in g_f]
        eg_b = [jnp.exp(g) for g in g_b]
        u_sol, w_sol = [], []
        for i in range(group):
            u_sol.append(_dot(_block_diag2(x_inv[i] * beta_row[i], is_b).astype(BF16),
                              jnp.concatenate([v_f[i], v_b[i]], axis=0).astype(BF16)))
            yield
            w_sol.append(_dot(_block_diag2(x_inv[i] * be_row[i], is_b).astype(BF16),
                              jnp.concatenate([k_f[i], k_b[i]], axis=0).astype(BF16)))
            yield
        for i in range(group):
            local = pl.ds(i * c, c)
            gl_f, gl_b = g_f[i][c - 1:c, :], g_b[i][0:1, :]
            for d, (qc, kc, eg, gl, g_d) in enumerate(((q_f[i], k_f[i], eg_f[i], gl_f, g_f[i]),
                                                       (q_b[i], k_b[i], eg_b[i], gl_b, g_b[i]))):
                u_ref[hd, slot, d, local, :] = u_sol[i][d * c:(d + 1) * c]
                w_ref[hd, slot, d, local, :] = w_sol[i][d * c:(d + 1) * c]
                qd_ref[hd, slot, d, local, :] = qc * eg
                kd_ref[hd, slot, d, local, :] = kc * jnp.exp(gl - g_d)
            aqk_ref[hd, slot, local, :] = jnp.where(incl, qk2[i] * gamma[i], 0.0)
            dend_ref[hd, slot, pl.ds(i * V7X_SUBLANES, V7X_SUBLANES), :] = jnp.where(
                is_b[:V7X_SUBLANES], jnp.exp(gl_b), jnp.exp(gl_f))

    lanes = (slice(0, c), slice(c, 2 * c))
    chains = [(hd, d) for hd in range(hps) for d in range(2)]

    def run_group(pg, slot, prep_gens, states):
        per_gap = -(-GDN_PREP_MATMULS // 2)

        def advance():
            for _ in range(per_gap):
                for gen in prep_gens:
                    next(gen, None)

        for i in range(group):
            rows = pair_rows(pg * group + i)
            local = pl.ds(i * c, c)
            stb = [st.astype(BF16) for st in states]
            ws = [_dot(w_ref[hd, slot, d, local, :].astype(BF16), stb[n]) for n, (hd, d) in enumerate(chains)]
            qs = [_dot(qd_ref[hd, slot, d, local, :].astype(BF16), stb[n]) for n, (hd, d) in enumerate(chains)]
            advance()
            vnb = [(u_ref[hd, slot, d, local, :] - ws[n]).astype(BF16) for n, (hd, d) in enumerate(chains)]
            av = [_dot(aqk_ref[hd, slot, local, :][:, lanes[d]].astype(BF16), vnb[n])
                  for n, (hd, d) in enumerate(chains)]
            kv = [_dot_tn(kd_ref[hd, slot, d, local, :].astype(BF16), vnb[n]) for n, (hd, d) in enumerate(chains)]
            advance()
            new_states = []
            for n, (hd, d) in enumerate(chains):
                (of_ref, ob_ref)[d][hd, rows[d], :] = qs[n] + av[n]
                dend = dend_ref[hd, slot, pl.ds(i * V7X_SUBLANES, 1), :][:, d * c:d * c + 1]
                new_states.append(states[n] * dend + kv[n])
            states = tuple(new_states)
        for gen in prep_gens:
            for _ in gen:
                pass
        return states

    for hd in range(hps):
        for _ in prep_stages(hd, 0, 0):
            pass

    def fused(tg, states):
        slot = lax.rem(tg, 2)
        return run_group(tg, slot, [prep_stages(hd, tg + 1, 1 - slot) for hd in range(hps)], states)

    zero_state = jnp.zeros((dk, dk), F32)
    states = lax.fori_loop(0, n_groups - 1, fused, (zero_state,) * len(chains))
    run_group(n_groups - 1, (n_groups - 1) % 2, [], states)

    for hd in range(hps):
        cols = slice(hd * dk, (hd + 1) * dk)
        for r0 in range(0, s, slab):
            rs = pl.ds(r0, slab)
            o = of_ref[hd, rs, :] + ob_ref[hd, rs, :]
            o = o * lax.rsqrt(jnp.mean(o * o, axis=-1, keepdims=True) + 1e-6) * nw_ref[...]
            o_ref[rs, cols] = (o * _silu(z_ref[rs, cols].astype(F32))).astype(o_ref.dtype)


def _gdn_core(proj, gates, gate_rows, norm_w, b, s):
    dk = GDN_HEAD_DIM
    hps = GDN_HEADS_PER_STEP
    nhb = GDN_HEADS // hps
    blk = lambda kind: pl.BlockSpec((s, hps * dk), lambda i, h: (i, kind * nhb + h))
    group_rows = min(GDN_PREP_GROUP, s // GDN_CHUNK) * GDN_CHUNK
    seq = lambda: pltpu.VMEM((hps, s, dk), F32)
    slots = lambda: pltpu.VMEM((hps, 2, 2, group_rows, dk), F32)
    return pl.pallas_call(
        _gdn_core_kernel,
        out_shape=jax.ShapeDtypeStruct((b * s, GDN_WIDTH), BF16),
        grid=(b, nhb),
        in_specs=[blk(0), blk(1), blk(2), blk(3),
                  pl.BlockSpec((s, GDN_GATE_LANES), lambda i, h: (i, 0)),
                  pl.BlockSpec((s // GDN_CHUNK, GDN_HEADS, 3 * GDN_GATE_LANES), lambda i, h: (i, 0, 0)),
                  pl.BlockSpec((1, dk), lambda i, h: (0, 0))],
        out_specs=pl.BlockSpec((s, hps * dk), lambda i, h: (i, h)),
        scratch_shapes=[slots(), slots(), slots(), slots(),
                        pltpu.VMEM((hps, 2, group_rows, 2 * GDN_CHUNK), F32),
                        pltpu.VMEM((hps, 2, group_rows // GDN_CHUNK * V7X_SUBLANES, 2 * GDN_CHUNK), F32),
                        seq(), seq()],
        compiler_params=_cparams("parallel", "parallel"),
        name="gdn_core",
    )(proj, proj, proj, proj, gates, gate_rows, norm_w.reshape(1, dk))


def _gdn_mixer_core(xf, w_in, conv_w, a_log, dt_bias, norm_w, b, s):
    n_main = 4 * GDN_WIDTH
    n_gate = 4 * GDN_HEADS
    proj = _gdn_inproj(xf, w_in[:, :n_main].astype(BF16), conv_w.astype(F32), b, s)
    w_gate = jnp.pad(w_in[:, n_main:], ((0, 0), (0, GDN_GATE_LANES - n_gate))).astype(BF16)
    lane_pad = lambda v: jnp.pad(v.astype(F32).reshape(1, 2 * GDN_HEADS), ((0, 0), (0, GDN_GATE_LANES - 2 * GDN_HEADS)))
    gates, gate_rows = _gdn_gates(xf, w_gate, lane_pad(a_log), lane_pad(dt_bias), b, s)
    return _gdn_core(proj, gates, gate_rows, norm_w.astype(F32), b, s)


def kernel(x, a_w_in, a_conv, a_A_log, a_dt_bias, a_norm_w, a_w_out, b_w_in, b_sink, b_w_out, rel_bias, c_w_in, c_w_group, c_scale, c_w_out, router_w, router_b, moe_w_gate, moe_w_up, moe_w_down, ln_g, ln_b):
    b, s, d = x.shape
    t = b * s
    xf = x.reshape(t, d)
    xb = xf.astype(BF16)
    router_wt = router_w.T.astype(BF16)
    router_b_col = router_b.reshape(N_EXPERTS, 1).astype(F32)
    bias_tbl = _attn_bias_table(rel_bias)
    for i in range(DEPTH):
        kind, j = i % N_MIXERS, i // N_MIXERS
        if kind == 0:
            h = _gdn_mixer_core(xb, a_w_in[j], a_conv[j], a_A_log[j], a_dt_bias[j], a_norm_w[j], b, s)
            w_out = a_w_out[j]
        elif kind == 1:
            proj = _linear(xb, b_w_in[j].astype(BF16), BF16)
            h = _attn_core(proj, bias_tbl, b_sink[j], b, s)
            w_out = b_w_out[j]
        else:
            u = _linear(xb, c_w_in[j].astype(BF16), F32)
            h = _pool_core(u.reshape(b, s, d), c_w_group[j].astype(BF16), c_scale[j]).reshape(t, d)
            w_out = c_w_out[j]
        x1w, route_i, counts = _post_mixer(h, w_out.astype(BF16), xf, ln_g[i, 0], ln_b[i, 0], router_wt, router_b_col)
        xf, xb = _moe_layer(x1w, route_i, counts[:, 0], moe_w_gate[i], moe_w_up[i], moe_w_down[i],
                            ln_g[i, 1], ln_b[i, 1])
    return xf.reshape(b, s, d)
```

```python
import functools
import math

import jax
import jax.numpy as jnp
from jax import lax
from jax.experimental import pallas as pl
from jax.experimental.pallas import tpu as pltpu

F32 = jnp.float32
BF16 = jnp.bfloat16
I32 = jnp.int32

D_MODEL = 1024
DEPTH = 4
N_MIXERS = 3
DEEPNORM_ALPHA = (2.0 * DEPTH) ** 0.25
LN_EPS = 1e-5

GDN_HEADS = 8
GDN_HEAD_DIM = 128
GDN_WIDTH = GDN_HEADS * GDN_HEAD_DIM
GDN_CONV = 5
GDN_CHUNK = 64

ATT_HEADS = 16
ATT_KV_HEADS = 4
ATT_HEAD_DIM = 64
ATT_GROUP = ATT_HEADS // ATT_KV_HEADS
ATT_WINDOW = 128
ATT_QBLOCK = 128
ATT_KSPAN = ATT_QBLOCK + 2 * ATT_WINDOW
REL_BUCKETS = 32
REL_MAX_DIST = 128

POOL_WINDOWS = (2, 4, 8, 16)
POOL_GROUP_DIM = D_MODEL // len(POOL_WINDOWS)

N_EXPERTS = 16
N_EXPERT_GROUPS = 4
EXPERTS_PER_GROUP = N_EXPERTS // N_EXPERT_GROUPS
TOP_K = 2
EXPERT_DIM = 512
PAIRS_PER_GROUP = EXPERTS_PER_GROUP * (EXPERTS_PER_GROUP - 1) // 2
N_SEGMENTS = N_EXPERT_GROUPS * PAIRS_PER_GROUP

V7X_LANES = 128
V7X_SUBLANES = 8
V7X_VMEM_LIMIT_BYTES = 56 * 1024 * 1024

ROW_TILE = 512
MOE_BLOCK = 256
ROW_EXTRA = 128
NEG_BIG = -1e30

HIGHEST = lax.Precision.HIGHEST


def _cparams(*sem):
    return pltpu.CompilerParams(dimension_semantics=tuple(sem), vmem_limit_bytes=V7X_VMEM_LIMIT_BYTES)


def _dot(a, b, **kw):
    return jnp.dot(a, b, preferred_element_type=F32, **kw)


def _dot_nt(a, b, **kw):
    return lax.dot_general(a, b, (((1,), (1,)), ((), ())), preferred_element_type=F32, **kw)


def _dot_tn(a, b, **kw):
    return lax.dot_general(a, b, (((0,), (0,)), ((), ())), preferred_element_type=F32, **kw)


def _silu(x):
    return x * jax.nn.sigmoid(x)


def _linear_kernel(x_ref, w_ref, o_ref):
    o_ref[...] = _dot(x_ref[...].astype(BF16), w_ref[...]).astype(o_ref.dtype)


def _linear(x, w_bf16, out_dtype, tm=ROW_TILE):
    m, k = x.shape
    n = w_bf16.shape[1]
    return pl.pallas_call(
        _linear_kernel,
        out_shape=jax.ShapeDtypeStruct((m, n), out_dtype),
        grid=(m // tm,),
        in_specs=[pl.BlockSpec((tm, k), lambda i: (i, 0)), pl.BlockSpec((k, n), lambda i: (0, 0))],
        out_specs=pl.BlockSpec((tm, n), lambda i: (i, 0)),
        compiler_params=_cparams("parallel"),
        name="linear",
    )(x, w_bf16)


def _layer_norm_rows(s, g, b):
    mu = jnp.mean(s, axis=-1, keepdims=True)
    xc = s - mu
    var = jnp.mean(xc * xc, axis=-1, keepdims=True)
    return xc * lax.rsqrt(var + LN_EPS) * g + b


def _top2_of4(a, b, c, d):
    hi1, lo1 = jnp.maximum(a, b), jnp.minimum(a, b)
    hi2, lo2 = jnp.maximum(c, d), jnp.minimum(c, d)
    return jnp.maximum(hi1, hi2) + jnp.maximum(jnp.minimum(hi1, hi2), jnp.maximum(lo1, lo2))


def _route_rows(scores, biased):
    bi = [biased[e:e + 1, :] for e in range(N_EXPERTS)]
    sc = [scores[e:e + 1, :] for e in range(N_EXPERTS)]
    gs = [_top2_of4(*bi[4 * g:4 * g + 4]) for g in range(N_EXPERT_GROUPS)]
    group = jnp.zeros_like(gs[0], dtype=I32)
    best = gs[0]
    for g in range(1, N_EXPERT_GROUPS):
        upd = gs[g] > best
        group = jnp.where(upd, g, group)
        best = jnp.where(upd, gs[g], best)

    def pick(rows, j):
        out = rows[j]
        for g in range(1, N_EXPERT_GROUPS):
            out = jnp.where(group == g, rows[4 * g + j], out)
        return out

    v = [pick(bi, j) for j in range(EXPERTS_PER_GROUP)]
    s = [pick(sc, j) for j in range(EXPERTS_PER_GROUP)]
    i1 = jnp.zeros_like(group)
    b1 = v[0]
    for j in range(1, EXPERTS_PER_GROUP):
        upd = v[j] > b1
        i1 = jnp.where(upd, j, i1)
        b1 = jnp.where(upd, v[j], b1)
    i2 = jnp.full_like(group, -1)
    b2 = jnp.full_like(b1, -jnp.inf)
    for j in range(EXPERTS_PER_GROUP):
        upd = (i1 != j) & ((v[j] > b2) | (i2 < 0))
        i2 = jnp.where(upd, j, i2)
        b2 = jnp.where(upd, v[j], b2)
    s1 = s[0]
    s2 = s[0]
    for j in range(1, EXPERTS_PER_GROUP):
        s1 = jnp.where(i1 == j, s[j], s1)
        s2 = jnp.where(i2 == j, s[j], s2)
    den = s1 + s2
    first_is_lo = i1 < i2
    lo = jnp.where(first_is_lo, i1, i2)
    hi = jnp.where(first_is_lo, i2, i1)
    pair = lax.shift_right_logical(lo * (7 - lo), 1) + hi - lo - 1
    g1, g2 = s1 / den, s2 / den
    return group * PAIRS_PER_GROUP + pair, jnp.where(first_is_lo, g1, g2), jnp.where(first_is_lo, g2, g1)


def _post_mixer_kernel(h_ref, w_ref, x_ref, g_ref, b_ref, rwt_ref, rb_ref, ustrict_ref, eye_ref,
                       x1w_ref, ri_ref, cnt_ref, base_ref):
    tm, d = x_ref.shape

    @pl.when(pl.program_id(0) == 0)
    def _():
        base_ref[...] = jnp.zeros_like(base_ref)

    y = _dot(h_ref[...], w_ref[...])
    x1 = _layer_norm_rows(DEEPNORM_ALPHA * x_ref[...] + y, g_ref[...], b_ref[...])
    x1w_ref[:, 0:d] = x1

    logits = _dot_nt(rwt_ref[...], x1.astype(BF16))
    scores = jax.nn.sigmoid(logits)
    seg, g_lo, g_hi = _route_rows(scores, scores + rb_ref[...])

    sidx = lax.broadcasted_iota(I32, (N_SEGMENTS, tm), 0)
    hit = sidx == seg
    onehot = jnp.where(hit, 1.0, 0.0)
    base = base_ref[:, 0:1]
    before = _dot(onehot.astype(BF16), ustrict_ref[...]) + base
    rank = jnp.sum(jnp.where(hit, before, 0.0), axis=0, keepdims=True)
    new_base = base + jnp.sum(onehot, axis=1, keepdims=True)
    base_ref[...] = jnp.broadcast_to(new_base, base_ref.shape)
    cnt_ref[...] = jnp.broadcast_to(new_base, cnt_ref.shape)

    ri_ref[...] = jnp.concatenate([seg, rank.astype(I32), jnp.zeros((6, tm), I32)], axis=0)
    gates = jnp.concatenate([g_lo, g_hi, jnp.zeros((ROW_EXTRA - 2, tm), F32)], axis=0)
    hi = gates.astype(BF16)
    lo = (gates - hi.astype(F32)).astype(BF16)
    x1w_ref[:, d:d + ROW_EXTRA] = _dot_nt(eye_ref[...], hi) + _dot_nt(eye_ref[...], lo)


def _post_mixer(h_bf16, w_out_bf16, x, ln_g, ln_b, router_wt, router_b_col, tm=ROW_TILE):
    t, d = x.shape
    ustrict = (jnp.arange(tm)[:, None] < jnp.arange(tm)[None, :]).astype(BF16)
    eye = jnp.eye(tm, dtype=BF16)
    full = lambda shape: pl.BlockSpec(shape, lambda i: (0,) * len(shape))
    return pl.pallas_call(
        _post_mixer_kernel,
        out_shape=(jax.ShapeDtypeStruct((t, d + ROW_EXTRA), F32),
                   jax.ShapeDtypeStruct((8, t), I32),
                   jax.ShapeDtypeStruct((N_SEGMENTS, V7X_LANES), F32)),
        grid=(t // tm,),
        in_specs=[pl.BlockSpec((tm, h_bf16.shape[1]), lambda i: (i, 0)),
                  full(w_out_bf16.shape),
                  pl.BlockSpec((tm, d), lambda i: (i, 0)),
                  full((1, d)), full((1, d)),
                  full((N_EXPERTS, d)), full((N_EXPERTS, 1)),
                  full((tm, tm)), full((tm, tm))],
        out_specs=(pl.BlockSpec((tm, d + ROW_EXTRA), lambda i: (i, 0)),
                   pl.BlockSpec((8, tm), lambda i: (0, i)),
                   pl.BlockSpec((N_SEGMENTS, V7X_LANES), lambda i: (0, 0))),
        scratch_shapes=[pltpu.VMEM((N_SEGMENTS, V7X_LANES), F32)],
        compiler_params=_cparams("arbitrary"),
        name="post_mixer_router",
    )(h_bf16, w_out_bf16, x, ln_g.reshape(1, d), ln_b.reshape(1, d), router_wt, router_b_col, ustrict, eye)


def _row_copy(src_ref, src_row, dst_ref, dst_row, sem):
    return pltpu.make_async_copy(src_ref.at[pl.ds(src_row, 1), :], dst_ref.at[pl.ds(dst_row, 1), :], sem)


def _dispatch_kernel(n_tiles, pad_lo_ref, pad_hi_ref, x_ref, dest_hbm, xs_hbm,
                     idx_smem, zero_ref, idx_sem, row_sem, pad_sem):
    tm = x_ref.shape[0]
    i = pl.program_id(0)
    slot = lax.rem(i, 2)

    def idx_copy(tile, sl):
        return pltpu.make_async_copy(dest_hbm.at[tile], idx_smem.at[sl], idx_sem.at[sl])

    @pl.when(i == 0)
    def _():
        idx_copy(0, 0).start()

    idx_copy(i, slot).wait()

    @pl.when(i + 1 < n_tiles)
    def _():
        idx_copy(i + 1, 1 - slot).start()

    def issue(j, carry):
        _row_copy(x_ref, j, xs_hbm, idx_smem[slot, 0, j], row_sem).start()
        return carry

    lax.fori_loop(0, tm, issue, 0, unroll=8)

    def drain(j, carry):
        _row_copy(x_ref, j, xs_hbm, idx_smem[slot, 0, j], row_sem).wait()
        return carry

    lax.fori_loop(0, tm, drain, 0, unroll=8)

    @pl.when(i == pl.num_programs(0) - 1)
    def _():
        zero_ref[...] = jnp.zeros_like(zero_ref)
        for e in range(N_SEGMENTS + 1):
            def fill(r, carry):
                _row_copy(zero_ref, 0, xs_hbm, r, pad_sem).start()
                return carry

            lax.fori_loop(pad_lo_ref[e], pad_hi_ref[e], fill, 0)
        for e in range(N_SEGMENTS + 1):
            def filled(r, carry):
                _row_copy(zero_ref, 0, xs_hbm, r, pad_sem).wait()
                return carry

            lax.fori_loop(pad_lo_ref[e], pad_hi_ref[e], filled, 0)


def _dispatch(x1w, dest_tiles, pad_lo, pad_hi, n_rows, tm=ROW_TILE):
    t, d = x1w.shape
    grid_spec = pltpu.PrefetchScalarGridSpec(
        num_scalar_prefetch=2,
        grid=(t // tm,),
        in_specs=[pl.BlockSpec((tm, d), lambda i, lo, hi: (i, 0)),
                  pl.BlockSpec(memory_space=pl.ANY)],
        out_specs=pl.BlockSpec(memory_space=pl.ANY),
        scratch_shapes=[pltpu.SMEM((2, 1, tm), I32), pltpu.VMEM((V7X_SUBLANES, d), F32),
                        pltpu.SemaphoreType.DMA((2,)), pltpu.SemaphoreType.DMA, pltpu.SemaphoreType.DMA],
    )
    return pl.pallas_call(
        functools.partial(_dispatch_kernel, t // tm),
        out_shape=jax.ShapeDtypeStruct((n_rows, d), F32),
        grid_spec=grid_spec,
        compiler_params=_cparams("arbitrary"),
        name="moe_dispatch",
    )(pad_lo, pad_hi, x1w, dest_tiles)


def _expert_kernel(ea_ref, eb_ref, xs_ref, wga_ref, wua_ref, wda_ref, wgb_ref, wub_ref, wdb_ref, ys_ref):
    del ea_ref, eb_ref
    d = ys_ref.shape[1]
    x = xs_ref[:, 0:d].astype(BF16)
    extra = xs_ref[:, d:d + ROW_EXTRA]
    y = None
    for col, (wg_ref, wu_ref, wd_ref) in enumerate(((wga_ref, wua_ref, wda_ref), (wgb_ref, wub_ref, wdb_ref))):
        hidden = _silu(_dot(x, wg_ref[0, 0].astype(BF16))) * _dot(x, wu_ref[0, 0].astype(BF16))
        part = _dot((hidden * extra[:, col:col + 1]).astype(BF16), wd_ref[0, 0].astype(BF16))
        y = part if y is None else y + part
    ys_ref[...] = y


def _experts(block_ea, block_eb, xs, wg, wu, wd, layer):
    n_rows = xs.shape[0]
    d = xs.shape[1] - ROW_EXTRA
    n_blocks = n_rows // MOE_BLOCK
    f = wg.shape[3]
    up = lambda which: pl.BlockSpec((1, 1, d, f), lambda i, ea, eb: (layer, (ea, eb)[which][i], 0, 0))
    down = lambda which: pl.BlockSpec((1, 1, f, d), lambda i, ea, eb: (layer, (ea, eb)[which][i], 0, 0))
    grid_spec = pltpu.PrefetchScalarGridSpec(
        num_scalar_prefetch=2,
        grid=(n_blocks,),
        in_specs=[pl.BlockSpec((MOE_BLOCK, d + ROW_EXTRA), lambda i, ea, eb: (i, 0)),
                  up(0), up(0), down(0), up(1), up(1), down(1)],
        out_specs=pl.BlockSpec((MOE_BLOCK, d), lambda i, ea, eb: (i, 0)),
    )
    return pl.pallas_call(
        _expert_kernel,
        out_shape=jax.ShapeDtypeStruct((n_rows, d), F32),
        grid_spec=grid_spec,
        compiler_params=_cparams("parallel"),
        name="moe_experts",
    )(block_ea, block_eb, xs, wg, wu, wd, wg, wu, wd)


def _combine_kernel(n_tiles, x1w_ref, g_ref, b_ref, dest_hbm, ys_hbm, o_ref, ob_ref,
                    idx_smem, ybuf, idx_sem, row_sem):
    tm, d = o_ref.shape
    i = pl.program_id(0)
    slot = lax.rem(i, 2)
    other = 1 - slot

    def idx_copy(tile, sl):
        return pltpu.make_async_copy(dest_hbm.at[tile], idx_smem.at[sl], idx_sem.at[sl])

    def gather(sl, start):
        def body(j, carry):
            cp = _row_copy(ys_hbm, idx_smem[sl, 0, j], ybuf.at[sl], j, row_sem.at[sl])
            if start:
                cp.start()
            else:
                cp.wait()
            return carry

        lax.fori_loop(0, tm, body, 0, unroll=8)

    @pl.when(i == 0)
    def _():
        first = idx_copy(0, 0)
        first.start()
        first.wait()
        gather(0, True)
        if n_tiles > 1:
            idx_copy(1, 1).start()

    @pl.when(i + 1 < n_tiles)
    def _():
        idx_copy(i + 1, other).wait()
        gather(other, True)

    gather(slot, False)

    @pl.when(i + 2 < n_tiles)
    def _():
        idx_copy(i + 2, slot).start()

    x2 = _layer_norm_rows(DEEPNORM_ALPHA * x1w_ref[:, 0:d] + ybuf[slot], g_ref[...], b_ref[...])
    o_ref[...] = x2
    ob_ref[...] = x2.astype(ob_ref.dtype)


def _combine(x1w, ln_g, ln_b, dest_tiles, ys, tm=ROW_TILE):
    t = x1w.shape[0]
    d = ys.shape[1]
    full = lambda shape: pl.BlockSpec(shape, lambda i: (0,) * len(shape))
    return pl.pallas_call(
        functools.partial(_combine_kernel, t // tm),
        out_shape=(jax.ShapeDtypeStruct((t, d), F32), jax.ShapeDtypeStruct((t, d), BF16)),
        grid=(t // tm,),
        in_specs=[pl.BlockSpec((tm, d + ROW_EXTRA), lambda i: (i, 0)),
                  full((1, d)), full((1, d)),
                  pl.BlockSpec(memory_space=pl.ANY),
                  pl.BlockSpec(memory_space=pl.ANY)],
        out_specs=(pl.BlockSpec((tm, d), lambda i: (i, 0)), pl.BlockSpec((tm, d), lambda i: (i, 0))),
        scratch_shapes=[pltpu.SMEM((2, 1, tm), I32), pltpu.VMEM((2, tm, d), F32),
                        pltpu.SemaphoreType.DMA((2,)), pltpu.SemaphoreType.DMA((2,))],
        compiler_params=_cparams("arbitrary"),
        name="moe_combine_ln",
    )(x1w, ln_g.reshape(1, d), ln_b.reshape(1, d), dest_tiles, ys)


def _moe_layer(x1w, route_i, counts, wg, wu, wd, layer, ln_g, ln_b, tm=ROW_TILE):
    t = x1w.shape[0]
    n_tiles = t // tm
    n_rows = t + N_SEGMENTS * MOE_BLOCK
    counts = counts.astype(I32)
    padded = (counts + MOE_BLOCK - 1) // MOE_BLOCK * MOE_BLOCK
    pends = jnp.cumsum(padded)
    pstarts = pends - padded
    seg = route_i[0:1]
    start_of = jnp.sum(jnp.where(seg[None] == jnp.arange(N_SEGMENTS, dtype=I32)[:, None, None],
                                 pstarts[:, None, None], 0), axis=0)
    dest = start_of + route_i[1:2]
    dest_tiles = dest.reshape(1, n_tiles, tm).transpose(1, 0, 2)
    n_blocks = n_rows // MOE_BLOCK
    block_seg = jnp.minimum(
        jnp.sum((jnp.arange(n_blocks, dtype=I32) * MOE_BLOCK)[:, None] >= pends[None, :], axis=-1),
        N_SEGMENTS - 1).astype(I32)
    pair_lo = jnp.array([lo for lo in range(EXPERTS_PER_GROUP) for hi in range(lo + 1, EXPERTS_PER_GROUP)], I32)
    pair_hi = jnp.array([hi for lo in range(EXPERTS_PER_GROUP) for hi in range(lo + 1, EXPERTS_PER_GROUP)], I32)
    block_group, block_pair = block_seg // PAIRS_PER_GROUP, block_seg % PAIRS_PER_GROUP
    block_ea = block_group * EXPERTS_PER_GROUP + pair_lo[block_pair]
    block_eb = block_group * EXPERTS_PER_GROUP + pair_hi[block_pair]
    pad_lo = jnp.concatenate([pstarts + counts, pends[-1:]]).astype(I32)
    pad_hi = jnp.concatenate([pends, jnp.full((1,), n_rows, I32)]).astype(I32)
    xs = _dispatch(x1w, dest_tiles, pad_lo, pad_hi, n_rows, tm)
    ys = _experts(block_ea, block_eb, xs, wg, wu, wd, layer)
    return _combine(x1w, ln_g, ln_b, dest_tiles, ys, tm)


POOL_PAD = 16


def _pool_kernel(u_ref, wg_ref, scale_ref, o_ref, pad_ref):
    s, c = u_ref.shape[1], u_ref.shape[2]
    u = u_ref[0]
    zeros = jnp.zeros((POOL_PAD, c), F32)
    pad_ref[pl.ds(0, POOL_PAD), :] = zeros
    pad_ref[pl.ds(POOL_PAD + s, POOL_PAD), :] = zeros
    pad_ref[pl.ds(POOL_PAD, s), :] = u
    pos = lax.broadcasted_iota(I32, (s, c), 0)
    group = pl.program_id(1)
    for gi, win in enumerate(POOL_WINDOWS):
        @pl.when(group == gi)
        def _(win=win):
            half = win // 2
            total = pad_ref[pl.ds(POOL_PAD - half, s), :]
            for j in range(1 - half, half):
                total = total + pad_ref[pl.ds(POOL_PAD + j, s), :]
            count = (jnp.minimum(pos + half, s) - jnp.maximum(pos - half, 0)).astype(F32)
            mixed = total / count - u
            y = _dot(mixed.astype(BF16), wg_ref[0]) * scale_ref[...]
            o_ref[0] = y.astype(o_ref.dtype)


def _pool_core(u, w_group_bf16, scale):
    b, s, d = u.shape
    c = POOL_GROUP_DIM
    return pl.pallas_call(
        _pool_kernel,
        out_shape=jax.ShapeDtypeStruct((b, s, d), BF16),
        grid=(b, d // c),
        in_specs=[pl.BlockSpec((1, s, c), lambda i, g: (i, 0, g)),
                  pl.BlockSpec((1, c, c), lambda i, g: (g, 0, 0)),
                  pl.BlockSpec((1, c), lambda i, g: (0, g))],
        out_specs=pl.BlockSpec((1, s, c), lambda i, g: (i, 0, g)),
        scratch_shapes=[pltpu.VMEM((s + 2 * POOL_PAD, c), F32)],
        compiler_params=_cparams("parallel", "parallel"),
        name="pool_core",
    )(u, w_group_bf16, scale.reshape(1, d))


def _t5_bucket(rel):
    half = REL_BUCKETS // 2
    max_exact = half // 2
    n = jnp.abs(rel)
    large = max_exact + (jnp.log(jnp.maximum(n, 1).astype(F32) / max_exact)
                         / math.log(REL_MAX_DIST / max_exact) * (half - max_exact)).astype(I32)
    large = jnp.minimum(large, half - 1)
    return (rel > 0).astype(I32) * half + jnp.where(n < max_exact, n, large)


def _attn_bias_table(rel_bias):
    rel = jnp.arange(ATT_KSPAN)[None, :] - ATT_WINDOW - jnp.arange(ATT_QBLOCK)[:, None]
    onehot = (_t5_bucket(rel)[..., None] == jnp.arange(REL_BUCKETS)).astype(F32)
    bias = jnp.einsum("qkb,bh->hqk", onehot, rel_bias.astype(F32), precision=HIGHEST)
    return jnp.where((jnp.abs(rel) <= ATT_WINDOW)[None], bias, NEG_BIG)


def _attn_kernel(q_ref, kp_ref, kc_ref, kn_ref, vp_ref, vc_ref, vn_ref, bias_ref, sink_ref, o_ref):
    j = pl.program_id(1)
    nb = pl.num_programs(1)
    qb = ATT_QBLOCK
    col = lax.broadcasted_iota(I32, (qb, ATT_KSPAN), 1)
    edge = jnp.where(((j == 0) & (col < qb)) | ((j == nb - 1) & (col >= 2 * qb)), NEG_BIG, 0.0)
    k_all = jnp.concatenate([kp_ref[...], kc_ref[...], kn_ref[...]], axis=0)
    v_all = jnp.concatenate([vp_ref[...], vc_ref[...], vn_ref[...]], axis=0)
    q = q_ref[...] * (ATT_HEAD_DIM ** -0.5)
    heads = lambda g: range(g * ATT_GROUP, (g + 1) * ATT_GROUP)
    cols = lambda i: slice(i * ATT_HEAD_DIM, (i + 1) * ATT_HEAD_DIM)

    def scores(g):
        return [_dot_nt(q[:, cols(hd)], k_all[:, cols(g)]) for hd in heads(g)]

    def attend(g, raw):
        outs = []
        probs, dens = [], []
        for hd, qk in zip(heads(g), raw):
            logits = qk + bias_ref[hd] + edge
            sink = sink_ref[hd]
            mx = jnp.maximum(jnp.max(logits, axis=-1, keepdims=True), sink)
            p = jnp.exp(logits - mx)
            dens.append(jnp.sum(p, axis=-1, keepdims=True) + jnp.exp(sink - mx))
            probs.append(p.astype(BF16))
        for p, den in zip(probs, dens):
            outs.append(_dot(p, v_all[:, cols(g)]) / den)
        return outs

    outs = []
    raw = scores(0)
    for g in range(ATT_KV_HEADS):
        nxt = scores(g + 1) if g + 1 < ATT_KV_HEADS else None
        outs += attend(g, raw)
        raw = nxt
    o_ref[...] = jnp.concatenate(outs, axis=-1).astype(o_ref.dtype)


def _attn_core(proj, bias_tbl, sink, b, s):
    qw = ATT_HEADS * ATT_HEAD_DIM
    kw = ATT_KV_HEADS * ATT_HEAD_DIM
    nb = s // ATT_QBLOCK
    kcol, vcol = qw // kw, qw // kw + 1
    row = lambda i, j: i * nb + j
    prev = lambda i, j: i * nb + jnp.maximum(j - 1, 0)
    nxt = lambda i, j: i * nb + jnp.minimum(j + 1, nb - 1)
    kv_spec = lambda rowf, c: pl.BlockSpec((ATT_QBLOCK, kw), lambda i, j: (rowf(i, j), c))
    return pl.pallas_call(
        _attn_kernel,
        out_shape=jax.ShapeDtypeStruct((b * s, qw), BF16),
        grid=(b, nb),
        in_specs=[pl.BlockSpec((ATT_QBLOCK, qw), lambda i, j: (row(i, j), 0)),
                  kv_spec(prev, kcol), kv_spec(row, kcol), kv_spec(nxt, kcol),
                  kv_spec(prev, vcol), kv_spec(row, vcol), kv_spec(nxt, vcol),
                  pl.BlockSpec((ATT_HEADS, ATT_QBLOCK, ATT_KSPAN), lambda i, j: (0, 0, 0)),
                  pl.BlockSpec(memory_space=pltpu.SMEM)],
        out_specs=pl.BlockSpec((ATT_QBLOCK, qw), lambda i, j: (row(i, j), 0)),
        compiler_params=_cparams("parallel", "parallel"),
        name="attn_core",
    )(proj, proj, proj, proj, proj, proj, proj, bias_tbl, sink.astype(F32))


GDN_GATE_LANES = V7X_LANES
GDN_CONV_PAD = 8
GDN_PREP_GROUP = 8
GDN_INPROJ_COLS = 512
GDN_INPROJ_SLAB = 512
GDN_HEADS_PER_STEP = 2
GDN_PREP_MATMULS = 10


def _softplus(x):
    return jnp.maximum(x, 0.0) + jnp.log(1.0 + jnp.exp(-jnp.abs(x)))


def _gdn_gates_kernel(x_ref, w_ref, alog_ref, dtb_ref, o_ref, rows_ref):
    s = x_ref.shape[0]
    c = GDN_CHUNK
    gl = _dot(x_ref[...].astype(BF16), w_ref[...])
    decay = -jnp.exp(alog_ref[...]) * _softplus(gl + dtb_ref[...])
    beta = jax.nn.sigmoid(gl)
    r = lax.broadcasted_iota(I32, (c, c), 0)
    q = lax.broadcasted_iota(I32, (c, c), 1)
    lower = jnp.where(r >= q, 1.0, 0.0)
    upper = jnp.where(r <= q, 1.0, 0.0)
    lane = lax.broadcasted_iota(I32, (c, GDN_GATE_LANES), 1)
    hh = GDN_HEADS
    eye = jnp.where(lax.broadcasted_iota(I32, (4 * hh, GDN_GATE_LANES), 0)
                    == lax.broadcasted_iota(I32, (4 * hh, GDN_GATE_LANES), 1), 1.0, 0.0)
    for n in range(s // c):
        d_c = decay[n * c:(n + 1) * c]
        pre = _dot(lower, d_c, precision=HIGHEST)
        suf = _dot(upper, d_c, precision=HIGHEST)
        tile = jnp.where(lane < hh, pre, jnp.where(lane < 2 * hh, suf, beta[n * c:(n + 1) * c]))
        o_ref[pl.ds(n * c, c), :] = tile
        t_rows = _dot_nt(eye, tile, precision=HIGHEST)
        g_rows = jnp.concatenate([t_rows[0:hh], t_rows[hh:2 * hh]], axis=1)
        b_rows = jnp.concatenate([t_rows[2 * hh:3 * hh], t_rows[3 * hh:4 * hh]], axis=1)
        rows_ref[n] = jnp.concatenate([g_rows, b_rows, b_rows * jnp.exp(g_rows)], axis=1)


def _gdn_gates(xf, w_gate_bf16, alog_vec, dtb_vec, b, s):
    d = xf.shape[1]
    n_chunks = s // GDN_CHUNK
    full = lambda shape: pl.BlockSpec(shape, lambda i: (0,) * len(shape))
    return pl.pallas_call(
        _gdn_gates_kernel,
        out_shape=(jax.ShapeDtypeStruct((b * s, GDN_GATE_LANES), F32),
                   jax.ShapeDtypeStruct((b * n_chunks, GDN_HEADS, 3 * GDN_GATE_LANES), F32)),
        grid=(b,),
        in_specs=[pl.BlockSpec((s, d), lambda i: (i, 0)), full((d, GDN_GATE_LANES)),
                  full((1, GDN_GATE_LANES)), full((1, GDN_GATE_LANES))],
        out_specs=(pl.BlockSpec((s, GDN_GATE_LANES), lambda i: (i, 0)),
                   pl.BlockSpec((n_chunks, GDN_HEADS, 3 * GDN_GATE_LANES), lambda i: (i, 0, 0))),
        compiler_params=_cparams("parallel"),
        name="gdn_gates",
    )(xf, w_gate_bf16, alog_vec, dtb_vec)


def _block_diag2(r, is_b):
    return jnp.concatenate([jnp.where(is_b, 0.0, r), jnp.where(is_b, r, 0.0)], axis=0)


def _gdn_inproj_kernel(x_ref, w_ref, cw_ref, o_ref, pad_ref):
    s = x_ref.shape[0]
    n = w_ref.shape[1]
    dk = GDN_HEAD_DIM
    kind = pl.program_id(1) // (GDN_WIDTH // n)
    slab = min(s, GDN_INPROJ_SLAB)
    n_slabs = s // slab
    project = lambda k: _dot(x_ref[pl.ds(k * slab, slab), :].astype(BF16), w_ref[...])

    @pl.when(kind == 3)
    def _():
        for k in range(n_slabs):
            o_ref[pl.ds(k * slab, slab), :] = project(k).astype(o_ref.dtype)

    @pl.when(kind < 3)
    def _():
        norm_on = kind < 2
        q_scale = jnp.where(kind == 0, GDN_HEAD_DIM ** -0.5, 1.0)
        zeros = jnp.zeros((GDN_CONV_PAD, n), F32)
        pad_ref[pl.ds(0, GDN_CONV_PAD), :] = zeros
        pad_ref[pl.ds(GDN_CONV_PAD + s, GDN_CONV_PAD), :] = zeros
        cw = cw_ref[...]

        def finish(k):
            r0 = k * slab
            acc = None
            for j in range(GDN_CONV):
                term = pad_ref[pl.ds(GDN_CONV_PAD + r0 + j - GDN_CONV // 2, slab), :] * cw[j:j + 1, :]
                acc = term if acc is None else acc + term
            act = _silu(acc)
            outs = []
            for hd in range(n // dk):
                a = act[:, hd * dk:(hd + 1) * dk]
                inv = lax.rsqrt(jnp.sum(a * a, axis=-1, keepdims=True) + 1e-6) * q_scale
                outs.append(a * jnp.where(norm_on, inv, 1.0))
            o_ref[pl.ds(r0, slab), :] = jnp.concatenate(outs, axis=1).astype(o_ref.dtype)

        for k in range(n_slabs):
            pad_ref[pl.ds(GDN_CONV_PAD + k * slab, slab), :] = project(k)
            if k >= 1:
                finish(k - 1)
        finish(n_slabs - 1)


def _gdn_inproj(xf, w_main_bf16, conv_w, b, s):
    d = xf.shape[1]
    n = GDN_INPROJ_COLS
    n_conv_blocks = 3 * GDN_WIDTH // n
    return pl.pallas_call(
        _gdn_inproj_kernel,
        out_shape=jax.ShapeDtypeStruct((b * s, 4 * GDN_WIDTH), BF16),
        grid=(b, 4 * GDN_WIDTH // n),
        in_specs=[pl.BlockSpec((s, d), lambda i, j: (i, 0)),
                  pl.BlockSpec((d, n), lambda i, j: (0, j)),
                  pl.BlockSpec((GDN_CONV, n), lambda i, j: (0, jnp.minimum(j, n_conv_blocks - 1)))],
        out_specs=pl.BlockSpec((s, n), lambda i, j: (i, j)),
        scratch_shapes=[pltpu.VMEM((s + 2 * GDN_CONV_PAD, n), F32)],
        compiler_params=_cparams("parallel", "parallel"),
        name="gdn_inproj",
    )(xf, w_main_bf16, conv_w)


def _gdn_core_kernel(q_ref, k_ref, v_ref, z_ref, gt_ref, rows_ref, nw_ref, o_ref,
                     u_ref, w_ref, qd_ref, kd_ref, aqk_ref, dend_ref, of_ref, ob_ref):
    s = q_ref.shape[0]
    dk = GDN_HEAD_DIM
    hps = q_ref.shape[1] // dk
    c = GDN_CHUNK
    n_chunks = s // c
    first_head = pl.program_id(1) * hps
    slab = min(s, 256)
    head_cols = [slice(hd * dk, (hd + 1) * dk) for hd in range(hps)]

    lane = lax.broadcasted_iota(I32, (c, 2 * c), 1)
    row = lax.broadcasted_iota(I32, (c, 2 * c), 0)
    is_b = lane >= c
    col = jnp.where(is_b, lane - c, lane)
    ahead = jnp.where(is_b, col - row, row - col)
    incl = ahead >= 0
    strict = ahead > 0
    eye2 = jnp.where(row == col, 1.0, 0.0)
    shifts = [lax.rem(2 * V7X_LANES - first_head - hd, V7X_LANES) for hd in range(hps)]

    group = min(GDN_PREP_GROUP, n_chunks)
    n_groups = n_chunks // group

    def pair_rows(t):
        if isinstance(t, int):
            return pl.ds(t * c, c), pl.ds((n_chunks - 1 - t) * c, c)
        return pl.ds(pl.multiple_of(t * c, c), c), pl.ds(pl.multiple_of((n_chunks - 1 - t) * c, c), c)

    def prep_stages(hd, pg, slot):
        rows = [pair_rows(pg * group + i) for i in range(group)]
        cols = head_cols[hd]
        q_f, k_f, v_f = ([ref[r[0], cols].astype(F32) for r in rows] for ref in (q_ref, k_ref, v_ref))
        q_b, k_b, v_b = ([ref[r[1], cols].astype(F32) for r in rows] for ref in (q_ref, k_ref, v_ref))
        gt_f = [pltpu.roll(gt_ref[r[0], :], shifts[hd], axis=1) for r in rows]
        gt_b = [pltpu.roll(gt_ref[r[1], :], shifts[hd], axis=1) for r in rows]
        g_f = [t[:, 0:1] for t in gt_f]
        b_f = [t[:, 2 * GDN_HEADS:2 * GDN_HEADS + 1] for t in gt_f]
        g_b = [t[:, GDN_HEADS:GDN_HEADS + 1] for t in gt_b]
        b_b = [t[:, 3 * GDN_HEADS:3 * GDN_HEADS + 1] for t in gt_b]
        head = pl.ds(first_head + hd, 1)
        g_row, beta_row, be_row = [], [], []
        for i in range(group):
            t = pg * group + i
            row_f, row_b = rows_ref[t, head, :], rows_ref[n_chunks - 1 - t, head, :]
            pick = lambda j: jnp.where(is_b[0:1], row_b[:, j * 2 * c:(j + 1) * 2 * c], row_f[:, j * 2 * c:(j + 1) * 2 * c])
            g_row.append(pick(0))
            beta_row.append(pick(1))
            be_row.append(pick(2))
        dmat = [jnp.where(is_b, g_b[i], g_f[i]) - g_row[i] for i in range(group)]
        prod_f, prod_b = [], []
        for i in range(group):
            kfb, kbb = k_f[i].astype(BF16), k_b[i].astype(BF16)
            keys = jnp.concatenate([kfb, kbb], axis=0)
            prod_f.append(_dot_nt(jnp.concatenate([q_f[i].astype(BF16), kfb], axis=0), keys))
            yield
            prod_b.append(_dot_nt(jnp.concatenate([q_b[i].astype(BF16), kbb], axis=0), keys))
            yield
        qk2 = [jnp.where(is_b, prod_b[i][:c], prod_f[i][:c]) for i in range(group)]
        kk2 = [jnp.where(is_b, prod_b[i][c:], prod_f[i][c:]) for i in range(group)]
        gamma = [jnp.exp(jnp.where(incl, d_i, NEG_BIG)) for d_i in dmat]
        m = [jnp.where(strict, kk2[i] * gamma[i], 0.0) * jnp.where(is_b, b_b[i], b_f[i]) for i in range(group)]
        x_inv = [eye2 - m_i for m_i in m]
        p = []
        for m_i in m:
            p.append(_dot(m_i.astype(BF16), _block_diag2(m_i, is_b).astype(BF16)))
            yield
        for _ in range(5):
            y = []
            for i in range(group):
                y.append(_dot(jnp.concatenate([x_inv[i], p[i]], axis=0).astype(BF16),
                              _block_diag2(p[i], is_b).astype(BF16)))
                yield
            x_inv = [x_inv[i] + y[i][:c] for i in range(group)]
            p = [y_i[c:] for y_i in y]
        eg_f = [jnp.exp(g) for g in g_f]
        eg_b = [jnp.exp(g) for g in g_b]
        u_sol, w_sol = [], []
        for i in range(group):
            u_sol.append(_dot(_block_diag2(x_inv[i] * beta_row[i], is_b).astype(BF16),
                              jnp.concatenate([v_f[i], v_b[i]], axis=0).astype(BF16)))
            yield
            w_sol.append(_dot(_block_diag2(x_inv[i] * be_row[i], is_b).astype(BF16),
                              jnp.concatenate([k_f[i], k_b[i]], axis=0).astype(BF16)))
            yield
        for i in range(group):
            local = pl.ds(i * c, c)
            gl_f, gl_b = g_f[i][c - 1:c, :], g_b[i][0:1, :]
            for d, (qc, kc, eg, gl, g_d) in enumerate(((q_f[i], k_f[i], eg_f[i], gl_f, g_f[i]),
                                                       (q_b[i], k_b[i], eg_b[i], gl_b, g_b[i]))):
                u_ref[hd, slot, d, local, :] = u_sol[i][d * c:(d + 1) * c]
                w_ref[hd, slot, d, local, :] = w_sol[i][d * c:(d + 1) * c]
                qd_ref[hd, slot, d, local, :] = qc * eg
                kd_ref[hd, slot, d, local, :] = kc * jnp.exp(gl - g_d)
            aqk_ref[hd, slot, local, :] = jnp.where(incl, qk2[i] * gamma[i], 0.0)
            dend_ref[hd, slot, pl.ds(i * V7X_SUBLANES, V7X_SUBLANES), :] = jnp.where(
                is_b[:V7X_SUBLANES], jnp.exp(gl_b), jnp.exp(gl_f))

    lanes = (slice(0, c), slice(c, 2 * c))
    chains = [(hd, d) for hd in range(hps) for d in range(2)]

    def run_group(pg, slot, prep_gens, states):
        per_gap = -(-GDN_PREP_MATMULS // 2)

        def advance():
            for _ in range(per_gap):
                for gen in prep_gens:
                    next(gen, None)

        for i in range(group):
            rows = pair_rows(pg * group + i)
            local = pl.ds(i * c, c)
            stb = [st.astype(BF16) for st in states]
            ws = [_dot(w_ref[hd, slot, d, local, :].astype(BF16), stb[n]) for n, (hd, d) in enumerate(chains)]
            qs = [_dot(qd_ref[hd, slot, d, local, :].astype(BF16), stb[n]) for n, (hd, d) in enumerate(chains)]
            advance()
            vnb = [(u_ref[hd, slot, d, local, :] - ws[n]).astype(BF16) for n, (hd, d) in enumerate(chains)]
            av = [_dot(aqk_ref[hd, slot, local, :][:, lanes[d]].astype(BF16), vnb[n])
                  for n, (hd, d) in enumerate(chains)]
            kv = [_dot_tn(kd_ref[hd, slot, d, local, :].astype(BF16), vnb[n]) for n, (hd, d) in enumerate(chains)]
            advance()
            new_states = []
            for n, (hd, d) in enumerate(chains):
                (of_ref, ob_ref)[d][hd, rows[d], :] = qs[n] + av[n]
                dend = dend_ref[hd, slot, pl.ds(i * V7X_SUBLANES, 1), :][:, d * c:d * c + 1]
                new_states.append(states[n] * dend + kv[n])
            states = tuple(new_states)
        for gen in prep_gens:
            for _ in gen:
                pass
        return states

    for hd in range(hps):
        for _ in prep_stages(hd, 0, 0):
            pass

    def fused(tg, states):
        slot = lax.rem(tg, 2)
        return run_group(tg, slot, [prep_stages(hd, tg + 1, 1 - slot) for hd in range(hps)], states)

    zero_state = jnp.zeros((dk, dk), F32)
    states = lax.fori_loop(0, n_groups - 1, fused, (zero_state,) * len(chains))
    run_group(n_groups - 1, (n_groups - 1) % 2, [], states)

    for hd in range(hps):
        cols = slice(hd * dk, (hd + 1) * dk)
        for r0 in range(0, s, slab):
            rs = pl.ds(r0, slab)
            o = of_ref[hd, rs, :] + ob_ref[hd, rs, :]
            o = o * lax.rsqrt(jnp.mean(o * o, axis=-1, keepdims=True) + 1e-6) * nw_ref[...]
            o_ref[rs, cols] = (o * _silu(z_ref[rs, cols].astype(F32))).astype(o_ref.dtype)


def _gdn_core(proj, gates, gate_rows, norm_w, b, s):
    dk = GDN_HEAD_DIM
    hps = GDN_HEADS_PER_STEP
    nhb = GDN_HEADS // hps
    blk = lambda kind: pl.BlockSpec((s, hps * dk), lambda i, h: (i, kind * nhb + h))
    group_rows = min(GDN_PREP_GROUP, s // GDN_CHUNK) * GDN_CHUNK
    seq = lambda: pltpu.VMEM((hps, s, dk), F32)
    slots = lambda: pltpu.VMEM((hps, 2, 2, group_rows, dk), F32)
    return pl.pallas_call(
        _gdn_core_kernel,
        out_shape=jax.ShapeDtypeStruct((b * s, GDN_WIDTH), BF16),
        grid=(b, nhb),
        in_specs=[blk(0), blk(1), blk(2), blk(3),
                  pl.BlockSpec((s, GDN_GATE_LANES), lambda i, h: (i, 0)),
                  pl.BlockSpec((s // GDN_CHUNK, GDN_HEADS, 3 * GDN_GATE_LANES), lambda i, h: (i, 0, 0)),
                  pl.BlockSpec((1, dk), lambda i, h: (0, 0))],
        out_specs=pl.BlockSpec((s, hps * dk), lambda i, h: (i, h)),
        scratch_shapes=[slots(), slots(), slots(), slots(),
                        pltpu.VMEM((hps, 2, group_rows, 2 * GDN_CHUNK), F32),
                        pltpu.VMEM((hps, 2, group_rows // GDN_CHUNK * V7X_SUBLANES, 2 * GDN_CHUNK), F32),
                        seq(), seq()],
        compiler_params=_cparams("parallel", "parallel"),
        name="gdn_core",
    )(proj, proj, proj, proj, gates, gate_rows, norm_w.reshape(1, dk))


def _gdn_mixer_core(xf, w_in, conv_w, a_log, dt_bias, norm_w, b, s):
    n_main = 4 * GDN_WIDTH
    n_gate = 4 * GDN_HEADS
    proj = _gdn_inproj(xf, w_in[:, :n_main].astype(BF16), conv_w.astype(F32), b, s)
    w_gate = jnp.pad(w_in[:, n_main:], ((0, 0), (0, GDN_GATE_LANES - n_gate))).astype(BF16)
    lane_pad = lambda v: jnp.pad(v.astype(F32).reshape(1, 2 * GDN_HEADS), ((0, 0), (0, GDN_GATE_LANES - 2 * GDN_HEADS)))
    gates, gate_rows = _gdn_gates(xf, w_gate, lane_pad(a_log), lane_pad(dt_bias), b, s)
    return _gdn_core(proj, gates, gate_rows, norm_w.astype(F32), b, s)


def kernel(x, a_w_in, a_conv, a_A_log, a_dt_bias, a_norm_w, a_w_out, b_w_in, b_sink, b_w_out, rel_bias, c_w_in, c_w_group, c_scale, c_w_out, router_w, router_b, moe_w_gate, moe_w_up, moe_w_down, ln_g, ln_b):
    b, s, d = x.shape
    t = b * s
    xf = x.reshape(t, d)
    xb = xf
    router_wt = router_w.T.astype(BF16)
    router_b_col = router_b.reshape(N_EXPERTS, 1).astype(F32)
    bias_tbl = _attn_bias_table(rel_bias)
    for i in range(DEPTH):
        kind, j = i % N_MIXERS, i // N_MIXERS
        if kind == 0:
            h = _gdn_mixer_core(xb, a_w_in[j], a_conv[j], a_A_log[j], a_dt_bias[j], a_norm_w[j], b, s)
            w_out = a_w_out[j]
        elif kind == 1:
            proj = _linear(xb, b_w_in[j].astype(BF16), BF16)
            h = _attn_core(proj, bias_tbl, b_sink[j], b, s)
            w_out = b_w_out[j]
        else:
            u = _linear(xb, c_w_in[j].astype(BF16), F32)
            h = _pool_core(u.reshape(b, s, d), c_w_group[j].astype(BF16), c_scale[j]).reshape(t, d)
            w_out = c_w_out[j]
        x1w, route_i, counts = _post_mixer(h, w_out.astype(BF16), xf, ln_g[i, 0], ln_b[i, 0], router_wt, router_b_col)
        xf, xb = _moe_layer(x1w, route_i, counts[:, 0], moe_w_gate, moe_w_up, moe_w_down, i, ln_g[i, 1], ln_b[i, 1])
    return xf.reshape(b, s, d)
```

```python
import functools
import math

import jax
import jax.numpy as jnp
from jax import lax
from jax.experimental import pallas as pl
from jax.experimental.pallas import tpu as pltpu

F32 = jnp.float32
BF16 = jnp.bfloat16
I32 = jnp.int32

D_MODEL = 1024
DEPTH = 4
N_MIXERS = 3
DEEPNORM_ALPHA = (2.0 * DEPTH) ** 0.25
LN_EPS = 1e-5

GDN_HEADS = 8
GDN_HEAD_DIM = 128
GDN_WIDTH = GDN_HEADS * GDN_HEAD_DIM
GDN_CONV = 5
GDN_CHUNK = 64

ATT_HEADS = 16
ATT_KV_HEADS = 4
ATT_HEAD_DIM = 64
ATT_GROUP = ATT_HEADS // ATT_KV_HEADS
ATT_WINDOW = 128
ATT_QBLOCK = 128
ATT_KSPAN = ATT_QBLOCK + 2 * ATT_WINDOW
REL_BUCKETS = 32
REL_MAX_DIST = 128

POOL_WINDOWS = (2, 4, 8, 16)
POOL_GROUP_DIM = D_MODEL // len(POOL_WINDOWS)

N_EXPERTS = 16
N_EXPERT_GROUPS = 4
EXPERTS_PER_GROUP = N_EXPERTS // N_EXPERT_GROUPS
TOP_K = 2
EXPERT_DIM = 512
PAIRS_PER_GROUP = EXPERTS_PER_GROUP * (EXPERTS_PER_GROUP - 1) // 2
N_SEGMENTS = N_EXPERT_GROUPS * PAIRS_PER_GROUP

V7X_LANES = 128
V7X_SUBLANES = 8
V7X_VMEM_LIMIT_BYTES = 56 * 1024 * 1024

ROW_TILE = 512
MOE_BLOCK = 256
ROW_EXTRA = 128
NEG_BIG = -1e30

HIGHEST = lax.Precision.HIGHEST


def _cparams(*sem):
    return pltpu.CompilerParams(dimension_semantics=tuple(sem), vmem_limit_bytes=V7X_VMEM_LIMIT_BYTES)


def _dot(a, b, **kw):
    return jnp.dot(a, b, preferred_element_type=F32, **kw)


def _dot_nt(a, b, **kw):
    return lax.dot_general(a, b, (((1,), (1,)), ((), ())), preferred_element_type=F32, **kw)


def _dot_tn(a, b, **kw):
    return lax.dot_general(a, b, (((0,), (0,)), ((), ())), preferred_element_type=F32, **kw)


def _silu(x):
    return x * jax.nn.sigmoid(x)


def _linear_kernel(x_ref, w_ref, o_ref):
    o_ref[...] = _dot(x_ref[...].astype(BF16), w_ref[...]).astype(o_ref.dtype)


def _linear(x, w_bf16, out_dtype, tm=ROW_TILE):
    m, k = x.shape
    n = w_bf16.shape[1]
    return pl.pallas_call(
        _linear_kernel,
        out_shape=jax.ShapeDtypeStruct((m, n), out_dtype),
        grid=(m // tm,),
        in_specs=[pl.BlockSpec((tm, k), lambda i: (i, 0)), pl.BlockSpec((k, n), lambda i: (0, 0))],
        out_specs=pl.BlockSpec((tm, n), lambda i: (i, 0)),
        compiler_params=_cparams("parallel"),
        name="linear",
    )(x, w_bf16)


def _layer_norm_rows(s, g, b):
    mu = jnp.mean(s, axis=-1, keepdims=True)
    xc = s - mu
    var = jnp.mean(xc * xc, axis=-1, keepdims=True)
    return xc * lax.rsqrt(var + LN_EPS) * g + b


def _top2_of4(a, b, c, d):
    hi1, lo1 = jnp.maximum(a, b), jnp.minimum(a, b)
    hi2, lo2 = jnp.maximum(c, d), jnp.minimum(c, d)
    return jnp.maximum(hi1, hi2) + jnp.maximum(jnp.minimum(hi1, hi2), jnp.maximum(lo1, lo2))


def _route_rows(scores, biased):
    bi = [biased[e:e + 1, :] for e in range(N_EXPERTS)]
    sc = [scores[e:e + 1, :] for e in range(N_EXPERTS)]
    gs = [_top2_of4(*bi[4 * g:4 * g + 4]) for g in range(N_EXPERT_GROUPS)]
    group = jnp.zeros_like(gs[0], dtype=I32)
    best = gs[0]
    for g in range(1, N_EXPERT_GROUPS):
        upd = gs[g] > best
        group = jnp.where(upd, g, group)
        best = jnp.where(upd, gs[g], best)

    def pick(rows, j):
        out = rows[j]
        for g in range(1, N_EXPERT_GROUPS):
            out = jnp.where(group == g, rows[4 * g + j], out)
        return out

    v = [pick(bi, j) for j in range(EXPERTS_PER_GROUP)]
    s = [pick(sc, j) for j in range(EXPERTS_PER_GROUP)]
    i1 = jnp.zeros_like(group)
    b1 = v[0]
    for j in range(1, EXPERTS_PER_GROUP):
        upd = v[j] > b1
        i1 = jnp.where(upd, j, i1)
        b1 = jnp.where(upd, v[j], b1)
    i2 = jnp.full_like(group, -1)
    b2 = jnp.full_like(b1, -jnp.inf)
    for j in range(EXPERTS_PER_GROUP):
        upd = (i1 != j) & ((v[j] > b2) | (i2 < 0))
        i2 = jnp.where(upd, j, i2)
        b2 = jnp.where(upd, v[j], b2)
    s1 = s[0]
    s2 = s[0]
    for j in range(1, EXPERTS_PER_GROUP):
        s1 = jnp.where(i1 == j, s[j], s1)
        s2 = jnp.where(i2 == j, s[j], s2)
    den = s1 + s2
    first_is_lo = i1 < i2
    lo = jnp.where(first_is_lo, i1, i2)
    hi = jnp.where(first_is_lo, i2, i1)
    pair = lax.shift_right_logical(lo * (7 - lo), 1) + hi - lo - 1
    g1, g2 = s1 / den, s2 / den
    return group * PAIRS_PER_GROUP + pair, jnp.where(first_is_lo, g1, g2), jnp.where(first_is_lo, g2, g1)


def _post_mixer_kernel(h_ref, w_ref, x_ref, g_ref, b_ref, rwt_ref, rb_ref, ustrict_ref, eye_ref,
                       x1w_ref, ri_ref, cnt_ref, base_ref):
    tm, d = x_ref.shape

    @pl.when(pl.program_id(0) == 0)
    def _():
        base_ref[...] = jnp.zeros_like(base_ref)

    y = _dot(h_ref[...], w_ref[...])
    x1 = _layer_norm_rows(DEEPNORM_ALPHA * x_ref[...] + y, g_ref[...], b_ref[...])
    x1w_ref[:, 0:d] = x1

    logits = _dot_nt(rwt_ref[...], x1.astype(BF16))
    scores = jax.nn.sigmoid(logits)
    seg, g_lo, g_hi = _route_rows(scores, scores + rb_ref[...])

    sidx = lax.broadcasted_iota(I32, (N_SEGMENTS, tm), 0)
    hit = sidx == seg
    onehot = jnp.where(hit, 1.0, 0.0)
    base = base_ref[:, 0:1]
    before = _dot(onehot.astype(BF16), ustrict_ref[...]) + base
    rank = jnp.sum(jnp.where(hit, before, 0.0), axis=0, keepdims=True)
    new_base = base + jnp.sum(onehot, axis=1, keepdims=True)
    base_ref[...] = jnp.broadcast_to(new_base, base_ref.shape)
    cnt_ref[...] = jnp.broadcast_to(new_base, cnt_ref.shape)

    ri_ref[...] = jnp.concatenate([seg, rank.astype(I32), jnp.zeros((6, tm), I32)], axis=0)
    gates = jnp.concatenate([g_lo, g_hi, jnp.zeros((ROW_EXTRA - 2, tm), F32)], axis=0)
    hi = gates.astype(BF16)
    lo = (gates - hi.astype(F32)).astype(BF16)
    x1w_ref[:, d:d + ROW_EXTRA] = _dot_nt(eye_ref[...], hi) + _dot_nt(eye_ref[...], lo)


def _post_mixer(h_bf16, w_out_bf16, x, ln_g, ln_b, router_wt, router_b_col, tm=ROW_TILE):
    t, d = x.shape
    ustrict = (jnp.arange(tm)[:, None] < jnp.arange(tm)[None, :]).astype(BF16)
    eye = jnp.eye(tm, dtype=BF16)
    full = lambda shape: pl.BlockSpec(shape, lambda i: (0,) * len(shape))
    return pl.pallas_call(
        _post_mixer_kernel,
        out_shape=(jax.ShapeDtypeStruct((t, d + ROW_EXTRA), F32),
                   jax.ShapeDtypeStruct((8, t), I32),
                   jax.ShapeDtypeStruct((N_SEGMENTS, V7X_LANES), F32)),
        grid=(t // tm,),
        in_specs=[pl.BlockSpec((tm, h_bf16.shape[1]), lambda i: (i, 0)),
                  full(w_out_bf16.shape),
                  pl.BlockSpec((tm, d), lambda i: (i, 0)),
                  full((1, d)), full((1, d)),
                  full((N_EXPERTS, d)), full((N_EXPERTS, 1)),
                  full((tm, tm)), full((tm, tm))],
        out_specs=(pl.BlockSpec((tm, d + ROW_EXTRA), lambda i: (i, 0)),
                   pl.BlockSpec((8, tm), lambda i: (0, i)),
                   pl.BlockSpec((N_SEGMENTS, V7X_LANES), lambda i: (0, 0))),
        scratch_shapes=[pltpu.VMEM((N_SEGMENTS, V7X_LANES), F32)],
        compiler_params=_cparams("arbitrary"),
        name="post_mixer_router",
    )(h_bf16, w_out_bf16, x, ln_g.reshape(1, d), ln_b.reshape(1, d), router_wt, router_b_col, ustrict, eye)


def _row_copy(src_ref, src_row, dst_ref, dst_row, sem):
    return pltpu.make_async_copy(src_ref.at[pl.ds(src_row, 1), :], dst_ref.at[pl.ds(dst_row, 1), :], sem)


def _dispatch_kernel(n_tiles, pad_lo_ref, pad_hi_ref, x_ref, dest_hbm, xs_hbm,
                     idx_smem, zero_ref, idx_sem, row_sem, pad_sem):
    tm = x_ref.shape[0]
    i = pl.program_id(0)
    slot = lax.rem(i, 2)

    def idx_copy(tile, sl):
        return pltpu.make_async_copy(dest_hbm.at[tile], idx_smem.at[sl], idx_sem.at[sl])

    @pl.when(i == 0)
    def _():
        idx_copy(0, 0).start()

    idx_copy(i, slot).wait()

    @pl.when(i + 1 < n_tiles)
    def _():
        idx_copy(i + 1, 1 - slot).start()

    def issue(j, carry):
        _row_copy(x_ref, j, xs_hbm, idx_smem[slot, 0, j], row_sem).start()
        return carry

    lax.fori_loop(0, tm, issue, 0, unroll=8)

    def drain(j, carry):
        _row_copy(x_ref, j, xs_hbm, idx_smem[slot, 0, j], row_sem).wait()
        return carry

    lax.fori_loop(0, tm, drain, 0, unroll=8)

    @pl.when(i == pl.num_programs(0) - 1)
    def _():
        zero_ref[...] = jnp.zeros_like(zero_ref)
        for e in range(N_SEGMENTS + 1):
            def fill(r, carry):
                _row_copy(zero_ref, 0, xs_hbm, r, pad_sem).start()
                return carry

            lax.fori_loop(pad_lo_ref[e], pad_hi_ref[e], fill, 0)
        for e in range(N_SEGMENTS + 1):
            def filled(r, carry):
                _row_copy(zero_ref, 0, xs_hbm, r, pad_sem).wait()
                return carry

            lax.fori_loop(pad_lo_ref[e], pad_hi_ref[e], filled, 0)


def _dispatch(x1w, dest_tiles, pad_lo, pad_hi, n_rows, tm=ROW_TILE):
    t, d = x1w.shape
    grid_spec = pltpu.PrefetchScalarGridSpec(
        num_scalar_prefetch=2,
        grid=(t // tm,),
        in_specs=[pl.BlockSpec((tm, d), lambda i, lo, hi: (i, 0)),
                  pl.BlockSpec(memory_space=pl.ANY)],
        out_specs=pl.BlockSpec(memory_space=pl.ANY),
        scratch_shapes=[pltpu.SMEM((2, 1, tm), I32), pltpu.VMEM((V7X_SUBLANES, d), F32),
                        pltpu.SemaphoreType.DMA((2,)), pltpu.SemaphoreType.DMA, pltpu.SemaphoreType.DMA],
    )
    return pl.pallas_call(
        functools.partial(_dispatch_kernel, t // tm),
        out_shape=jax.ShapeDtypeStruct((n_rows, d), F32),
        grid_spec=grid_spec,
        compiler_params=_cparams("arbitrary"),
        name="moe_dispatch",
    )(pad_lo, pad_hi, x1w, dest_tiles)


def _expert_kernel(ea_ref, eb_ref, xs_ref, wga_ref, wua_ref, wda_ref, wgb_ref, wub_ref, wdb_ref, ys_ref):
    del ea_ref, eb_ref
    d = ys_ref.shape[1]
    x = xs_ref[:, 0:d].astype(BF16)
    extra = xs_ref[:, d:d + ROW_EXTRA]
    y = None
    for col, (wg_ref, wu_ref, wd_ref) in enumerate(((wga_ref, wua_ref, wda_ref), (wgb_ref, wub_ref, wdb_ref))):
        hidden = _silu(_dot(x, wg_ref[0, 0].astype(BF16))) * _dot(x, wu_ref[0, 0].astype(BF16))
        part = _dot((hidden * extra[:, col:col + 1]).astype(BF16), wd_ref[0, 0].astype(BF16))
        y = part if y is None else y + part
    ys_ref[...] = y


def _experts(block_ea, block_eb, xs, wg, wu, wd, layer):
    n_rows = xs.shape[0]
    d = xs.shape[1] - ROW_EXTRA
    n_blocks = n_rows // MOE_BLOCK
    f = wg.shape[3]
    up = lambda which: pl.BlockSpec((1, 1, d, f), lambda i, ea, eb: (layer, (ea, eb)[which][i], 0, 0))
    down = lambda which: pl.BlockSpec((1, 1, f, d), lambda i, ea, eb: (layer, (ea, eb)[which][i], 0, 0))
    grid_spec = pltpu.PrefetchScalarGridSpec(
        num_scalar_prefetch=2,
        grid=(n_blocks,),
        in_specs=[pl.BlockSpec((MOE_BLOCK, d + ROW_EXTRA), lambda i, ea, eb: (i, 0)),
                  up(0), up(0), down(0), up(1), up(1), down(1)],
        out_specs=pl.BlockSpec((MOE_BLOCK, d), lambda i, ea, eb: (i, 0)),
    )
    return pl.pallas_call(
        _expert_kernel,
        out_shape=jax.ShapeDtypeStruct((n_rows, d), F32),
        grid_spec=grid_spec,
        compiler_params=_cparams("parallel"),
        name="moe_experts",
    )(block_ea, block_eb, xs, wg, wu, wd, wg, wu, wd)


def _combine_kernel(n_tiles, x1w_ref, g_ref, b_ref, dest_hbm, ys_hbm, o_ref, ob_ref,
                    idx_smem, ybuf, idx_sem, row_sem):
    tm, d = o_ref.shape
    i = pl.program_id(0)
    slot = lax.rem(i, 2)
    other = 1 - slot

    def idx_copy(tile, sl):
        return pltpu.make_async_copy(dest_hbm.at[tile], idx_smem.at[sl], idx_sem.at[sl])

    def gather(sl, start):
        def body(j, carry):
            cp = _row_copy(ys_hbm, idx_smem[sl, 0, j], ybuf.at[sl], j, row_sem.at[sl])
            if start:
                cp.start()
            else:
                cp.wait()
            return carry

        lax.fori_loop(0, tm, body, 0, unroll=8)

    @pl.when(i == 0)
    def _():
        first = idx_copy(0, 0)
        first.start()
        first.wait()
        gather(0, True)
        if n_tiles > 1:
            idx_copy(1, 1).start()

    @pl.when(i + 1 < n_tiles)
    def _():
        idx_copy(i + 1, other).wait()
        gather(other, True)

    gather(slot, False)

    @pl.when(i + 2 < n_tiles)
    def _():
        idx_copy(i + 2, slot).start()

    x2 = _layer_norm_rows(DEEPNORM_ALPHA * x1w_ref[:, 0:d] + ybuf[slot], g_ref[...], b_ref[...])
    o_ref[...] = x2
    ob_ref[...] = x2.astype(ob_ref.dtype)


def _combine(x1w, ln_g, ln_b, dest_tiles, ys, tm=ROW_TILE):
    t = x1w.shape[0]
    d = ys.shape[1]
    full = lambda shape: pl.BlockSpec(shape, lambda i: (0,) * len(shape))
    return pl.pallas_call(
        functools.partial(_combine_kernel, t // tm),
        out_shape=(jax.ShapeDtypeStruct((t, d), F32), jax.ShapeDtypeStruct((t, d), BF16)),
        grid=(t // tm,),
        in_specs=[pl.BlockSpec((tm, d + ROW_EXTRA), lambda i: (i, 0)),
                  full((1, d)), full((1, d)),
                  pl.BlockSpec(memory_space=pl.ANY),
                  pl.BlockSpec(memory_space=pl.ANY)],
        out_specs=(pl.BlockSpec((tm, d), lambda i: (i, 0)), pl.BlockSpec((tm, d), lambda i: (i, 0))),
        scratch_shapes=[pltpu.SMEM((2, 1, tm), I32), pltpu.VMEM((2, tm, d), F32),
                        pltpu.SemaphoreType.DMA((2,)), pltpu.SemaphoreType.DMA((2,))],
        compiler_params=_cparams("arbitrary"),
        name="moe_combine_ln",
    )(x1w, ln_g.reshape(1, d), ln_b.reshape(1, d), dest_tiles, ys)


def _moe_layer(x1w, route_i, counts, wg, wu, wd, layer, ln_g, ln_b, tm=ROW_TILE):
    t = x1w.shape[0]
    n_tiles = t // tm
    n_rows = t + N_SEGMENTS * MOE_BLOCK
    counts = counts.astype(I32)
    padded = (counts + MOE_BLOCK - 1) // MOE_BLOCK * MOE_BLOCK
    pends = jnp.cumsum(padded)
    pstarts = pends - padded
    seg = route_i[0:1]
    start_of = jnp.sum(jnp.where(seg[None] == jnp.arange(N_SEGMENTS, dtype=I32)[:, None, None],
                                 pstarts[:, None, None], 0), axis=0)
    dest = start_of + route_i[1:2]
    dest_tiles = dest.reshape(1, n_tiles, tm).transpose(1, 0, 2)
    n_blocks = n_rows // MOE_BLOCK
    block_seg = jnp.minimum(
        jnp.sum((jnp.arange(n_blocks, dtype=I32) * MOE_BLOCK)[:, None] >= pends[None, :], axis=-1),
        N_SEGMENTS - 1).astype(I32)
    pair_lo = jnp.array([lo for lo in range(EXPERTS_PER_GROUP) for hi in range(lo + 1, EXPERTS_PER_GROUP)], I32)
    pair_hi = jnp.array([hi for lo in range(EXPERTS_PER_GROUP) for hi in range(lo + 1, EXPERTS_PER_GROUP)], I32)
    block_group, block_pair = block_seg // PAIRS_PER_GROUP, block_seg % PAIRS_PER_GROUP
    block_ea = block_group * EXPERTS_PER_GROUP + pair_lo[block_pair]
    block_eb = block_group * EXPERTS_PER_GROUP + pair_hi[block_pair]
    pad_lo = jnp.concatenate([pstarts + counts, pends[-1:]]).astype(I32)
    pad_hi = jnp.concatenate([pends, jnp.full((1,), n_rows, I32)]).astype(I32)
    xs = _dispatch(x1w, dest_tiles, pad_lo, pad_hi, n_rows, tm)
    ys = _experts(block_ea, block_eb, xs, wg, wu, wd, layer)
    return _combine(x1w, ln_g, ln_b, dest_tiles, ys, tm)


POOL_PAD = 16


def _pool_kernel(u_ref, wg_ref, scale_ref, o_ref, pad_ref):
    s, c = u_ref.shape[1], u_ref.shape[2]
    u = u_ref[0]
    zeros = jnp.zeros((POOL_PAD, c), F32)
    pad_ref[pl.ds(0, POOL_PAD), :] = zeros
    pad_ref[pl.ds(POOL_PAD + s, POOL_PAD), :] = zeros
    pad_ref[pl.ds(POOL_PAD, s), :] = u
    pos = lax.broadcasted_iota(I32, (s, c), 0)
    group = pl.program_id(1)
    for gi, win in enumerate(POOL_WINDOWS):
        @pl.when(group == gi)
        def _(win=win):
            half = win // 2
            total = pad_ref[pl.ds(POOL_PAD - half, s), :]
            for j in range(1 - half, half):
                total = total + pad_ref[pl.ds(POOL_PAD + j, s), :]
            count = (jnp.minimum(pos + half, s) - jnp.maximum(pos - half, 0)).astype(F32)
            mixed = total / count - u
            y = _dot(mixed.astype(BF16), wg_ref[0]) * scale_ref[...]
            o_ref[0] = y.astype(o_ref.dtype)


def _pool_core(u, w_group_bf16, scale):
    b, s, d = u.shape
    c = POOL_GROUP_DIM
    return pl.pallas_call(
        _pool_kernel,
        out_shape=jax.ShapeDtypeStruct((b, s, d), BF16),
        grid=(b, d // c),
        in_specs=[pl.BlockSpec((1, s, c), lambda i, g: (i, 0, g)),
                  pl.BlockSpec((1, c, c), lambda i, g: (g, 0, 0)),
                  pl.BlockSpec((1, c), lambda i, g: (0, g))],
        out_specs=pl.BlockSpec((1, s, c), lambda i, g: (i, 0, g)),
        scratch_shapes=[pltpu.VMEM((s + 2 * POOL_PAD, c), F32)],
        compiler_params=_cparams("parallel", "parallel"),
        name="pool_core",
    )(u, w_group_bf16, scale.reshape(1, d))


def _t5_bucket(rel):
    half = REL_BUCKETS // 2
    max_exact = half // 2
    n = jnp.abs(rel)
    large = max_exact + (jnp.log(jnp.maximum(n, 1).astype(F32) / max_exact)
                         / math.log(REL_MAX_DIST / max_exact) * (half - max_exact)).astype(I32)
    large = jnp.minimum(large, half - 1)
    return (rel > 0).astype(I32) * half + jnp.where(n < max_exact, n, large)


def _attn_bias_table(rel_bias):
    rel = jnp.arange(ATT_KSPAN)[None, :] - ATT_WINDOW - jnp.arange(ATT_QBLOCK)[:, None]
    onehot = (_t5_bucket(rel)[..., None] == jnp.arange(REL_BUCKETS)).astype(F32)
    bias = jnp.einsum("qkb,bh->hqk", onehot, rel_bias.astype(F32), precision=HIGHEST)
    return jnp.where((jnp.abs(rel) <= ATT_WINDOW)[None], bias, NEG_BIG)


def _attn_kernel(q_ref, kp_ref, kc_ref, kn_ref, vp_ref, vc_ref, vn_ref, bias_ref, sink_ref, o_ref):
    j = pl.program_id(1)
    nb = pl.num_programs(1)
    qb = ATT_QBLOCK
    col = lax.broadcasted_iota(I32, (qb, ATT_KSPAN), 1)
    edge = jnp.where(((j == 0) & (col < qb)) | ((j == nb - 1) & (col >= 2 * qb)), NEG_BIG, 0.0)
    k_all = jnp.concatenate([kp_ref[...], kc_ref[...], kn_ref[...]], axis=0)
    v_all = jnp.concatenate([vp_ref[...], vc_ref[...], vn_ref[...]], axis=0)
    q = q_ref[...] * (ATT_HEAD_DIM ** -0.5)
    heads = lambda g: range(g * ATT_GROUP, (g + 1) * ATT_GROUP)
    cols = lambda i: slice(i * ATT_HEAD_DIM, (i + 1) * ATT_HEAD_DIM)

    def scores(g):
        return [_dot_nt(q[:, cols(hd)], k_all[:, cols(g)]) for hd in heads(g)]

    def attend(g, raw):
        outs = []
        probs, dens = [], []
        for hd, qk in zip(heads(g), raw):
            logits = qk + bias_ref[hd] + edge
            sink = sink_ref[hd]
            mx = jnp.maximum(jnp.max(logits, axis=-1, keepdims=True), sink)
            p = jnp.exp(logits - mx)
            dens.append(jnp.sum(p, axis=-1, keepdims=True) + jnp.exp(sink - mx))
            probs.append(p.astype(BF16))
        for p, den in zip(probs, dens):
            outs.append(_dot(p, v_all[:, cols(g)]) / den)
        return outs

    outs = []
    raw = scores(0)
    for g in range(ATT_KV_HEADS):
        nxt = scores(g + 1) if g + 1 < ATT_KV_HEADS else None
        outs += attend(g, raw)
        raw = nxt
    o_ref[...] = jnp.concatenate(outs, axis=-1).astype(o_ref.dtype)


def _attn_core(proj, bias_tbl, sink, b, s):
    qw = ATT_HEADS * ATT_HEAD_DIM
    kw = ATT_KV_HEADS * ATT_HEAD_DIM
    nb = s // ATT_QBLOCK
    kcol, vcol = qw // kw, qw // kw + 1
    row = lambda i, j: i * nb + j
    prev = lambda i, j: i * nb + jnp.maximum(j - 1, 0)
    nxt = lambda i, j: i * nb + jnp.minimum(j + 1, nb - 1)
    kv_spec = lambda rowf, c: pl.BlockSpec((ATT_QBLOCK, kw), lambda i, j: (rowf(i, j), c))
    return pl.pallas_call(
        _attn_kernel,
        out_shape=jax.ShapeDtypeStruct((b * s, qw), BF16),
        grid=(b, nb),
        in_specs=[pl.BlockSpec((ATT_QBLOCK, qw), lambda i, j: (row(i, j), 0)),
                  kv_spec(prev, kcol), kv_spec(row, kcol), kv_spec(nxt, kcol),
                  kv_spec(prev, vcol), kv_spec(row, vcol), kv_spec(nxt, vcol),
                  pl.BlockSpec((ATT_HEADS, ATT_QBLOCK, ATT_KSPAN), lambda i, j: (0, 0, 0)),
                  pl.BlockSpec(memory_space=pltpu.SMEM)],
        out_specs=pl.BlockSpec((ATT_QBLOCK, qw), lambda i, j: (row(i, j), 0)),
        compiler_params=_cparams("parallel", "parallel"),
        name="attn_core",
    )(proj, proj, proj, proj, proj, proj, proj, bias_tbl, sink.astype(F32))


GDN_GATE_LANES = V7X_LANES
GDN_CONV_PAD = 8
GDN_PREP_GROUP = 16
GDN_INPROJ_COLS = 512
GDN_INPROJ_SLAB = 256
GDN_HEADS_PER_STEP = 2
GDN_PREP_MATMULS = 10


def _softplus(x):
    return jnp.maximum(x, 0.0) + jnp.log(1.0 + jnp.exp(-jnp.abs(x)))


def _gdn_gates_kernel(x_ref, w_ref, alog_ref, dtb_ref, o_ref, rows_ref):
    s = x_ref.shape[0]
    c = GDN_CHUNK
    gl = _dot(x_ref[...].astype(BF16), w_ref[...])
    decay = -jnp.exp(alog_ref[...]) * _softplus(gl + dtb_ref[...])
    beta = jax.nn.sigmoid(gl)
    r = lax.broadcasted_iota(I32, (c, c), 0)
    q = lax.broadcasted_iota(I32, (c, c), 1)
    lower = jnp.where(r >= q, 1.0, 0.0)
    upper = jnp.where(r <= q, 1.0, 0.0)
    lane = lax.broadcasted_iota(I32, (c, GDN_GATE_LANES), 1)
    hh = GDN_HEADS
    eye = jnp.where(lax.broadcasted_iota(I32, (4 * hh, GDN_GATE_LANES), 0)
                    == lax.broadcasted_iota(I32, (4 * hh, GDN_GATE_LANES), 1), 1.0, 0.0)
    for n in range(s // c):
        d_c = decay[n * c:(n + 1) * c]
        pre = _dot(lower, d_c, precision=HIGHEST)
        suf = _dot(upper, d_c, precision=HIGHEST)
        tile = jnp.where(lane < hh, pre, jnp.where(lane < 2 * hh, suf, beta[n * c:(n + 1) * c]))
        o_ref[pl.ds(n * c, c), :] = tile
        t_rows = _dot_nt(eye, tile, precision=HIGHEST)
        g_rows = jnp.concatenate([t_rows[0:hh], t_rows[hh:2 * hh]], axis=1)
        b_rows = jnp.concatenate([t_rows[2 * hh:3 * hh], t_rows[3 * hh:4 * hh]], axis=1)
        rows_ref[n] = jnp.concatenate([g_rows, b_rows, b_rows * jnp.exp(g_rows)], axis=1)


def _gdn_gates(xf, w_gate_bf16, alog_vec, dtb_vec, b, s):
    d = xf.shape[1]
    n_chunks = s // GDN_CHUNK
    full = lambda shape: pl.BlockSpec(shape, lambda i: (0,) * len(shape))
    return pl.pallas_call(
        _gdn_gates_kernel,
        out_shape=(jax.ShapeDtypeStruct((b * s, GDN_GATE_LANES), F32),
                   jax.ShapeDtypeStruct((b * n_chunks, GDN_HEADS, 3 * GDN_GATE_LANES), F32)),
        grid=(b,),
        in_specs=[pl.BlockSpec((s, d), lambda i: (i, 0)), full((d, GDN_GATE_LANES)),
                  full((1, GDN_GATE_LANES)), full((1, GDN_GATE_LANES))],
        out_specs=(pl.BlockSpec((s, GDN_GATE_LANES), lambda i: (i, 0)),
                   pl.BlockSpec((n_chunks, GDN_HEADS, 3 * GDN_GATE_LANES), lambda i: (i, 0, 0))),
        compiler_params=_cparams("parallel"),
        name="gdn_gates",
    )(xf, w_gate_bf16, alog_vec, dtb_vec)


def _block_diag2(r, is_b):
    return jnp.concatenate([jnp.where(is_b, 0.0, r), jnp.where(is_b, r, 0.0)], axis=0)


def _gdn_inproj_kernel(x_ref, w_ref, cw_ref, o_ref, pad_ref):
    s = x_ref.shape[0]
    n = w_ref.shape[1]
    dk = GDN_HEAD_DIM
    kind = pl.program_id(1) // (GDN_WIDTH // n)
    slab = min(s, GDN_INPROJ_SLAB)
    n_slabs = s // slab
    project = lambda k: _dot(x_ref[pl.ds(k * slab, slab), :].astype(BF16), w_ref[...])

    @pl.when(kind == 3)
    def _():
        for k in range(n_slabs):
            o_ref[pl.ds(k * slab, slab), :] = project(k).astype(o_ref.dtype)

    @pl.when(kind < 3)
    def _():
        norm_on = kind < 2
        q_scale = jnp.where(kind == 0, GDN_HEAD_DIM ** -0.5, 1.0)
        zeros = jnp.zeros((GDN_CONV_PAD, n), F32)
        pad_ref[pl.ds(0, GDN_CONV_PAD), :] = zeros
        pad_ref[pl.ds(GDN_CONV_PAD + s, GDN_CONV_PAD), :] = zeros
        cw = cw_ref[...]

        def finish(k):
            r0 = k * slab
            acc = None
            for j in range(GDN_CONV):
                term = pad_ref[pl.ds(GDN_CONV_PAD + r0 + j - GDN_CONV // 2, slab), :] * cw[j:j + 1, :]
                acc = term if acc is None else acc + term
            act = _silu(acc)
            outs = []
            for hd in range(n // dk):
                a = act[:, hd * dk:(hd + 1) * dk]
                inv = lax.rsqrt(jnp.sum(a * a, axis=-1, keepdims=True) + 1e-6) * q_scale
                outs.append(a * jnp.where(norm_on, inv, 1.0))
            o_ref[pl.ds(r0, slab), :] = jnp.concatenate(outs, axis=1).astype(o_ref.dtype)

        for k in range(n_slabs):
            pad_ref[pl.ds(GDN_CONV_PAD + k * slab, slab), :] = project(k)
            if k >= 1:
                finish(k - 1)
        finish(n_slabs - 1)


def _gdn_inproj(xf, w_main_bf16, conv_w, b, s):
    d = xf.shape[1]
    n = GDN_INPROJ_COLS
    n_conv_blocks = 3 * GDN_WIDTH // n
    return pl.pallas_call(
        _gdn_inproj_kernel,
        out_shape=jax.ShapeDtypeStruct((b * s, 4 * GDN_WIDTH), BF16),
        grid=(b, 4 * GDN_WIDTH // n),
        in_specs=[pl.BlockSpec((s, d), lambda i, j: (i, 0)),
                  pl.BlockSpec((d, n), lambda i, j: (0, j)),
                  pl.BlockSpec((GDN_CONV, n), lambda i, j: (0, jnp.minimum(j, n_conv_blocks - 1)))],
        out_specs=pl.BlockSpec((s, n), lambda i, j: (i, j)),
        scratch_shapes=[pltpu.VMEM((s + 2 * GDN_CONV_PAD, n), F32)],
        compiler_params=_cparams("parallel", "parallel"),
        name="gdn_inproj",
    )(xf, w_main_bf16, conv_w)


def _gdn_core_kernel(q_ref, k_ref, v_ref, z_ref, gt_ref, rows_ref, nw_ref, o_ref,
                     u_ref, w_ref, qd_ref, kd_ref, aqk_ref, dend_ref, of_ref, ob_ref):
    s = q_ref.shape[0]
    dk = GDN_HEAD_DIM
    hps = q_ref.shape[1] // dk
    c = GDN_CHUNK
    n_chunks = s // c
    first_head = pl.program_id(1) * hps
    slab = min(s, 256)
    head_cols = [slice(hd * dk, (hd + 1) * dk) for hd in range(hps)]

    lane = lax.broadcasted_iota(I32, (c, 2 * c), 1)
    row = lax.broadcasted_iota(I32, (c, 2 * c), 0)
    is_b = lane >= c
    col = jnp.where(is_b, lane - c, lane)
    ahead = jnp.where(is_b, col - row, row - col)
    incl = ahead >= 0
    strict = ahead > 0
    eye2 = jnp.where(row == col, 1.0, 0.0)
    shifts = [lax.rem(2 * V7X_LANES - first_head - hd, V7X_LANES) for hd in range(hps)]

    group = min(GDN_PREP_GROUP, n_chunks)
    n_groups = n_chunks // group

    def pair_rows(t):
        if isinstance(t, int):
            return pl.ds(t * c, c), pl.ds((n_chunks - 1 - t) * c, c)
        return pl.ds(pl.multiple_of(t * c, c), c), pl.ds(pl.multiple_of((n_chunks - 1 - t) * c, c), c)

    def prep_stages(hd, pg, slot):
        rows = [pair_rows(pg * group + i) for i in range(group)]
        cols = head_cols[hd]
        q_f, k_f, v_f = ([ref[r[0], cols].astype(F32) for r in rows] for ref in (q_ref, k_ref, v_ref))
        q_b, k_b, v_b = ([ref[r[1], cols].astype(F32) for r in rows] for ref in (q_ref, k_ref, v_ref))
        gt_f = [pltpu.roll(gt_ref[r[0], :], shifts[hd], axis=1) for r in rows]
        gt_b = [pltpu.roll(gt_ref[r[1], :], shifts[hd], axis=1) for r in rows]
        g_f = [t[:, 0:1] for t in gt_f]
        b_f = [t[:, 2 * GDN_HEADS:2 * GDN_HEADS + 1] for t in gt_f]
        g_b = [t[:, GDN_HEADS:GDN_HEADS + 1] for t in gt_b]
        b_b = [t[:, 3 * GDN_HEADS:3 * GDN_HEADS + 1] for t in gt_b]
        head = pl.ds(first_head + hd, 1)
        g_row, beta_row, be_row = [], [], []
        for i in range(group):
            t = pg * group + i
            row_f, row_b = rows_ref[t, head, :], rows_ref[n_chunks - 1 - t, head, :]
            pick = lambda j: jnp.where(is_b[0:1], row_b[:, j * 2 * c:(j + 1) * 2 * c], row_f[:, j * 2 * c:(j + 1) * 2 * c])
            g_row.append(pick(0))
            beta_row.append(pick(1))
            be_row.append(pick(2))
        dmat = [jnp.where(is_b, g_b[i], g_f[i]) - g_row[i] for i in range(group)]
        prod_f, prod_b = [], []
        for i in range(group):
            kfb, kbb = k_f[i].astype(BF16), k_b[i].astype(BF16)
            keys = jnp.concatenate([kfb, kbb], axis=0)
            prod_f.append(_dot_nt(jnp.concatenate([q_f[i].astype(BF16), kfb], axis=0), keys))
            yield
            prod_b.append(_dot_nt(jnp.concatenate([q_b[i].astype(BF16), kbb], axis=0), keys))
            yield
        qk2 = [jnp.where(is_b, prod_b[i][:c], prod_f[i][:c]) for i in range(group)]
        kk2 = [jnp.where(is_b, prod_b[i][c:], prod_f[i][c:]) for i in range(group)]
        gamma = [jnp.exp(jnp.where(incl, d_i, NEG_BIG)) for d_i in dmat]
        m = [jnp.where(strict, kk2[i] * gamma[i], 0.0) * jnp.where(is_b, b_b[i], b_f[i]) for i in range(group)]
        x_inv = [eye2 - m_i for m_i in m]
        p = []
        for m_i in m:
            p.append(_dot(m_i.astype(BF16), _block_diag2(m_i, is_b).astype(BF16)))
            yield
        for _ in range(5):
            y = []
            for i in range(group):
                y.append(_dot(jnp.concatenate([x_inv[i], p[i]], axis=0).astype(BF16),
                              _block_diag2(p[i], is_b).astype(BF16)))
                yield
            x_inv = [x_inv[i] + y[i][:c] for i in range(group)]
            p = [y_i[c:] for y_i in y]
        eg_f = [jnp.exp(g) for g in g_f]
        eg_b = [jnp.exp(g) for g in g_b]
        u_sol, w_sol = [], []
        for i in range(group):
            u_sol.append(_dot(_block_diag2(x_inv[i] * beta_row[i], is_b).astype(BF16),
                              jnp.concatenate([v_f[i], v_b[i]], axis=0).astype(BF16)))
            yield
            w_sol.append(_dot(_block_diag2(x_inv[i] * be_row[i], is_b).astype(BF16),
                              jnp.concatenate([k_f[i], k_b[i]], axis=0).astype(BF16)))
            yield
        for i in range(group):
            local = pl.ds(i * c, c)
            gl_f, gl_b = g_f[i][c - 1:c, :], g_b[i][0:1, :]
            for d, (qc, kc, eg, gl, g_d) in enumerate(((q_f[i], k_f[i], eg_f[i], gl_f, g_f[i]),
                                                       (q_b[i], k_b[i], eg_b[i], gl_b, g_b[i]))):
                u_ref[hd, slot, d, local, :] = u_sol[i][d * c:(d + 1) * c]
                w_ref[hd, slot, d, local, :] = w_sol[i][d * c:(d + 1) * c]
                qd_ref[hd, slot, d, local, :] = qc * eg
                kd_ref[hd, slot, d, local, :] = kc * jnp.exp(gl - g_d)
            aqk_ref[hd, slot, local, :] = jnp.where(incl, qk2[i] * gamma[i], 0.0)
            dend_ref[hd, slot, pl.ds(i * V7X_SUBLANES, V7X_SUBLANES), :] = jnp.where(
                is_b[:V7X_SUBLANES], jnp.exp(gl_b), jnp.exp(gl_f))

    lanes = (slice(0, c), slice(c, 2 * c))
    chains = [(hd, d) for hd in range(hps) for d in range(2)]

    def run_group(pg, slot, prep_gens, states):
        per_gap = -(-GDN_PREP_MATMULS // 2)

        def advance():
            for _ in range(per_gap):
                for gen in prep_gens:
                    next(gen, None)

        for i in range(group):
            rows = pair_rows(pg * group + i)
            local = pl.ds(i * c, c)
            stb = [st.astype(BF16) for st in states]
            ws = [_dot(w_ref[hd, slot, d, local, :].astype(BF16), stb[n]) for n, (hd, d) in enumerate(chains)]
            qs = [_dot(qd_ref[hd, slot, d, local, :].astype(BF16), stb[n]) for n, (hd, d) in enumerate(chains)]
            advance()
            vnb = [(u_ref[hd, slot, d, local, :] - ws[n]).astype(BF16) for n, (hd, d) in enumerate(chains)]
            av = [_dot(aqk_ref[hd, slot, local, :][:, lanes[d]].astype(BF16), vnb[n])
                  for n, (hd, d) in enumerate(chains)]
            kv = [_dot_tn(kd_ref[hd, slot, d, local, :].astype(BF16), vnb[n]) for n, (hd, d) in enumerate(chains)]
            advance()
            new_states = []
            for n, (hd, d) in enumerate(chains):
                (of_ref, ob_ref)[d][hd, rows[d], :] = qs[n] + av[n]
                dend = dend_ref[hd, slot, pl.ds(i * V7X_SUBLANES, 1), :][:, d * c:d * c + 1]
                new_states.append(states[n] * dend + kv[n])
            states = tuple(new_states)
        for gen in prep_gens:
            for _ in gen:
                pass
        return states

    for hd in range(hps):
        for _ in prep_stages(hd, 0, 0):
            pass

    def fused(tg, states):
        slot = lax.rem(tg, 2)
        return run_group(tg, slot, [prep_stages(hd, tg + 1, 1 - slot) for hd in range(hps)], states)

    zero_state = jnp.zeros((dk, dk), F32)
    states = lax.fori_loop(0, n_groups - 1, fused, (zero_state,) * len(chains))
    run_group(n_groups - 1, (n_groups - 1) % 2, [], states)

    for hd in range(hps):
        cols = slice(hd * dk, (hd + 1) * dk)
        for r0 in range(0, s, slab):
            rs = pl.ds(r0, slab)
            o = of_ref[hd, rs, :] + ob_ref[hd, rs, :]
            o = o * lax.rsqrt(jnp.mean(o * o, axis=-1, keepdims=True) + 1e-6) * nw_ref[...]
            o_ref[rs, cols] = (o * _silu(z_ref[rs, cols].astype(F32))).astype(o_ref.dtype)


def _gdn_core(proj, gates, gate_rows, norm_w, b, s):
    dk = GDN_HEAD_DIM
    hps = GDN_HEADS_PER_STEP
    nhb = GDN_HEADS // hps
    blk = lambda kind: pl.BlockSpec((s, hps * dk), lambda i, h: (i, kind * nhb + h))
    group_rows = min(GDN_PREP_GROUP, s // GDN_CHUNK) * GDN_CHUNK
    seq = lambda: pltpu.VMEM((hps, s, dk), F32)
    slots = lambda: pltpu.VMEM((hps, 2, 2, group_rows, dk), F32)
    return pl.pallas_call(
        _gdn_core_kernel,
        out_shape=jax.ShapeDtypeStruct((b * s, GDN_WIDTH), BF16),
        grid=(b, nhb),
        in_specs=[blk(0), blk(1), blk(2), blk(3),
                  pl.BlockSpec((s, GDN_GATE_LANES), lambda i, h: (i, 0)),
                  pl.BlockSpec((s // GDN_CHUNK, GDN_HEADS, 3 * GDN_GATE_LANES), lambda i, h: (i, 0, 0)),
                  pl.BlockSpec((1, dk), lambda i, h: (0, 0))],
        out_specs=pl.BlockSpec((s, hps * dk), lambda i, h: (i, h)),
        scratch_shapes=[slots(), slots(), slots(), slots(),
                        pltpu.VMEM((hps, 2, group_rows, 2 * GDN_CHUNK), F32),
                        pltpu.VMEM((hps, 2, group_rows // GDN_CHUNK * V7X_SUBLANES, 2 * GDN_CHUNK), F32),
                        seq(), seq()],
        compiler_params=_cparams("parallel", "parallel"),
        name="gdn_core",
    )(proj, proj, proj, proj, gates, gate_rows, norm_w.reshape(1, dk))


def _gdn_mixer_core(xf, w_in, conv_w, a_log, dt_bias, norm_w, b, s):
    n_main = 4 * GDN_WIDTH
    n_gate = 4 * GDN_HEADS
    proj = _gdn_inproj(xf, w_in[:, :n_main].astype(BF16), conv_w.astype(F32), b, s)
    w_gate = jnp.pad(w_in[:, n_main:], ((0, 0), (0, GDN_GATE_LANES - n_gate))).astype(BF16)
    lane_pad = lambda v: jnp.pad(v.astype(F32).reshape(1, 2 * GDN_HEADS), ((0, 0), (0, GDN_GATE_LANES - 2 * GDN_HEADS)))
    gates, gate_rows = _gdn_gates(xf, w_gate, lane_pad(a_log), lane_pad(dt_bias), b, s)
    return _gdn_core(proj, gates, gate_rows, norm_w.astype(F32), b, s)


def kernel(x, a_w_in, a_conv, a_A_log, a_dt_bias, a_norm_w, a_w_out, b_w_in, b_sink, b_w_out, rel_bias, c_w_in, c_w_group, c_scale, c_w_out, router_w, router_b, moe_w_gate, moe_w_up, moe_w_down, ln_g, ln_b):
    b, s, d = x.shape
    t = b * s
    xf = x.reshape(t, d)
    xb = xf
    router_wt = router_w.T.astype(BF16)
    router_b_col = router_b.reshape(N_EXPERTS, 1).astype(F32)
    bias_tbl = _attn_bias_table(rel_bias)
    for i in range(DEPTH):
        kind, j = i % N_MIXERS, i // N_MIXERS
        if kind == 0:
            h = _gdn_mixer_core(xb, a_w_in[j], a_conv[j], a_A_log[j], a_dt_bias[j], a_norm_w[j], b, s)
            w_out = a_w_out[j]
        elif kind == 1:
            proj = _linear(xb, b_w_in[j].astype(BF16), BF16)
            h = _attn_core(proj, bias_tbl, b_sink[j], b, s)
            w_out = b_w_out[j]
        else:
            u = _linear(xb, c_w_in[j].astype(BF16), F32)
            h = _pool_core(u.reshape(b, s, d), c_w_group[j].astype(BF16), c_scale[j]).reshape(t, d)
            w_out = c_w_out[j]
        x1w, route_i, counts = _post_mixer(h, w_out.astype(BF16), xf, ln_g[i, 0], ln_b[i, 0], router_wt, router_b_col)
        xf, xb = _moe_layer(x1w, route_i, counts[:, 0], moe_w_gate, moe_w_up, moe_w_down, i, ln_g[i, 1], ln_b[i, 1])
    return xf.reshape(b, s, d)
```

```python
import functools
import math

import jax
import jax.numpy as jnp
from jax import lax
from jax.experimental import pallas as pl
from jax.experimental.pallas import tpu as pltpu

F32 = jnp.float32
BF16 = jnp.bfloat16
I32 = jnp.int32

D_MODEL = 1024
DEPTH = 4
N_MIXERS = 3
DEEPNORM_ALPHA = (2.0 * DEPTH) ** 0.25
LN_EPS = 1e-5

GDN_HEADS = 8
GDN_HEAD_DIM = 128
GDN_WIDTH = GDN_HEADS * GDN_HEAD_DIM
GDN_CONV = 5
GDN_CHUNK = 64

ATT_HEADS = 16
ATT_KV_HEADS = 4
ATT_HEAD_DIM = 64
ATT_GROUP = ATT_HEADS // ATT_KV_HEADS
ATT_WINDOW = 128
ATT_QBLOCK = 128
ATT_KSPAN = ATT_QBLOCK + 2 * ATT_WINDOW
REL_BUCKETS = 32
REL_MAX_DIST = 128

POOL_WINDOWS = (2, 4, 8, 16)
POOL_GROUP_DIM = D_MODEL // len(POOL_WINDOWS)

N_EXPERTS = 16
N_EXPERT_GROUPS = 4
EXPERTS_PER_GROUP = N_EXPERTS // N_EXPERT_GROUPS
TOP_K = 2
EXPERT_DIM = 512
PAIRS_PER_GROUP = EXPERTS_PER_GROUP * (EXPERTS_PER_GROUP - 1) // 2
N_SEGMENTS = N_EXPERT_GROUPS * PAIRS_PER_GROUP

V7X_LANES = 128
V7X_SUBLANES = 8
V7X_VMEM_LIMIT_BYTES = 56 * 1024 * 1024

ROW_TILE = 512
MOE_BLOCK = 256
DMA_PRIORITIES = 2
ROW_EXTRA = 128
NEG_BIG = -1e30

HIGHEST = lax.Precision.HIGHEST


def _cparams(*sem):
    return pltpu.CompilerParams(dimension_semantics=tuple(sem), vmem_limit_bytes=V7X_VMEM_LIMIT_BYTES)


def _dot(a, b, **kw):
    return jnp.dot(a, b, preferred_element_type=F32, **kw)


def _dot_nt(a, b, **kw):
    return lax.dot_general(a, b, (((1,), (1,)), ((), ())), preferred_element_type=F32, **kw)


def _dot_tn(a, b, **kw):
    return lax.dot_general(a, b, (((0,), (0,)), ((), ())), preferred_element_type=F32, **kw)


def _silu(x):
    return x * jax.nn.sigmoid(x)


def _linear_kernel(x_ref, w_ref, o_ref):
    o_ref[...] = _dot(x_ref[...].astype(BF16), w_ref[...]).astype(o_ref.dtype)


def _linear(x, w_bf16, out_dtype, tm=ROW_TILE):
    m, k = x.shape
    n = w_bf16.shape[1]
    return pl.pallas_call(
        _linear_kernel,
        out_shape=jax.ShapeDtypeStruct((m, n), out_dtype),
        grid=(m // tm,),
        in_specs=[pl.BlockSpec((tm, k), lambda i: (i, 0)), pl.BlockSpec((k, n), lambda i: (0, 0))],
        out_specs=pl.BlockSpec((tm, n), lambda i: (i, 0)),
        compiler_params=_cparams("parallel"),
        name="linear",
    )(x, w_bf16)


def _layer_norm_rows(s, g, b):
    mu = jnp.mean(s, axis=-1, keepdims=True)
    xc = s - mu
    var = jnp.mean(xc * xc, axis=-1, keepdims=True)
    return xc * lax.rsqrt(var + LN_EPS) * g + b


def _top2_of4(a, b, c, d):
    hi1, lo1 = jnp.maximum(a, b), jnp.minimum(a, b)
    hi2, lo2 = jnp.maximum(c, d), jnp.minimum(c, d)
    return jnp.maximum(hi1, hi2) + jnp.maximum(jnp.minimum(hi1, hi2), jnp.maximum(lo1, lo2))


def _route_rows(scores, biased):
    bi = [biased[e:e + 1, :] for e in range(N_EXPERTS)]
    sc = [scores[e:e + 1, :] for e in range(N_EXPERTS)]
    gs = [_top2_of4(*bi[4 * g:4 * g + 4]) for g in range(N_EXPERT_GROUPS)]
    group = jnp.zeros_like(gs[0], dtype=I32)
    best = gs[0]
    for g in range(1, N_EXPERT_GROUPS):
        upd = gs[g] > best
        group = jnp.where(upd, g, group)
        best = jnp.where(upd, gs[g], best)

    def pick(rows, j):
        out = rows[j]
        for g in range(1, N_EXPERT_GROUPS):
            out = jnp.where(group == g, rows[4 * g + j], out)
        return out

    v = [pick(bi, j) for j in range(EXPERTS_PER_GROUP)]
    s = [pick(sc, j) for j in range(EXPERTS_PER_GROUP)]
    i1 = jnp.zeros_like(group)
    b1 = v[0]
    for j in range(1, EXPERTS_PER_GROUP):
        upd = v[j] > b1
        i1 = jnp.where(upd, j, i1)
        b1 = jnp.where(upd, v[j], b1)
    i2 = jnp.full_like(group, -1)
    b2 = jnp.full_like(b1, -jnp.inf)
    for j in range(EXPERTS_PER_GROUP):
        upd = (i1 != j) & ((v[j] > b2) | (i2 < 0))
        i2 = jnp.where(upd, j, i2)
        b2 = jnp.where(upd, v[j], b2)
    s1 = s[0]
    s2 = s[0]
    for j in range(1, EXPERTS_PER_GROUP):
        s1 = jnp.where(i1 == j, s[j], s1)
        s2 = jnp.where(i2 == j, s[j], s2)
    den = s1 + s2
    first_is_lo = i1 < i2
    lo = jnp.where(first_is_lo, i1, i2)
    hi = jnp.where(first_is_lo, i2, i1)
    pair = lax.shift_right_logical(lo * (7 - lo), 1) + hi - lo - 1
    g1, g2 = s1 / den, s2 / den
    return group * PAIRS_PER_GROUP + pair, jnp.where(first_is_lo, g1, g2), jnp.where(first_is_lo, g2, g1)


def _post_mixer_kernel(h_ref, w_ref, x_ref, g_ref, b_ref, rwt_ref, rb_ref, ustrict_ref, eye_ref,
                       x1w_ref, ri_ref, cnt_ref, base_ref):
    tm, d = x_ref.shape

    @pl.when(pl.program_id(0) == 0)
    def _():
        base_ref[...] = jnp.zeros_like(base_ref)

    y = _dot(h_ref[...], w_ref[...])
    x1 = _layer_norm_rows(DEEPNORM_ALPHA * x_ref[...] + y, g_ref[...], b_ref[...])
    x1w_ref[:, 0:d] = x1

    logits = _dot_nt(rwt_ref[...], x1.astype(BF16))
    scores = jax.nn.sigmoid(logits)
    seg, g_lo, g_hi = _route_rows(scores, scores + rb_ref[...])

    sidx = lax.broadcasted_iota(I32, (N_SEGMENTS, tm), 0)
    hit = sidx == seg
    onehot = jnp.where(hit, 1.0, 0.0)
    base = base_ref[:, 0:1]
    before = _dot(onehot.astype(BF16), ustrict_ref[...]) + base
    rank = jnp.sum(jnp.where(hit, before, 0.0), axis=0, keepdims=True)
    new_base = base + jnp.sum(onehot, axis=1, keepdims=True)
    base_ref[...] = jnp.broadcast_to(new_base, base_ref.shape)
    cnt_ref[...] = jnp.broadcast_to(new_base, cnt_ref.shape)

    ri_ref[...] = jnp.concatenate([seg, rank.astype(I32), jnp.zeros((6, tm), I32)], axis=0)
    gates = jnp.concatenate([g_lo, g_hi, jnp.zeros((ROW_EXTRA - 2, tm), F32)], axis=0)
    hi = gates.astype(BF16)
    lo = (gates - hi.astype(F32)).astype(BF16)
    x1w_ref[:, d:d + ROW_EXTRA] = _dot_nt(eye_ref[...], hi) + _dot_nt(eye_ref[...], lo)


def _post_mixer(h_bf16, w_out_bf16, x, ln_g, ln_b, router_wt, router_b_col, tm=ROW_TILE):
    t, d = x.shape
    ustrict = (jnp.arange(tm)[:, None] < jnp.arange(tm)[None, :]).astype(BF16)
    eye = jnp.eye(tm, dtype=BF16)
    full = lambda shape: pl.BlockSpec(shape, lambda i: (0,) * len(shape))
    return pl.pallas_call(
        _post_mixer_kernel,
        out_shape=(jax.ShapeDtypeStruct((t, d + ROW_EXTRA), F32),
                   jax.ShapeDtypeStruct((8, t), I32),
                   jax.ShapeDtypeStruct((N_SEGMENTS, V7X_LANES), F32)),
        grid=(t // tm,),
        in_specs=[pl.BlockSpec((tm, h_bf16.shape[1]), lambda i: (i, 0)),
                  full(w_out_bf16.shape),
                  pl.BlockSpec((tm, d), lambda i: (i, 0)),
                  full((1, d)), full((1, d)),
                  full((N_EXPERTS, d)), full((N_EXPERTS, 1)),
                  full((tm, tm)), full((tm, tm))],
        out_specs=(pl.BlockSpec((tm, d + ROW_EXTRA), lambda i: (i, 0)),
                   pl.BlockSpec((8, tm), lambda i: (0, i)),
                   pl.BlockSpec((N_SEGMENTS, V7X_LANES), lambda i: (0, 0))),
        scratch_shapes=[pltpu.VMEM((N_SEGMENTS, V7X_LANES), F32)],
        compiler_params=_cparams("arbitrary"),
        name="post_mixer_router",
    )(h_bf16, w_out_bf16, x, ln_g.reshape(1, d), ln_b.reshape(1, d), router_wt, router_b_col, ustrict, eye)


def _row_copy(src_ref, src_row, dst_ref, dst_row, sem):
    return pltpu.make_async_copy(src_ref.at[pl.ds(src_row, 1), :], dst_ref.at[pl.ds(dst_row, 1), :], sem)


def _dispatch_kernel(n_tiles, pad_lo_ref, pad_hi_ref, x_ref, dest_hbm, xs_hbm,
                     idx_smem, zero_ref, idx_sem, row_sem, pad_sem):
    tm = x_ref.shape[0]
    i = pl.program_id(0)
    slot = lax.rem(i, 2)

    def idx_copy(tile, sl):
        return pltpu.make_async_copy(dest_hbm.at[tile], idx_smem.at[sl], idx_sem.at[sl])

    @pl.when(i == 0)
    def _():
        idx_copy(0, 0).start()

    idx_copy(i, slot).wait()

    @pl.when(i + 1 < n_tiles)
    def _():
        idx_copy(i + 1, 1 - slot).start()

    def issue(jj, carry):
        for prio in range(DMA_PRIORITIES):
            j = jj * DMA_PRIORITIES + prio
            _row_copy(x_ref, j, xs_hbm, idx_smem[slot, 0, j], row_sem).start(priority=prio)
        return carry

    lax.fori_loop(0, tm // DMA_PRIORITIES, issue, 0, unroll=4)

    def drain(j, carry):
        _row_copy(x_ref, j, xs_hbm, idx_smem[slot, 0, j], row_sem).wait()
        return carry

    lax.fori_loop(0, tm, drain, 0, unroll=8)

    @pl.when(i == pl.num_programs(0) - 1)
    def _():
        zero_ref[...] = jnp.zeros_like(zero_ref)
        for e in range(N_SEGMENTS + 1):
            def fill(r, carry):
                _row_copy(zero_ref, 0, xs_hbm, r, pad_sem).start()
                return carry

            lax.fori_loop(pad_lo_ref[e], pad_hi_ref[e], fill, 0)
        for e in range(N_SEGMENTS + 1):
            def filled(r, carry):
                _row_copy(zero_ref, 0, xs_hbm, r, pad_sem).wait()
                return carry

            lax.fori_loop(pad_lo_ref[e], pad_hi_ref[e], filled, 0)


def _dispatch(x1w, dest_tiles, pad_lo, pad_hi, n_rows, tm=ROW_TILE):
    t, d = x1w.shape
    grid_spec = pltpu.PrefetchScalarGridSpec(
        num_scalar_prefetch=2,
        grid=(t // tm,),
        in_specs=[pl.BlockSpec((tm, d), lambda i, lo, hi: (i, 0)),
                  pl.BlockSpec(memory_space=pl.ANY)],
        out_specs=pl.BlockSpec(memory_space=pl.ANY),
        scratch_shapes=[pltpu.SMEM((2, 1, tm), I32), pltpu.VMEM((V7X_SUBLANES, d), F32),
                        pltpu.SemaphoreType.DMA((2,)), pltpu.SemaphoreType.DMA, pltpu.SemaphoreType.DMA],
    )
    return pl.pallas_call(
        functools.partial(_dispatch_kernel, t // tm),
        out_shape=jax.ShapeDtypeStruct((n_rows, d), F32),
        grid_spec=grid_spec,
        compiler_params=_cparams("arbitrary"),
        name="moe_dispatch",
    )(pad_lo, pad_hi, x1w, dest_tiles)


def _expert_kernel(ea_ref, eb_ref, xs_ref, wga_ref, wua_ref, wda_ref, wgb_ref, wub_ref, wdb_ref, ys_ref):
    del ea_ref, eb_ref
    d = ys_ref.shape[1]
    x = xs_ref[:, 0:d].astype(BF16)
    extra = xs_ref[:, d:d + ROW_EXTRA]
    y = None
    for col, (wg_ref, wu_ref, wd_ref) in enumerate(((wga_ref, wua_ref, wda_ref), (wgb_ref, wub_ref, wdb_ref))):
        hidden = _silu(_dot(x, wg_ref[0, 0].astype(BF16))) * _dot(x, wu_ref[0, 0].astype(BF16))
        part = _dot((hidden * extra[:, col:col + 1]).astype(BF16), wd_ref[0, 0].astype(BF16))
        y = part if y is None else y + part
    ys_ref[...] = y


def _experts(block_ea, block_eb, xs, wg, wu, wd, layer):
    n_rows = xs.shape[0]
    d = xs.shape[1] - ROW_EXTRA
    n_blocks = n_rows // MOE_BLOCK
    f = wg.shape[3]
    up = lambda which: pl.BlockSpec((1, 1, d, f), lambda i, ea, eb: (layer, (ea, eb)[which][i], 0, 0))
    down = lambda which: pl.BlockSpec((1, 1, f, d), lambda i, ea, eb: (layer, (ea, eb)[which][i], 0, 0))
    grid_spec = pltpu.PrefetchScalarGridSpec(
        num_scalar_prefetch=2,
        grid=(n_blocks,),
        in_specs=[pl.BlockSpec((MOE_BLOCK, d + ROW_EXTRA), lambda i, ea, eb: (i, 0)),
                  up(0), up(0), down(0), up(1), up(1), down(1)],
        out_specs=pl.BlockSpec((MOE_BLOCK, d), lambda i, ea, eb: (i, 0)),
    )
    return pl.pallas_call(
        _expert_kernel,
        out_shape=jax.ShapeDtypeStruct((n_rows, d), F32),
        grid_spec=grid_spec,
        compiler_params=_cparams("parallel"),
        name="moe_experts",
    )(block_ea, block_eb, xs, wg, wu, wd, wg, wu, wd)


def _combine_kernel(n_tiles, x1w_ref, g_ref, b_ref, dest_hbm, ys_hbm, o_ref, ob_ref,
                    idx_smem, ybuf, idx_sem, row_sem):
    tm, d = o_ref.shape
    i = pl.program_id(0)
    slot = lax.rem(i, 2)
    other = 1 - slot

    def idx_copy(tile, sl):
        return pltpu.make_async_copy(dest_hbm.at[tile], idx_smem.at[sl], idx_sem.at[sl])

    def gather(sl, start):
        def body(jj, carry):
            for prio in range(DMA_PRIORITIES):
                j = jj * DMA_PRIORITIES + prio
                cp = _row_copy(ys_hbm, idx_smem[sl, 0, j], ybuf.at[sl], j, row_sem.at[sl])
                if start:
                    cp.start(priority=prio)
                else:
                    cp.wait()
            return carry

        lax.fori_loop(0, tm // DMA_PRIORITIES, body, 0, unroll=4)

    @pl.when(i == 0)
    def _():
        first = idx_copy(0, 0)
        first.start()
        first.wait()
        gather(0, True)
        if n_tiles > 1:
            idx_copy(1, 1).start()

    @pl.when(i + 1 < n_tiles)
    def _():
        idx_copy(i + 1, other).wait()
        gather(other, True)

    gather(slot, False)

    @pl.when(i + 2 < n_tiles)
    def _():
        idx_copy(i + 2, slot).start()

    x2 = _layer_norm_rows(DEEPNORM_ALPHA * x1w_ref[:, 0:d] + ybuf[slot], g_ref[...], b_ref[...])
    o_ref[...] = x2
    ob_ref[...] = x2.astype(ob_ref.dtype)


def _combine(x1w, ln_g, ln_b, dest_tiles, ys, tm=ROW_TILE):
    t = x1w.shape[0]
    d = ys.shape[1]
    full = lambda shape: pl.BlockSpec(shape, lambda i: (0,) * len(shape))
    return pl.pallas_call(
        functools.partial(_combine_kernel, t // tm),
        out_shape=(jax.ShapeDtypeStruct((t, d), F32), jax.ShapeDtypeStruct((t, d), BF16)),
        grid=(t // tm,),
        in_specs=[pl.BlockSpec((tm, d + ROW_EXTRA), lambda i: (i, 0)),
                  full((1, d)), full((1, d)),
                  pl.BlockSpec(memory_space=pl.ANY),
                  pl.BlockSpec(memory_space=pl.ANY)],
        out_specs=(pl.BlockSpec((tm, d), lambda i: (i, 0)), pl.BlockSpec((tm, d), lambda i: (i, 0))),
        scratch_shapes=[pltpu.SMEM((2, 1, tm), I32), pltpu.VMEM((2, tm, d), F32),
                        pltpu.SemaphoreType.DMA((2,)), pltpu.SemaphoreType.DMA((2,))],
        compiler_params=_cparams("arbitrary"),
        name="moe_combine_ln",
    )(x1w, ln_g.reshape(1, d), ln_b.reshape(1, d), dest_tiles, ys)


def _moe_layer(x1w, route_i, counts, wg, wu, wd, layer, ln_g, ln_b, tm=ROW_TILE):
    t = x1w.shape[0]
    n_tiles = t // tm
    n_rows = t + N_SEGMENTS * MOE_BLOCK
    counts = counts.astype(I32)
    padded = (counts + MOE_BLOCK - 1) // MOE_BLOCK * MOE_BLOCK
    pends = jnp.cumsum(padded)
    pstarts = pends - padded
    seg = route_i[0:1]
    start_of = jnp.sum(jnp.where(seg[None] == jnp.arange(N_SEGMENTS, dtype=I32)[:, None, None],
                                 pstarts[:, None, None], 0), axis=0)
    dest = start_of + route_i[1:2]
    dest_tiles = dest.reshape(1, n_tiles, tm).transpose(1, 0, 2)
    n_blocks = n_rows // MOE_BLOCK
    block_seg = jnp.minimum(
        jnp.sum((jnp.arange(n_blocks, dtype=I32) * MOE_BLOCK)[:, None] >= pends[None, :], axis=-1),
        N_SEGMENTS - 1).astype(I32)
    pair_lo = jnp.array([lo for lo in range(EXPERTS_PER_GROUP) for hi in range(lo + 1, EXPERTS_PER_GROUP)], I32)
    pair_hi = jnp.array([hi for lo in range(EXPERTS_PER_GROUP) for hi in range(lo + 1, EXPERTS_PER_GROUP)], I32)
    block_group, block_pair = block_seg // PAIRS_PER_GROUP, block_seg % PAIRS_PER_GROUP
    block_ea = block_group * EXPERTS_PER_GROUP + pair_lo[block_pair]
    block_eb = block_group * EXPERTS_PER_GROUP + pair_hi[block_pair]
    pad_lo = jnp.concatenate([pstarts + counts, pends[-1:]]).astype(I32)
    pad_hi = jnp.concatenate([pends, jnp.full((1,), n_rows, I32)]).astype(I32)
    xs = _dispatch(x1w, dest_tiles, pad_lo, pad_hi, n_rows, tm)
    ys = _experts(block_ea, block_eb, xs, wg, wu, wd, layer)
    return _combine(x1w, ln_g, ln_b, dest_tiles, ys, tm)


POOL_PAD = 16


def _pool_kernel(u_ref, wg_ref, scale_ref, o_ref, pad_ref):
    s, c = u_ref.shape[1], u_ref.shape[2]
    u = u_ref[0]
    zeros = jnp.zeros((POOL_PAD, c), F32)
    pad_ref[pl.ds(0, POOL_PAD), :] = zeros
    pad_ref[pl.ds(POOL_PAD + s, POOL_PAD), :] = zeros
    pad_ref[pl.ds(POOL_PAD, s), :] = u
    pos = lax.broadcasted_iota(I32, (s, c), 0)
    group = pl.program_id(1)
    for gi, win in enumerate(POOL_WINDOWS):
        @pl.when(group == gi)
        def _(win=win):
            half = win // 2
            total = pad_ref[pl.ds(POOL_PAD - half, s), :]
            for j in range(1 - half, half):
                total = total + pad_ref[pl.ds(POOL_PAD + j, s), :]
            count = (jnp.minimum(pos + half, s) - jnp.maximum(pos - half, 0)).astype(F32)
            mixed = total / count - u
            y = _dot(mixed.astype(BF16), wg_ref[0]) * scale_ref[...]
            o_ref[0] = y.astype(o_ref.dtype)


def _pool_core(u, w_group_bf16, scale):
    b, s, d = u.shape
    c = POOL_GROUP_DIM
    return pl.pallas_call(
        _pool_kernel,
        out_shape=jax.ShapeDtypeStruct((b, s, d), BF16),
        grid=(b, d // c),
        in_specs=[pl.BlockSpec((1, s, c), lambda i, g: (i, 0, g)),
                  pl.BlockSpec((1, c, c), lambda i, g: (g, 0, 0)),
                  pl.BlockSpec((1, c), lambda i, g: (0, g))],
        out_specs=pl.BlockSpec((1, s, c), lambda i, g: (i, 0, g)),
        scratch_shapes=[pltpu.VMEM((s + 2 * POOL_PAD, c), F32)],
        compiler_params=_cparams("parallel", "parallel"),
        name="pool_core",
    )(u, w_group_bf16, scale.reshape(1, d))


def _t5_bucket(rel):
    half = REL_BUCKETS // 2
    max_exact = half // 2
    n = jnp.abs(rel)
    large = max_exact + (jnp.log(jnp.maximum(n, 1).astype(F32) / max_exact)
                         / math.log(REL_MAX_DIST / max_exact) * (half - max_exact)).astype(I32)
    large = jnp.minimum(large, half - 1)
    return (rel > 0).astype(I32) * half + jnp.where(n < max_exact, n, large)


def _attn_bias_table(rel_bias):
    rel = jnp.arange(ATT_KSPAN)[None, :] - ATT_WINDOW - jnp.arange(ATT_QBLOCK)[:, None]
    onehot = (_t5_bucket(rel)[..., None] == jnp.arange(REL_BUCKETS)).astype(F32)
    bias = jnp.einsum("qkb,bh->hqk", onehot, rel_bias.astype(F32), precision=HIGHEST)
    return jnp.where((jnp.abs(rel) <= ATT_WINDOW)[None], bias, NEG_BIG)


def _attn_kernel(q_ref, kp_ref, kc_ref, kn_ref, vp_ref, vc_ref, vn_ref, bias_ref, sink_ref, o_ref):
    j = pl.program_id(1)
    nb = pl.num_programs(1)
    qb = ATT_QBLOCK
    col = lax.broadcasted_iota(I32, (qb, ATT_KSPAN), 1)
    edge = jnp.where(((j == 0) & (col < qb)) | ((j == nb - 1) & (col >= 2 * qb)), NEG_BIG, 0.0)
    k_all = jnp.concatenate([kp_ref[...], kc_ref[...], kn_ref[...]], axis=0)
    v_all = jnp.concatenate([vp_ref[...], vc_ref[...], vn_ref[...]], axis=0)
    q = q_ref[...] * (ATT_HEAD_DIM ** -0.5)
    heads = lambda g: range(g * ATT_GROUP, (g + 1) * ATT_GROUP)
    cols = lambda i: slice(i * ATT_HEAD_DIM, (i + 1) * ATT_HEAD_DIM)

    def scores(g):
        return [_dot_nt(q[:, cols(hd)], k_all[:, cols(g)]) for hd in heads(g)]

    def attend(g, raw):
        outs = []
        probs, dens = [], []
        for hd, qk in zip(heads(g), raw):
            logits = qk + bias_ref[hd] + edge
            sink = sink_ref[hd]
            mx = jnp.maximum(jnp.max(logits, axis=-1, keepdims=True), sink)
            p = jnp.exp(logits - mx)
            dens.append(jnp.sum(p, axis=-1, keepdims=True) + jnp.exp(sink - mx))
            probs.append(p.astype(BF16))
        for p, den in zip(probs, dens):
            outs.append(_dot(p, v_all[:, cols(g)]) / den)
        return outs

    outs = []
    raw = scores(0)
    for g in range(ATT_KV_HEADS):
        nxt = scores(g + 1) if g + 1 < ATT_KV_HEADS else None
        outs += attend(g, raw)
        raw = nxt
    o_ref[...] = jnp.concatenate(outs, axis=-1).astype(o_ref.dtype)


def _attn_core(proj, bias_tbl, sink, b, s):
    qw = ATT_HEADS * ATT_HEAD_DIM
    kw = ATT_KV_HEADS * ATT_HEAD_DIM
    nb = s // ATT_QBLOCK
    kcol, vcol = qw // kw, qw // kw + 1
    row = lambda i, j: i * nb + j
    prev = lambda i, j: i * nb + jnp.maximum(j - 1, 0)
    nxt = lambda i, j: i * nb + jnp.minimum(j + 1, nb - 1)
    kv_spec = lambda rowf, c: pl.BlockSpec((ATT_QBLOCK, kw), lambda i, j: (rowf(i, j), c))
    return pl.pallas_call(
        _attn_kernel,
        out_shape=jax.ShapeDtypeStruct((b * s, qw), BF16),
        grid=(b, nb),
        in_specs=[pl.BlockSpec((ATT_QBLOCK, qw), lambda i, j: (row(i, j), 0)),
                  kv_spec(prev, kcol), kv_spec(row, kcol), kv_spec(nxt, kcol),
                  kv_spec(prev, vcol), kv_spec(row, vcol), kv_spec(nxt, vcol),
                  pl.BlockSpec((ATT_HEADS, ATT_QBLOCK, ATT_KSPAN), lambda i, j: (0, 0, 0)),
                  pl.BlockSpec(memory_space=pltpu.SMEM)],
        out_specs=pl.BlockSpec((ATT_QBLOCK, qw), lambda i, j: (row(i, j), 0)),
        compiler_params=_cparams("parallel", "parallel"),
        name="attn_core",
    )(proj, proj, proj, proj, proj, proj, proj, bias_tbl, sink.astype(F32))


GDN_GATE_LANES = V7X_LANES
GDN_CONV_PAD = 8
GDN_PREP_GROUP = 16
GDN_INPROJ_COLS = 512
GDN_INPROJ_SLAB = 256
GDN_HEADS_PER_STEP = 2
GDN_PREP_MATMULS = 10


def _softplus(x):
    return jnp.maximum(x, 0.0) + jnp.log(1.0 + jnp.exp(-jnp.abs(x)))


def _gdn_gates_kernel(x_ref, w_ref, alog_ref, dtb_ref, o_ref, rows_ref):
    s = x_ref.shape[0]
    c = GDN_CHUNK
    gl = _dot(x_ref[...].astype(BF16), w_ref[...])
    decay = -jnp.exp(alog_ref[...]) * _softplus(gl + dtb_ref[...])
    beta = jax.nn.sigmoid(gl)
    r = lax.broadcasted_iota(I32, (c, c), 0)
    q = lax.broadcasted_iota(I32, (c, c), 1)
    lower = jnp.where(r >= q, 1.0, 0.0)
    upper = jnp.where(r <= q, 1.0, 0.0)
    lane = lax.broadcasted_iota(I32, (c, GDN_GATE_LANES), 1)
    hh = GDN_HEADS
    eye = jnp.where(lax.broadcasted_iota(I32, (4 * hh, GDN_GATE_LANES), 0)
                    == lax.broadcasted_iota(I32, (4 * hh, GDN_GATE_LANES), 1), 1.0, 0.0)
    for n in range(s // c):
        d_c = decay[n * c:(n + 1) * c]
        pre = _dot(lower, d_c, precision=HIGHEST)
        suf = _dot(upper, d_c, precision=HIGHEST)
        tile = jnp.where(lane < hh, pre, jnp.where(lane < 2 * hh, suf, beta[n * c:(n + 1) * c]))
        o_ref[pl.ds(n * c, c), :] = tile
        t_rows = _dot_nt(eye, tile, precision=HIGHEST)
        g_rows = jnp.concatenate([t_rows[0:hh], t_rows[hh:2 * hh]], axis=1)
        b_rows = jnp.concatenate([t_rows[2 * hh:3 * hh], t_rows[3 * hh:4 * hh]], axis=1)
        rows_ref[n] = jnp.concatenate([g_rows, b_rows, b_rows * jnp.exp(g_rows)], axis=1)


def _gdn_gates(xf, w_gate_bf16, alog_vec, dtb_vec, b, s):
    d = xf.shape[1]
    n_chunks = s // GDN_CHUNK
    full = lambda shape: pl.BlockSpec(shape, lambda i: (0,) * len(shape))
    return pl.pallas_call(
        _gdn_gates_kernel,
        out_shape=(jax.ShapeDtypeStruct((b * s, GDN_GATE_LANES), F32),
                   jax.ShapeDtypeStruct((b * n_chunks, GDN_HEADS, 3 * GDN_GATE_LANES), F32)),
        grid=(b,),
        in_specs=[pl.BlockSpec((s, d), lambda i: (i, 0)), full((d, GDN_GATE_LANES)),
                  full((1, GDN_GATE_LANES)), full((1, GDN_GATE_LANES))],
        out_specs=(pl.BlockSpec((s, GDN_GATE_LANES), lambda i: (i, 0)),
                   pl.BlockSpec((n_chunks, GDN_HEADS, 3 * GDN_GATE_LANES), lambda i: (i, 0, 0))),
        compiler_params=_cparams("parallel"),
        name="gdn_gates",
    )(xf, w_gate_bf16, alog_vec, dtb_vec)


def _block_diag2(r, is_b):
    return jnp.concatenate([jnp.where(is_b, 0.0, r), jnp.where(is_b, r, 0.0)], axis=0)


def _gdn_inproj_kernel(x_ref, w_ref, cw_ref, o_ref, pad_ref):
    s = x_ref.shape[0]
    n = w_ref.shape[1]
    dk = GDN_HEAD_DIM
    kind = pl.program_id(1) // (GDN_WIDTH // n)
    slab = min(s, GDN_INPROJ_SLAB)
    n_slabs = s // slab
    project = lambda k: _dot(x_ref[pl.ds(k * slab, slab), :].astype(BF16), w_ref[...])

    @pl.when(kind == 3)
    def _():
        for k in range(n_slabs):
            o_ref[pl.ds(k * slab, slab), :] = project(k).astype(o_ref.dtype)

    @pl.when(kind < 3)
    def _():
        norm_on = kind < 2
        q_scale = jnp.where(kind == 0, GDN_HEAD_DIM ** -0.5, 1.0)
        zeros = jnp.zeros((GDN_CONV_PAD, n), F32)
        pad_ref[pl.ds(0, GDN_CONV_PAD), :] = zeros
        pad_ref[pl.ds(GDN_CONV_PAD + s, GDN_CONV_PAD), :] = zeros
        cw = cw_ref[...]

        def finish(k):
            r0 = k * slab
            acc = None
            for j in range(GDN_CONV):
                term = pad_ref[pl.ds(GDN_CONV_PAD + r0 + j - GDN_CONV // 2, slab), :] * cw[j:j + 1, :]
                acc = term if acc is None else acc + term
            act = _silu(acc)
            outs = []
            for hd in range(n // dk):
                a = act[:, hd * dk:(hd + 1) * dk]
                inv = lax.rsqrt(jnp.sum(a * a, axis=-1, keepdims=True) + 1e-6) * q_scale
                outs.append(a * jnp.where(norm_on, inv, 1.0))
            o_ref[pl.ds(r0, slab), :] = jnp.concatenate(outs, axis=1).astype(o_ref.dtype)

        for k in range(n_slabs):
            pad_ref[pl.ds(GDN_CONV_PAD + k * slab, slab), :] = project(k)
            if k >= 1:
                finish(k - 1)
        finish(n_slabs - 1)


def _gdn_inproj(xf, w_main_bf16, conv_w, b, s):
    d = xf.shape[1]
    n = GDN_INPROJ_COLS
    n_conv_blocks = 3 * GDN_WIDTH // n
    return pl.pallas_call(
        _gdn_inproj_kernel,
        out_shape=jax.ShapeDtypeStruct((b * s, 4 * GDN_WIDTH), BF16),
        grid=(b, 4 * GDN_WIDTH // n),
        in_specs=[pl.BlockSpec((s, d), lambda i, j: (i, 0)),
                  pl.BlockSpec((d, n), lambda i, j: (0, j)),
                  pl.BlockSpec((GDN_CONV, n), lambda i, j: (0, jnp.minimum(j, n_conv_blocks - 1)))],
        out_specs=pl.BlockSpec((s, n), lambda i, j: (i, j)),
        scratch_shapes=[pltpu.VMEM((s + 2 * GDN_CONV_PAD, n), F32)],
        compiler_params=_cparams("parallel", "parallel"),
        name="gdn_inproj",
    )(xf, w_main_bf16, conv_w)


def _gdn_core_kernel(q_ref, k_ref, v_ref, z_ref, gt_ref, rows_ref, nw_ref, o_ref,
                     u_ref, w_ref, qd_ref, kd_ref, aqk_ref, dend_ref, of_ref, ob_ref):
    s = q_ref.shape[0]
    dk = GDN_HEAD_DIM
    hps = q_ref.shape[1] // dk
    c = GDN_CHUNK
    n_chunks = s // c
    first_head = pl.program_id(1) * hps
    slab = min(s, 256)
    head_cols = [slice(hd * dk, (hd + 1) * dk) for hd in range(hps)]

    lane = lax.broadcasted_iota(I32, (c, 2 * c), 1)
    row = lax.broadcasted_iota(I32, (c, 2 * c), 0)
    is_b = lane >= c
    col = jnp.where(is_b, lane - c, lane)
    ahead = jnp.where(is_b, col - row, row - col)
    incl = ahead >= 0
    strict = ahead > 0
    eye2 = jnp.where(row == col, 1.0, 0.0)
    shifts = [lax.rem(2 * V7X_LANES - first_head - hd, V7X_LANES) for hd in range(hps)]

    group = min(GDN_PREP_GROUP, n_chunks)
    n_groups = n_chunks // group

    def pair_rows(t):
        if isinstance(t, int):
            return pl.ds(t * c, c), pl.ds((n_chunks - 1 - t) * c, c)
        return pl.ds(pl.multiple_of(t * c, c), c), pl.ds(pl.multiple_of((n_chunks - 1 - t) * c, c), c)

    def prep_stages(hd, pg, slot):
        rows = [pair_rows(pg * group + i) for i in range(group)]
        cols = head_cols[hd]
        q_f, k_f, v_f = ([ref[r[0], cols].astype(F32) for r in rows] for ref in (q_ref, k_ref, v_ref))
        q_b, k_b, v_b = ([ref[r[1], cols].astype(F32) for r in rows] for ref in (q_ref, k_ref, v_ref))
        gt_f = [pltpu.roll(gt_ref[r[0], :], shifts[hd], axis=1) for r in rows]
        gt_b = [pltpu.roll(gt_ref[r[1], :], shifts[hd], axis=1) for r in rows]
        g_f = [t[:, 0:1] for t in gt_f]
        b_f = [t[:, 2 * GDN_HEADS:2 * GDN_HEADS + 1] for t in gt_f]
        g_b = [t[:, GDN_HEADS:GDN_HEADS + 1] for t in gt_b]
        b_b = [t[:, 3 * GDN_HEADS:3 * GDN_HEADS + 1] for t in gt_b]
        head = pl.ds(first_head + hd, 1)
        g_row, beta_row, be_row = [], [], []
        for i in range(group):
            t = pg * group + i
            row_f, row_b = rows_ref[t, head, :], rows_ref[n_chunks - 1 - t, head, :]
            pick = lambda j: jnp.where(is_b[0:1], row_b[:, j * 2 * c:(j + 1) * 2 * c], row_f[:, j * 2 * c:(j + 1) * 2 * c])
            g_row.append(pick(0))
            beta_row.append(pick(1))
            be_row.append(pick(2))
        dmat = [jnp.where(is_b, g_b[i], g_f[i]) - g_row[i] for i in range(group)]
        prod_f, prod_b = [], []
        for i in range(group):
            kfb, kbb = k_f[i].astype(BF16), k_b[i].astype(BF16)
            keys = jnp.concatenate([kfb, kbb], axis=0)
            prod_f.append(_dot_nt(jnp.concatenate([q_f[i].astype(BF16), kfb], axis=0), keys))
            yield
            prod_b.append(_dot_nt(jnp.concatenate([q_b[i].astype(BF16), kbb], axis=0), keys))
            yield
        qk2 = [jnp.where(is_b, prod_b[i][:c], prod_f[i][:c]) for i in range(group)]
        kk2 = [jnp.where(is_b, prod_b[i][c:], prod_f[i][c:]) for i in range(group)]
        gamma = [jnp.exp(jnp.where(incl, d_i, NEG_BIG)) for d_i in dmat]
        m = [jnp.where(strict, kk2[i] * gamma[i], 0.0) * jnp.where(is_b, b_b[i], b_f[i]) for i in range(group)]
        x_inv = [eye2 - m_i for m_i in m]
        p = []
        for m_i in m:
            p.append(_dot(m_i.astype(BF16), _block_diag2(m_i, is_b).astype(BF16)))
            yield
        for _ in range(5):
            y = []
            for i in range(group):
                y.append(_dot(jnp.concatenate([x_inv[i], p[i]], axis=0).astype(BF16),
                              _block_diag2(p[i], is_b).astype(BF16)))
                yield
            x_inv = [x_inv[i] + y[i][:c] for i in range(group)]
            p = [y_i[c:] for y_i in y]
        eg_f = [jnp.exp(g) for g in g_f]
        eg_b = [jnp.exp(g) for g in g_b]
        u_sol, w_sol = [], []
        for i in range(group):
            u_sol.append(_dot(_block_diag2(x_inv[i] * beta_row[i], is_b).astype(BF16),
                              jnp.concatenate([v_f[i], v_b[i]], axis=0).astype(BF16)))
            yield
            w_sol.append(_dot(_block_diag2(x_inv[i] * be_row[i], is_b).astype(BF16),
                              jnp.concatenate([k_f[i], k_b[i]], axis=0).astype(BF16)))
            yield
        for i in range(group):
            local = pl.ds(i * c, c)
            gl_f, gl_b = g_f[i][c - 1:c, :], g_b[i][0:1, :]
            for d, (qc, kc, eg, gl, g_d) in enumerate(((q_f[i], k_f[i], eg_f[i], gl_f, g_f[i]),
                                                       (q_b[i], k_b[i], eg_b[i], gl_b, g_b[i]))):
                u_ref[hd, slot, d, local, :] = u_sol[i][d * c:(d + 1) * c]
                w_ref[hd, slot, d, local, :] = w_sol[i][d * c:(d + 1) * c]
                qd_ref[hd, slot, d, local, :] = qc * eg
                kd_ref[hd, slot, d, local, :] = kc * jnp.exp(gl - g_d)
            aqk_ref[hd, slot, local, :] = jnp.where(incl, qk2[i] * gamma[i], 0.0)
            dend_ref[hd, slot, pl.ds(i * V7X_SUBLANES, V7X_SUBLANES), :] = jnp.where(
                is_b[:V7X_SUBLANES], jnp.exp(gl_b), jnp.exp(gl_f))

    lanes = (slice(0, c), slice(c, 2 * c))
    chains = [(hd, d) for hd in range(hps) for d in range(2)]

    def run_group(pg, slot, prep_gens, states):
        per_gap = -(-GDN_PREP_MATMULS // 2)

        def advance():
            for _ in range(per_gap):
                for gen in prep_gens:
                    next(gen, None)

        for i in range(group):
            rows = pair_rows(pg * group + i)
            local = pl.ds(i * c, c)
            stb = [st.astype(BF16) for st in states]
            ws = [_dot(w_ref[hd, slot, d, local, :].astype(BF16), stb[n]) for n, (hd, d) in enumerate(chains)]
            qs = [_dot(qd_ref[hd, slot, d, local, :].astype(BF16), stb[n]) for n, (hd, d) in enumerate(chains)]
            advance()
            vnb = [(u_ref[hd, slot, d, local, :] - ws[n]).astype(BF16) for n, (hd, d) in enumerate(chains)]
            av = [_dot(aqk_ref[hd, slot, local, :][:, lanes[d]].astype(BF16), vnb[n])
                  for n, (hd, d) in enumerate(chains)]
            kv = [_dot_tn(kd_ref[hd, slot, d, local, :].astype(BF16), vnb[n]) for n, (hd, d) in enumerate(chains)]
            advance()
            new_states = []
            for n, (hd, d) in enumerate(chains):
                (of_ref, ob_ref)[d][hd, rows[d], :] = qs[n] + av[n]
                dend = dend_ref[hd, slot, pl.ds(i * V7X_SUBLANES, 1), :][:, d * c:d * c + 1]
                new_states.append(states[n] * dend + kv[n])
            states = tuple(new_states)
        for gen in prep_gens:
            for _ in gen:
                pass
        return states

    for hd in range(hps):
        for _ in prep_stages(hd, 0, 0):
            pass

    def fused(tg, states):
        slot = lax.rem(tg, 2)
        return run_group(tg, slot, [prep_stages(hd, tg + 1, 1 - slot) for hd in range(hps)], states)

    zero_state = jnp.zeros((dk, dk), F32)
    states = lax.fori_loop(0, n_groups - 1, fused, (zero_state,) * len(chains))
    run_group(n_groups - 1, (n_groups - 1) % 2, [], states)

    for hd in range(hps):
        cols = slice(hd * dk, (hd + 1) * dk)
        for r0 in range(0, s, slab):
            rs = pl.ds(r0, slab)
            o = of_ref[hd, rs, :] + ob_ref[hd, rs, :]
            o = o * lax.rsqrt(jnp.mean(o * o, axis=-1, keepdims=True) + 1e-6) * nw_ref[...]
            o_ref[rs, cols] = (o * _silu(z_ref[rs, cols].astype(F32))).astype(o_ref.dtype)


def _gdn_core(proj, gates, gate_rows, norm_w, b, s):
    dk = GDN_HEAD_DIM
    hps = GDN_HEADS_PER_STEP
    nhb = GDN_HEADS // hps
    blk = lambda kind: pl.BlockSpec((s, hps * dk), lambda i, h: (i, kind * nhb + h))
    group_rows = min(GDN_PREP_GROUP, s // GDN_CHUNK) * GDN_CHUNK
    seq = lambda: pltpu.VMEM((hps, s, dk), F32)
    slots = lambda: pltpu.VMEM((hps, 2, 2, group_rows, dk), F32)
    return pl.pallas_call(
        _gdn_core_kernel,
        out_shape=jax.ShapeDtypeStruct((b * s, GDN_WIDTH), BF16),
        grid=(b, nhb),
        in_specs=[blk(0), blk(1), blk(2), blk(3),
                  pl.BlockSpec((s, GDN_GATE_LANES), lambda i, h: (i, 0)),
                  pl.BlockSpec((s // GDN_CHUNK, GDN_HEADS, 3 * GDN_GATE_LANES), lambda i, h: (i, 0, 0)),
                  pl.BlockSpec((1, dk), lambda i, h: (0, 0))],
        out_specs=pl.BlockSpec((s, hps * dk), lambda i, h: (i, h)),
        scratch_shapes=[slots(), slots(), slots(), slots(),
                        pltpu.VMEM((hps, 2, group_rows, 2 * GDN_CHUNK), F32),
                        pltpu.VMEM((hps, 2, group_rows // GDN_CHUNK * V7X_SUBLANES, 2 * GDN_CHUNK), F32),
                        seq(), seq()],
        compiler_params=_cparams("parallel", "parallel"),
        name="gdn_core",
    )(proj, proj, proj, proj, gates, gate_rows, norm_w.reshape(1, dk))


def _gdn_mixer_core(xf, w_in, conv_w, a_log, dt_bias, norm_w, b, s):
    n_main = 4 * GDN_WIDTH
    n_gate = 4 * GDN_HEADS
    proj = _gdn_inproj(xf, w_in[:, :n_main].astype(BF16), conv_w.astype(F32), b, s)
    w_gate = jnp.pad(w_in[:, n_main:], ((0, 0), (0, GDN_GATE_LANES - n_gate))).astype(BF16)
    lane_pad = lambda v: jnp.pad(v.astype(F32).reshape(1, 2 * GDN_HEADS), ((0, 0), (0, GDN_GATE_LANES - 2 * GDN_HEADS)))
    gates, gate_rows = _gdn_gates(xf, w_gate, lane_pad(a_log), lane_pad(dt_bias), b, s)
    return _gdn_core(proj, gates, gate_rows, norm_w.astype(F32), b, s)


def kernel(x, a_w_in, a_conv, a_A_log, a_dt_bias, a_norm_w, a_w_out, b_w_in, b_sink, b_w_out, rel_bias, c_w_in, c_w_group, c_scale, c_w_out, router_w, router_b, moe_w_gate, moe_w_up, moe_w_down, ln_g, ln_b):
    b, s, d = x.shape
    t = b * s
    xf = x.reshape(t, d)
    xb = xf
    router_wt = router_w.T.astype(BF16)
    router_b_col = router_b.reshape(N_EXPERTS, 1).astype(F32)
    bias_tbl = _attn_bias_table(rel_bias)
    for i in range(DEPTH):
        kind, j = i % N_MIXERS, i // N_MIXERS
        if kind == 0:
            h = _gdn_mixer_core(xb, a_w_in[j], a_conv[j], a_A_log[j], a_dt_bias[j], a_norm_w[j], b, s)
            w_out = a_w_out[j]
        elif kind == 1:
            proj = _linear(xb, b_w_in[j].astype(BF16), BF16)
            h = _attn_core(proj, bias_tbl, b_sink[j], b, s)
            w_out = b_w_out[j]
        else:
            u = _linear(xb, c_w_in[j].astype(BF16), F32)
            h = _pool_core(u.reshape(b, s, d), c_w_group[j].astype(BF16), c_scale[j]).reshape(t, d)
            w_out = c_w_out[j]
        x1w, route_i, counts = _post_mixer(h, w_out.astype(BF16), xf, ln_g[i, 0], ln_b[i, 0], router_wt, router_b_col)
        xf, xb = _moe_layer(x1w, route_i, counts[:, 0], moe_w_gate, moe_w_up, moe_w_down, i, ln_g[i, 1], ln_b[i, 1])
    return xf.reshape(b, s, d)
```

```python
import functools
import math

import jax
import jax.numpy as jnp
from jax import lax
from jax.experimental import pallas as pl
from jax.experimental.pallas import tpu as pltpu

F32 = jnp.float32
BF16 = jnp.bfloat16
I32 = jnp.int32

D_MODEL = 1024
DEPTH = 4
N_MIXERS = 3
DEEPNORM_ALPHA = (2.0 * DEPTH) ** 0.25
LN_EPS = 1e-5

GDN_HEADS = 8
GDN_HEAD_DIM = 128
GDN_WIDTH = GDN_HEADS * GDN_HEAD_DIM
GDN_CONV = 5
GDN_CHUNK = 64

ATT_HEADS = 16
ATT_KV_HEADS = 4
ATT_HEAD_DIM = 64
ATT_GROUP = ATT_HEADS // ATT_KV_HEADS
ATT_WINDOW = 128
ATT_QBLOCK = 128
ATT_KSPAN = ATT_QBLOCK + 2 * ATT_WINDOW
REL_BUCKETS = 32
REL_MAX_DIST = 128

POOL_WINDOWS = (2, 4, 8, 16)
POOL_GROUP_DIM = D_MODEL // len(POOL_WINDOWS)

N_EXPERTS = 16
N_EXPERT_GROUPS = 4
EXPERTS_PER_GROUP = N_EXPERTS // N_EXPERT_GROUPS
TOP_K = 2
EXPERT_DIM = 512
PAIRS_PER_GROUP = EXPERTS_PER_GROUP * (EXPERTS_PER_GROUP - 1) // 2
N_SEGMENTS = N_EXPERT_GROUPS * PAIRS_PER_GROUP

V7X_LANES = 128
V7X_SUBLANES = 8
V7X_VMEM_LIMIT_BYTES = 56 * 1024 * 1024

ROW_TILE = 512
MOE_BLOCK = 256
ROW_EXTRA = 128
NEG_BIG = -1e30

HIGHEST = lax.Precision.HIGHEST


def _cparams(*sem):
    return pltpu.CompilerParams(dimension_semantics=tuple(sem), vmem_limit_bytes=V7X_VMEM_LIMIT_BYTES)


def _dot(a, b, **kw):
    return jnp.dot(a, b, preferred_element_type=F32, **kw)


def _dot_nt(a, b, **kw):
    return lax.dot_general(a, b, (((1,), (1,)), ((), ())), preferred_element_type=F32, **kw)


def _dot_tn(a, b, **kw):
    return lax.dot_general(a, b, (((0,), (0,)), ((), ())), preferred_element_type=F32, **kw)


def _silu(x):
    return x * jax.nn.sigmoid(x)


def _linear_kernel(x_ref, w_ref, o_ref):
    o_ref[...] = _dot(x_ref[...].astype(BF16), w_ref[...]).astype(o_ref.dtype)


def _linear(x, w_bf16, out_dtype, tm=ROW_TILE):
    m, k = x.shape
    n = w_bf16.shape[1]
    return pl.pallas_call(
        _linear_kernel,
        out_shape=jax.ShapeDtypeStruct((m, n), out_dtype),
        grid=(m // tm,),
        in_specs=[pl.BlockSpec((tm, k), lambda i: (i, 0)), pl.BlockSpec((k, n), lambda i: (0, 0))],
        out_specs=pl.BlockSpec((tm, n), lambda i: (i, 0)),
        compiler_params=_cparams("parallel"),
        name="linear",
    )(x, w_bf16)


def _layer_norm_rows(s, g, b):
    mu = jnp.mean(s, axis=-1, keepdims=True)
    xc = s - mu
    var = jnp.mean(xc * xc, axis=-1, keepdims=True)
    return xc * lax.rsqrt(var + LN_EPS) * g + b


def _top2_of4(a, b, c, d):
    hi1, lo1 = jnp.maximum(a, b), jnp.minimum(a, b)
    hi2, lo2 = jnp.maximum(c, d), jnp.minimum(c, d)
    return jnp.maximum(hi1, hi2) + jnp.maximum(jnp.minimum(hi1, hi2), jnp.maximum(lo1, lo2))


def _route_rows(scores, biased):
    bi = [biased[e:e + 1, :] for e in range(N_EXPERTS)]
    sc = [scores[e:e + 1, :] for e in range(N_EXPERTS)]
    gs = [_top2_of4(*bi[4 * g:4 * g + 4]) for g in range(N_EXPERT_GROUPS)]
    group = jnp.zeros_like(gs[0], dtype=I32)
    best = gs[0]
    for g in range(1, N_EXPERT_GROUPS):
        upd = gs[g] > best
        group = jnp.where(upd, g, group)
        best = jnp.where(upd, gs[g], best)

    def pick(rows, j):
        out = rows[j]
        for g in range(1, N_EXPERT_GROUPS):
            out = jnp.where(group == g, rows[4 * g + j], out)
        return out

    v = [pick(bi, j) for j in range(EXPERTS_PER_GROUP)]
    s = [pick(sc, j) for j in range(EXPERTS_PER_GROUP)]
    i1 = jnp.zeros_like(group)
    b1 = v[0]
    for j in range(1, EXPERTS_PER_GROUP):
        upd = v[j] > b1
        i1 = jnp.where(upd, j, i1)
        b1 = jnp.where(upd, v[j], b1)
    i2 = jnp.full_like(group, -1)
    b2 = jnp.full_like(b1, -jnp.inf)
    for j in range(EXPERTS_PER_GROUP):
        upd = (i1 != j) & ((v[j] > b2) | (i2 < 0))
        i2 = jnp.where(upd, j, i2)
        b2 = jnp.where(upd, v[j], b2)
    s1 = s[0]
    s2 = s[0]
    for j in range(1, EXPERTS_PER_GROUP):
        s1 = jnp.where(i1 == j, s[j], s1)
        s2 = jnp.where(i2 == j, s[j], s2)
    den = s1 + s2
    first_is_lo = i1 < i2
    lo = jnp.where(first_is_lo, i1, i2)
    hi = jnp.where(first_is_lo, i2, i1)
    pair = lax.shift_right_logical(lo * (7 - lo), 1) + hi - lo - 1
    g1, g2 = s1 / den, s2 / den
    return group * PAIRS_PER_GROUP + pair, jnp.where(first_is_lo, g1, g2), jnp.where(first_is_lo, g2, g1)


def _post_mixer_kernel(h_ref, w_ref, x_ref, g_ref, b_ref, rwt_ref, rb_ref, ustrict_ref, eye_ref,
                       x1w_ref, ri_ref, cnt_ref, base_ref):
    tm, d = x_ref.shape

    @pl.when(pl.program_id(0) == 0)
    def _():
        base_ref[...] = jnp.zeros_like(base_ref)

    y = _dot(h_ref[...], w_ref[...])
    x1 = _layer_norm_rows(DEEPNORM_ALPHA * x_ref[...] + y, g_ref[...], b_ref[...])
    x1w_ref[:, 0:d] = x1

    logits = _dot_nt(rwt_ref[...], x1.astype(BF16))
    scores = jax.nn.sigmoid(logits)
    seg, g_lo, g_hi = _route_rows(scores, scores + rb_ref[...])

    sidx = lax.broadcasted_iota(I32, (N_SEGMENTS, tm), 0)
    hit = sidx == seg
    onehot = jnp.where(hit, 1.0, 0.0)
    base = base_ref[:, 0:1]
    before = _dot(onehot.astype(BF16), ustrict_ref[...]) + base
    rank = jnp.sum(jnp.where(hit, before, 0.0), axis=0, keepdims=True)
    new_base = base + jnp.sum(onehot, axis=1, keepdims=True)
    base_ref[...] = jnp.broadcast_to(new_base, base_ref.shape)
    cnt_ref[...] = jnp.broadcast_to(new_base, cnt_ref.shape)

    ri_ref[...] = jnp.concatenate([seg, rank.astype(I32), jnp.zeros((6, tm), I32)], axis=0)
    gates = jnp.concatenate([g_lo, g_hi, jnp.zeros((ROW_EXTRA - 2, tm), F32)], axis=0)
    hi = gates.astype(BF16)
    lo = (gates - hi.astype(F32)).astype(BF16)
    x1w_ref[:, d:d + ROW_EXTRA] = _dot_nt(eye_ref[...], hi) + _dot_nt(eye_ref[...], lo)


def _post_mixer(h_bf16, w_out_bf16, x, ln_g, ln_b, router_wt, router_b_col, tm=ROW_TILE):
    t, d = x.shape
    ustrict = (jnp.arange(tm)[:, None] < jnp.arange(tm)[None, :]).astype(BF16)
    eye = jnp.eye(tm, dtype=BF16)
    full = lambda shape: pl.BlockSpec(shape, lambda i: (0,) * len(shape))
    return pl.pallas_call(
        _post_mixer_kernel,
        out_shape=(jax.ShapeDtypeStruct((t, d + ROW_EXTRA), F32),
                   jax.ShapeDtypeStruct((8, t), I32),
                   jax.ShapeDtypeStruct((N_SEGMENTS, V7X_LANES), F32)),
        grid=(t // tm,),
        in_specs=[pl.BlockSpec((tm, h_bf16.shape[1]), lambda i: (i, 0)),
                  full(w_out_bf16.shape),
                  pl.BlockSpec((tm, d), lambda i: (i, 0)),
                  full((1, d)), full((1, d)),
                  full((N_EXPERTS, d)), full((N_EXPERTS, 1)),
                  full((tm, tm)), full((tm, tm))],
        out_specs=(pl.BlockSpec((tm, d + ROW_EXTRA), lambda i: (i, 0)),
                   pl.BlockSpec((8, tm), lambda i: (0, i)),
                   pl.BlockSpec((N_SEGMENTS, V7X_LANES), lambda i: (0, 0))),
        scratch_shapes=[pltpu.VMEM((N_SEGMENTS, V7X_LANES), F32)],
        compiler_params=_cparams("arbitrary"),
        name="post_mixer_router",
    )(h_bf16, w_out_bf16, x, ln_g.reshape(1, d), ln_b.reshape(1, d), router_wt, router_b_col, ustrict, eye)


def _row_copy(src_ref, src_row, dst_ref, dst_row, sem):
    return pltpu.make_async_copy(src_ref.at[pl.ds(src_row, 1), :], dst_ref.at[pl.ds(dst_row, 1), :], sem)


def _dispatch_kernel(n_tiles, pad_lo_ref, pad_hi_ref, x_ref, dest_hbm, xs_hbm,
                     idx_smem, zero_ref, idx_sem, row_sem, pad_sem):
    tm = x_ref.shape[0]
    i = pl.program_id(0)
    slot = lax.rem(i, 2)

    def idx_copy(tile, sl):
        return pltpu.make_async_copy(dest_hbm.at[tile], idx_smem.at[sl], idx_sem.at[sl])

    @pl.when(i == 0)
    def _():
        idx_copy(0, 0).start()

    idx_copy(i, slot).wait()

    @pl.when(i + 1 < n_tiles)
    def _():
        idx_copy(i + 1, 1 - slot).start()

    def issue(j, carry):
        _row_copy(x_ref, j, xs_hbm, idx_smem[slot, 0, j], row_sem).start()
        return carry

    lax.fori_loop(0, tm, issue, 0, unroll=8)

    def drain(j, carry):
        _row_copy(x_ref, j, xs_hbm, idx_smem[slot, 0, j], row_sem).wait()
        return carry

    lax.fori_loop(0, tm, drain, 0, unroll=8)

    @pl.when(i == pl.num_programs(0) - 1)
    def _():
        zero_ref[...] = jnp.zeros_like(zero_ref)
        for e in range(N_SEGMENTS + 1):
            def fill(r, carry):
                _row_copy(zero_ref, 0, xs_hbm, r, pad_sem).start()
                return carry

            lax.fori_loop(pad_lo_ref[e], pad_hi_ref[e], fill, 0)
        for e in range(N_SEGMENTS + 1):
            def filled(r, carry):
                _row_copy(zero_ref, 0, xs_hbm, r, pad_sem).wait()
                return carry

            lax.fori_loop(pad_lo_ref[e], pad_hi_ref[e], filled, 0)


def _dispatch(x1w, dest_tiles, pad_lo, pad_hi, n_rows, tm=ROW_TILE):
    t, d = x1w.shape
    grid_spec = pltpu.PrefetchScalarGridSpec(
        num_scalar_prefetch=2,
        grid=(t // tm,),
        in_specs=[pl.BlockSpec((tm, d), lambda i, lo, hi: (i, 0)),
                  pl.BlockSpec(memory_space=pl.ANY)],
        out_specs=pl.BlockSpec(memory_space=pl.ANY),
        scratch_shapes=[pltpu.SMEM((2, 1, tm), I32), pltpu.VMEM((V7X_SUBLANES, d), F32),
                        pltpu.SemaphoreType.DMA((2,)), pltpu.SemaphoreType.DMA, pltpu.SemaphoreType.DMA],
    )
    return pl.pallas_call(
        functools.partial(_dispatch_kernel, t // tm),
        out_shape=jax.ShapeDtypeStruct((n_rows, d), F32),
        grid_spec=grid_spec,
        compiler_params=_cparams("arbitrary"),
        name="moe_dispatch",
    )(pad_lo, pad_hi, x1w, dest_tiles)


def _expert_kernel(ea_ref, eb_ref, xs_ref, wga_ref, wua_ref, wda_ref, wgb_ref, wub_ref, wdb_ref, ys_ref):
    del ea_ref, eb_ref
    d = ys_ref.shape[1]
    x = xs_ref[:, 0:d].astype(BF16)
    extra = xs_ref[:, d:d + ROW_EXTRA]
    y = None
    for col, (wg_ref, wu_ref, wd_ref) in enumerate(((wga_ref, wua_ref, wda_ref), (wgb_ref, wub_ref, wdb_ref))):
        hidden = _silu(_dot(x, wg_ref[0, 0].astype(BF16))) * _dot(x, wu_ref[0, 0].astype(BF16))
        part = _dot((hidden * extra[:, col:col + 1]).astype(BF16), wd_ref[0, 0].astype(BF16))
        y = part if y is None else y + part
    ys_ref[...] = y


def _experts(block_ea, block_eb, xs, wg, wu, wd, layer):
    n_rows = xs.shape[0]
    d = xs.shape[1] - ROW_EXTRA
    n_blocks = n_rows // MOE_BLOCK
    f = wg.shape[3]
    up = lambda which: pl.BlockSpec((1, 1, d, f), lambda i, ea, eb: (layer, (ea, eb)[which][i], 0, 0))
    down = lambda which: pl.BlockSpec((1, 1, f, d), lambda i, ea, eb: (layer, (ea, eb)[which][i], 0, 0))
    grid_spec = pltpu.PrefetchScalarGridSpec(
        num_scalar_prefetch=2,
        grid=(n_blocks,),
        in_specs=[pl.BlockSpec((MOE_BLOCK, d + ROW_EXTRA), lambda i, ea, eb: (i, 0)),
                  up(0), up(0), down(0), up(1), up(1), down(1)],
        out_specs=pl.BlockSpec((MOE_BLOCK, d), lambda i, ea, eb: (i, 0)),
    )
    return pl.pallas_call(
        _expert_kernel,
        out_shape=jax.ShapeDtypeStruct((n_rows, d), F32),
        grid_spec=grid_spec,
        compiler_params=_cparams("parallel"),
        name="moe_experts",
    )(block_ea, block_eb, xs, wg, wu, wd, wg, wu, wd)


def _combine_kernel(n_tiles, x1w_ref, g_ref, b_ref, dest_hbm, ys_hbm, o_ref, ob_ref,
                    idx_smem, ybuf, idx_sem, row_sem):
    tm, d = o_ref.shape
    i = pl.program_id(0)
    slot = lax.rem(i, 2)
    other = 1 - slot

    def idx_copy(tile, sl):
        return pltpu.make_async_copy(dest_hbm.at[tile], idx_smem.at[sl], idx_sem.at[sl])

    def gather(sl, start):
        def body(j, carry):
            cp = _row_copy(ys_hbm, idx_smem[sl, 0, j], ybuf.at[sl], j, row_sem.at[sl])
            if start:
                cp.start()
            else:
                cp.wait()
            return carry

        lax.fori_loop(0, tm, body, 0, unroll=8)

    @pl.when(i == 0)
    def _():
        first = idx_copy(0, 0)
        first.start()
        first.wait()
        gather(0, True)
        if n_tiles > 1:
            idx_copy(1, 1).start()

    @pl.when(i + 1 < n_tiles)
    def _():
        idx_copy(i + 1, other).wait()
        gather(other, True)

    gather(slot, False)

    @pl.when(i + 2 < n_tiles)
    def _():
        idx_copy(i + 2, slot).start()

    x2 = _layer_norm_rows(DEEPNORM_ALPHA * x1w_ref[:, 0:d] + ybuf[slot], g_ref[...], b_ref[...])
    o_ref[...] = x2
    ob_ref[...] = x2.astype(ob_ref.dtype)


def _combine(x1w, ln_g, ln_b, dest_tiles, ys, tm=ROW_TILE):
    t = x1w.shape[0]
    d = ys.shape[1]
    full = lambda shape: pl.BlockSpec(shape, lambda i: (0,) * len(shape))
    return pl.pallas_call(
        functools.partial(_combine_kernel, t // tm),
        out_shape=(jax.ShapeDtypeStruct((t, d), F32), jax.ShapeDtypeStruct((t, d), BF16)),
        grid=(t // tm,),
        in_specs=[pl.BlockSpec((tm, d + ROW_EXTRA), lambda i: (i, 0)),
                  full((1, d)), full((1, d)),
                  pl.BlockSpec(memory_space=pl.ANY),
                  pl.BlockSpec(memory_space=pl.ANY)],
        out_specs=(pl.BlockSpec((tm, d), lambda i: (i, 0)), pl.BlockSpec((tm, d), lambda i: (i, 0))),
        scratch_shapes=[pltpu.SMEM((2, 1, tm), I32), pltpu.VMEM((2, tm, d), F32),
                        pltpu.SemaphoreType.DMA((2,)), pltpu.SemaphoreType.DMA((2,))],
        compiler_params=_cparams("arbitrary"),
        name="moe_combine_ln",
    )(x1w, ln_g.reshape(1, d), ln_b.reshape(1, d), dest_tiles, ys)


def _moe_layer(x1w, route_i, counts, wg, wu, wd, layer, ln_g, ln_b, tm=ROW_TILE):
    t = x1w.shape[0]
    n_tiles = t // tm
    n_rows = t + N_SEGMENTS * MOE_BLOCK
    counts = counts.astype(I32)
    padded = (counts + MOE_BLOCK - 1) // MOE_BLOCK * MOE_BLOCK
    pends = jnp.cumsum(padded)
    pstarts = pends - padded
    seg = route_i[0:1]
    start_of = jnp.sum(jnp.where(seg[None] == jnp.arange(N_SEGMENTS, dtype=I32)[:, None, None],
                                 pstarts[:, None, None], 0), axis=0)
    dest = start_of + route_i[1:2]
    dest_tiles = dest.reshape(1, n_tiles, tm).transpose(1, 0, 2)
    n_blocks = n_rows // MOE_BLOCK
    block_seg = jnp.minimum(
        jnp.sum((jnp.arange(n_blocks, dtype=I32) * MOE_BLOCK)[:, None] >= pends[None, :], axis=-1),
        N_SEGMENTS - 1).astype(I32)
    pair_lo = jnp.array([lo for lo in range(EXPERTS_PER_GROUP) for hi in range(lo + 1, EXPERTS_PER_GROUP)], I32)
    pair_hi = jnp.array([hi for lo in range(EXPERTS_PER_GROUP) for hi in range(lo + 1, EXPERTS_PER_GROUP)], I32)
    block_group, block_pair = block_seg // PAIRS_PER_GROUP, block_seg % PAIRS_PER_GROUP
    block_ea = block_group * EXPERTS_PER_GROUP + pair_lo[block_pair]
    block_eb = block_group * EXPERTS_PER_GROUP + pair_hi[block_pair]
    pad_lo = jnp.concatenate([pstarts + counts, pends[-1:]]).astype(I32)
    pad_hi = jnp.concatenate([pends, jnp.full((1,), n_rows, I32)]).astype(I32)
    xs = _dispatch(x1w, dest_tiles, pad_lo, pad_hi, n_rows, tm)
    ys = _experts(block_ea, block_eb, xs, wg, wu, wd, layer)
    return _combine(x1w, ln_g, ln_b, dest_tiles, ys, tm)


POOL_PAD = 16


def _pool_kernel(x_ref, win_ref, wg_ref, scale_ref, o_ref, pad_ref):
    s, c = x_ref.shape[1], win_ref.shape[1]
    zeros = jnp.zeros((POOL_PAD, c), F32)
    pad_ref[pl.ds(0, POOL_PAD), :] = zeros
    pad_ref[pl.ds(POOL_PAD + s, POOL_PAD), :] = zeros
    pad_ref[pl.ds(POOL_PAD, s), :] = _dot(x_ref[0].astype(BF16), win_ref[...])
    u = pad_ref[pl.ds(POOL_PAD, s), :]
    pos = lax.broadcasted_iota(I32, (s, c), 0)
    group = pl.program_id(1)
    for gi, win in enumerate(POOL_WINDOWS):
        @pl.when(group == gi)
        def _(win=win):
            half = win // 2
            total = pad_ref[pl.ds(POOL_PAD - half, s), :]
            for j in range(1 - half, half):
                total = total + pad_ref[pl.ds(POOL_PAD + j, s), :]
            count = (jnp.minimum(pos + half, s) - jnp.maximum(pos - half, 0)).astype(F32)
            mixed = total / count - u
            y = _dot(mixed.astype(BF16), wg_ref[0]) * scale_ref[...]
            o_ref[0] = y.astype(o_ref.dtype)


def _pool_core(x3, w_in_bf16, w_group_bf16, scale):
    b, s, d = x3.shape
    c = POOL_GROUP_DIM
    return pl.pallas_call(
        _pool_kernel,
        out_shape=jax.ShapeDtypeStruct((b, s, d), BF16),
        grid=(b, d // c),
        in_specs=[pl.BlockSpec((1, s, d), lambda i, g: (i, 0, 0)),
                  pl.BlockSpec((d, c), lambda i, g: (0, g)),
                  pl.BlockSpec((1, c, c), lambda i, g: (g, 0, 0)),
                  pl.BlockSpec((1, c), lambda i, g: (0, g))],
        out_specs=pl.BlockSpec((1, s, c), lambda i, g: (i, 0, g)),
        scratch_shapes=[pltpu.VMEM((s + 2 * POOL_PAD, c), F32)],
        compiler_params=_cparams("parallel", "parallel"),
        name="pool_core",
    )(x3, w_in_bf16, w_group_bf16, scale.reshape(1, d))


def _t5_bucket(rel):
    half = REL_BUCKETS // 2
    max_exact = half // 2
    n = jnp.abs(rel)
    large = max_exact + (jnp.log(jnp.maximum(n, 1).astype(F32) / max_exact)
                         / math.log(REL_MAX_DIST / max_exact) * (half - max_exact)).astype(I32)
    large = jnp.minimum(large, half - 1)
    return (rel > 0).astype(I32) * half + jnp.where(n < max_exact, n, large)


def _attn_bias_table(rel_bias):
    rel = jnp.arange(ATT_KSPAN)[None, :] - ATT_WINDOW - jnp.arange(ATT_QBLOCK)[:, None]
    onehot = (_t5_bucket(rel)[..., None] == jnp.arange(REL_BUCKETS)).astype(F32)
    bias = jnp.einsum("qkb,bh->hqk", onehot, rel_bias.astype(F32), precision=HIGHEST)
    return jnp.where((jnp.abs(rel) <= ATT_WINDOW)[None], bias, NEG_BIG)


def _attn_kernel(q_ref, kp_ref, kc_ref, kn_ref, vp_ref, vc_ref, vn_ref, bias_ref, sink_ref, o_ref):
    j = pl.program_id(1)
    nb = pl.num_programs(1)
    qb = ATT_QBLOCK
    col = lax.broadcasted_iota(I32, (qb, ATT_KSPAN), 1)
    edge = jnp.where(((j == 0) & (col < qb)) | ((j == nb - 1) & (col >= 2 * qb)), NEG_BIG, 0.0)
    k_all = jnp.concatenate([kp_ref[...], kc_ref[...], kn_ref[...]], axis=0)
    v_all = jnp.concatenate([vp_ref[...], vc_ref[...], vn_ref[...]], axis=0)
    q = q_ref[...] * (ATT_HEAD_DIM ** -0.5)
    heads = lambda g: range(g * ATT_GROUP, (g + 1) * ATT_GROUP)
    cols = lambda i: slice(i * ATT_HEAD_DIM, (i + 1) * ATT_HEAD_DIM)

    def scores(g):
        return [_dot_nt(q[:, cols(hd)], k_all[:, cols(g)]) for hd in heads(g)]

    def attend(g, raw):
        outs = []
        probs, dens = [], []
        for hd, qk in zip(heads(g), raw):
            logits = qk + bias_ref[hd] + edge
            sink = sink_ref[hd]
            mx = jnp.maximum(jnp.max(logits, axis=-1, keepdims=True), sink)
            p = jnp.exp(logits - mx)
            dens.append(jnp.sum(p, axis=-1, keepdims=True) + jnp.exp(sink - mx))
            probs.append(p.astype(BF16))
        for p, den in zip(probs, dens):
            outs.append(_dot(p, v_all[:, cols(g)]) / den)
        return outs

    outs = []
    raw = scores(0)
    for g in range(ATT_KV_HEADS):
        nxt = scores(g + 1) if g + 1 < ATT_KV_HEADS else None
        outs += attend(g, raw)
        raw = nxt
    o_ref[...] = jnp.concatenate(outs, axis=-1).astype(o_ref.dtype)


def _attn_core(proj, bias_tbl, sink, b, s):
    qw = ATT_HEADS * ATT_HEAD_DIM
    kw = ATT_KV_HEADS * ATT_HEAD_DIM
    nb = s // ATT_QBLOCK
    kcol, vcol = qw // kw, qw // kw + 1
    row = lambda i, j: i * nb + j
    prev = lambda i, j: i * nb + jnp.maximum(j - 1, 0)
    nxt = lambda i, j: i * nb + jnp.minimum(j + 1, nb - 1)
    kv_spec = lambda rowf, c: pl.BlockSpec((ATT_QBLOCK, kw), lambda i, j: (rowf(i, j), c))
    return pl.pallas_call(
        _attn_kernel,
        out_shape=jax.ShapeDtypeStruct((b * s, qw), BF16),
        grid=(b, nb),
        in_specs=[pl.BlockSpec((ATT_QBLOCK, qw), lambda i, j: (row(i, j), 0)),
                  kv_spec(prev, kcol), kv_spec(row, kcol), kv_spec(nxt, kcol),
                  kv_spec(prev, vcol), kv_spec(row, vcol), kv_spec(nxt, vcol),
                  pl.BlockSpec((ATT_HEADS, ATT_QBLOCK, ATT_KSPAN), lambda i, j: (0, 0, 0)),
                  pl.BlockSpec(memory_space=pltpu.SMEM)],
        out_specs=pl.BlockSpec((ATT_QBLOCK, qw), lambda i, j: (row(i, j), 0)),
        compiler_params=_cparams("parallel", "parallel"),
        name="attn_core",
    )(proj, proj, proj, proj, proj, proj, proj, bias_tbl, sink.astype(F32))


GDN_GATE_LANES = V7X_LANES
GDN_CONV_PAD = 8
GDN_PREP_GROUP = 16
GDN_INPROJ_COLS = 512
GDN_INPROJ_SLAB = 256
GDN_HEADS_PER_STEP = 2
GDN_PREP_MATMULS = 10


def _softplus(x):
    return jnp.maximum(x, 0.0) + jnp.log(1.0 + jnp.exp(-jnp.abs(x)))


def _gdn_gates_kernel(x_ref, w_ref, alog_ref, dtb_ref, o_ref, rows_ref):
    s = x_ref.shape[0]
    c = GDN_CHUNK
    gl = _dot(x_ref[...].astype(BF16), w_ref[...])
    decay = -jnp.exp(alog_ref[...]) * _softplus(gl + dtb_ref[...])
    beta = jax.nn.sigmoid(gl)
    r = lax.broadcasted_iota(I32, (c, c), 0)
    q = lax.broadcasted_iota(I32, (c, c), 1)
    lower = jnp.where(r >= q, 1.0, 0.0)
    upper = jnp.where(r <= q, 1.0, 0.0)
    lane = lax.broadcasted_iota(I32, (c, GDN_GATE_LANES), 1)
    hh = GDN_HEADS
    eye = jnp.where(lax.broadcasted_iota(I32, (4 * hh, GDN_GATE_LANES), 0)
                    == lax.broadcasted_iota(I32, (4 * hh, GDN_GATE_LANES), 1), 1.0, 0.0)
    for n in range(s // c):
        d_c = decay[n * c:(n + 1) * c]
        pre = _dot(lower, d_c, precision=HIGHEST)
        suf = _dot(upper, d_c, precision=HIGHEST)
        tile = jnp.where(lane < hh, pre, jnp.where(lane < 2 * hh, suf, beta[n * c:(n + 1) * c]))
        o_ref[pl.ds(n * c, c), :] = tile
        t_rows = _dot_nt(eye, tile, precision=HIGHEST)
        g_rows = jnp.concatenate([t_rows[0:hh], t_rows[hh:2 * hh]], axis=1)
        b_rows = jnp.concatenate([t_rows[2 * hh:3 * hh], t_rows[3 * hh:4 * hh]], axis=1)
        rows_ref[n] = jnp.concatenate([g_rows, b_rows, b_rows * jnp.exp(g_rows)], axis=1)


def _gdn_gates(xf, w_gate_bf16, alog_vec, dtb_vec, b, s):
    d = xf.shape[1]
    n_chunks = s // GDN_CHUNK
    full = lambda shape: pl.BlockSpec(shape, lambda i: (0,) * len(shape))
    return pl.pallas_call(
        _gdn_gates_kernel,
        out_shape=(jax.ShapeDtypeStruct((b * s, GDN_GATE_LANES), F32),
                   jax.ShapeDtypeStruct((b * n_chunks, GDN_HEADS, 3 * GDN_GATE_LANES), F32)),
        grid=(b,),
        in_specs=[pl.BlockSpec((s, d), lambda i: (i, 0)), full((d, GDN_GATE_LANES)),
                  full((1, GDN_GATE_LANES)), full((1, GDN_GATE_LANES))],
        out_specs=(pl.BlockSpec((s, GDN_GATE_LANES), lambda i: (i, 0)),
                   pl.BlockSpec((n_chunks, GDN_HEADS, 3 * GDN_GATE_LANES), lambda i: (i, 0, 0))),
        compiler_params=_cparams("parallel"),
        name="gdn_gates",
    )(xf, w_gate_bf16, alog_vec, dtb_vec)


def _block_diag2(r, is_b):
    return jnp.concatenate([jnp.where(is_b, 0.0, r), jnp.where(is_b, r, 0.0)], axis=0)


def _gdn_inproj_kernel(x_ref, w_ref, cw_ref, o_ref, pad_ref):
    s = x_ref.shape[0]
    n = w_ref.shape[1]
    dk = GDN_HEAD_DIM
    kind = pl.program_id(1) // (GDN_WIDTH // n)
    slab = min(s, GDN_INPROJ_SLAB)
    n_slabs = s // slab
    project = lambda k: _dot(x_ref[pl.ds(k * slab, slab), :].astype(BF16), w_ref[...])

    @pl.when(kind == 3)
    def _():
        for k in range(n_slabs):
            o_ref[pl.ds(k * slab, slab), :] = project(k).astype(o_ref.dtype)

    @pl.when(kind < 3)
    def _():
        norm_on = kind < 2
        q_scale = jnp.where(kind == 0, GDN_HEAD_DIM ** -0.5, 1.0)
        zeros = jnp.zeros((GDN_CONV_PAD, n), F32)
        pad_ref[pl.ds(0, GDN_CONV_PAD), :] = zeros
        pad_ref[pl.ds(GDN_CONV_PAD + s, GDN_CONV_PAD), :] = zeros
        cw = cw_ref[...]

        def finish(k):
            r0 = k * slab
            acc = None
            for j in range(GDN_CONV):
                term = pad_ref[pl.ds(GDN_CONV_PAD + r0 + j - GDN_CONV // 2, slab), :] * cw[j:j + 1, :]
                acc = term if acc is None else acc + term
            act = _silu(acc)
            outs = []
            for hd in range(n // dk):
                a = act[:, hd * dk:(hd + 1) * dk]
                inv = lax.rsqrt(jnp.sum(a * a, axis=-1, keepdims=True) + 1e-6) * q_scale
                outs.append(a * jnp.where(norm_on, inv, 1.0))
            o_ref[pl.ds(r0, slab), :] = jnp.concatenate(outs, axis=1).astype(o_ref.dtype)

        for k in range(n_slabs):
            pad_ref[pl.ds(GDN_CONV_PAD + k * slab, slab), :] = project(k)
            if k >= 1:
                finish(k - 1)
        finish(n_slabs - 1)


def _gdn_inproj(xf, w_main_bf16, conv_w, b, s):
    d = xf.shape[1]
    n = GDN_INPROJ_COLS
    n_conv_blocks = 3 * GDN_WIDTH // n
    return pl.pallas_call(
        _gdn_inproj_kernel,
        out_shape=jax.ShapeDtypeStruct((b * s, 4 * GDN_WIDTH), BF16),
        grid=(b, 4 * GDN_WIDTH // n),
        in_specs=[pl.BlockSpec((s, d), lambda i, j: (i, 0)),
                  pl.BlockSpec((d, n), lambda i, j: (0, j)),
                  pl.BlockSpec((GDN_CONV, n), lambda i, j: (0, jnp.minimum(j, n_conv_blocks - 1)))],
        out_specs=pl.BlockSpec((s, n), lambda i, j: (i, j)),
        scratch_shapes=[pltpu.VMEM((s + 2 * GDN_CONV_PAD, n), F32)],
        compiler_params=_cparams("parallel", "parallel"),
        name="gdn_inproj",
    )(xf, w_main_bf16, conv_w)


def _gdn_core_kernel(q_ref, k_ref, v_ref, z_ref, gt_ref, rows_ref, nw_ref, o_ref,
                     u_ref, w_ref, qd_ref, kd_ref, aqk_ref, dend_ref, of_ref, ob_ref):
    s = q_ref.shape[0]
    dk = GDN_HEAD_DIM
    hps = q_ref.shape[1] // dk
    c = GDN_CHUNK
    n_chunks = s // c
    first_head = pl.program_id(1) * hps
    slab = min(s, 256)
    head_cols = [slice(hd * dk, (hd + 1) * dk) for hd in range(hps)]

    lane = lax.broadcasted_iota(I32, (c, 2 * c), 1)
    row = lax.broadcasted_iota(I32, (c, 2 * c), 0)
    is_b = lane >= c
    col = jnp.where(is_b, lane - c, lane)
    ahead = jnp.where(is_b, col - row, row - col)
    incl = ahead >= 0
    strict = ahead > 0
    eye2 = jnp.where(row == col, 1.0, 0.0)
    shifts = [lax.rem(2 * V7X_LANES - first_head - hd, V7X_LANES) for hd in range(hps)]

    group = min(GDN_PREP_GROUP, n_chunks)
    n_groups = n_chunks // group

    def pair_rows(t):
        if isinstance(t, int):
            return pl.ds(t * c, c), pl.ds((n_chunks - 1 - t) * c, c)
        return pl.ds(pl.multiple_of(t * c, c), c), pl.ds(pl.multiple_of((n_chunks - 1 - t) * c, c), c)

    def prep_stages(hd, pg, slot):
        rows = [pair_rows(pg * group + i) for i in range(group)]
        cols = head_cols[hd]
        q_f, k_f, v_f = ([ref[r[0], cols].astype(F32) for r in rows] for ref in (q_ref, k_ref, v_ref))
        q_b, k_b, v_b = ([ref[r[1], cols].astype(F32) for r in rows] for ref in (q_ref, k_ref, v_ref))
        gt_f = [pltpu.roll(gt_ref[r[0], :], shifts[hd], axis=1) for r in rows]
        gt_b = [pltpu.roll(gt_ref[r[1], :], shifts[hd], axis=1) for r in rows]
        g_f = [t[:, 0:1] for t in gt_f]
        b_f = [t[:, 2 * GDN_HEADS:2 * GDN_HEADS + 1] for t in gt_f]
        g_b = [t[:, GDN_HEADS:GDN_HEADS + 1] for t in gt_b]
        b_b = [t[:, 3 * GDN_HEADS:3 * GDN_HEADS + 1] for t in gt_b]
        head = pl.ds(first_head + hd, 1)
        g_row, beta_row, be_row = [], [], []
        for i in range(group):
            t = pg * group + i
            row_f, row_b = rows_ref[t, head, :], rows_ref[n_chunks - 1 - t, head, :]
            pick = lambda j: jnp.where(is_b[0:1], row_b[:, j * 2 * c:(j + 1) * 2 * c], row_f[:, j * 2 * c:(j + 1) * 2 * c])
            g_row.append(pick(0))
            beta_row.append(pick(1))
            be_row.append(pick(2))
        dmat = [jnp.where(is_b, g_b[i], g_f[i]) - g_row[i] for i in range(group)]
        prod_f, prod_b = [], []
        for i in range(group):
            kfb, kbb = k_f[i].astype(BF16), k_b[i].astype(BF16)
            keys = jnp.concatenate([kfb, kbb], axis=0)
            prod_f.append(_dot_nt(jnp.concatenate([q_f[i].astype(BF16), kfb], axis=0), keys))
            yield
            prod_b.append(_dot_nt(jnp.concatenate([q_b[i].astype(BF16), kbb], axis=0), keys))
            yield
        qk2 = [jnp.where(is_b, prod_b[i][:c], prod_f[i][:c]) for i in range(group)]
        kk2 = [jnp.where(is_b, prod_b[i][c:], prod_f[i][c:]) for i in range(group)]
        gamma = [jnp.exp(jnp.where(incl, d_i, NEG_BIG)) for d_i in dmat]
        m = [jnp.where(strict, kk2[i] * gamma[i], 0.0) * jnp.where(is_b, b_b[i], b_f[i]) for i in range(group)]
        x_inv = [eye2 - m_i for m_i in m]
        p = []
        for m_i in m:
            p.append(_dot(m_i.astype(BF16), _block_diag2(m_i, is_b).astype(BF16)))
            yield
        for _ in range(5):
            y = []
            for i in range(group):
                y.append(_dot(jnp.concatenate([x_inv[i], p[i]], axis=0).astype(BF16),
                              _block_diag2(p[i], is_b).astype(BF16)))
                yield
            x_inv = [x_inv[i] + y[i][:c] for i in range(group)]
            p = [y_i[c:] for y_i in y]
        eg_f = [jnp.exp(g) for g in g_f]
        eg_b = [jnp.exp(g) for g in g_b]
        u_sol, w_sol = [], []
        for i in range(group):
            u_sol.append(_dot(_block_diag2(x_inv[i] * beta_row[i], is_b).astype(BF16),
                              jnp.concatenate([v_f[i], v_b[i]], axis=0).astype(BF16)))
            yield
            w_sol.append(_dot(_block_diag2(x_inv[i] * be_row[i], is_b).astype(BF16),
                              jnp.concatenate([k_f[i], k_b[i]], axis=0).astype(BF16)))
            yield
        for i in range(group):
            local = pl.ds(i * c, c)
            gl_f, gl_b = g_f[i][c - 1:c, :], g_b[i][0:1, :]
            for d, (qc, kc, eg, gl, g_d) in enumerate(((q_f[i], k_f[i], eg_f[i], gl_f, g_f[i]),
                                                       (q_b[i], k_b[i], eg_b[i], gl_b, g_b[i]))):
                u_ref[hd, slot, d, local, :] = u_sol[i][d * c:(d + 1) * c]
                w_ref[hd, slot, d, local, :] = w_sol[i][d * c:(d + 1) * c]
                qd_ref[hd, slot, d, local, :] = qc * eg
                kd_ref[hd, slot, d, local, :] = kc * jnp.exp(gl - g_d)
            aqk_ref[hd, slot, local, :] = jnp.where(incl, qk2[i] * gamma[i], 0.0)
            dend_ref[hd, slot, pl.ds(i * V7X_SUBLANES, V7X_SUBLANES), :] = jnp.where(
                is_b[:V7X_SUBLANES], jnp.exp(gl_b), jnp.exp(gl_f))

    lanes = (slice(0, c), slice(c, 2 * c))
    chains = [(hd, d) for hd in range(hps) for d in range(2)]

    def run_group(pg, slot, prep_gens, states):
        per_gap = -(-GDN_PREP_MATMULS // 2)

        def advance():
            for _ in range(per_gap):
                for gen in prep_gens:
                    next(gen, None)

        for i in range(group):
            rows = pair_rows(pg * group + i)
            local = pl.ds(i * c, c)
            stb = [st.astype(BF16) for st in states]
            ws = [_dot(w_ref[hd, slot, d, local, :].astype(BF16), stb[n]) for n, (hd, d) in enumerate(chains)]
            qs = [_dot(qd_ref[hd, slot, d, local, :].astype(BF16), stb[n]) for n, (hd, d) in enumerate(chains)]
            advance()
            vnb = [(u_ref[hd, slot, d, local, :] - ws[n]).astype(BF16) for n, (hd, d) in enumerate(chains)]
            av = [_dot(aqk_ref[hd, slot, local, :][:, lanes[d]].astype(BF16), vnb[n])
                  for n, (hd, d) in enumerate(chains)]
            kv = [_dot_tn(kd_ref[hd, slot, d, local, :].astype(BF16), vnb[n]) for n, (hd, d) in enumerate(chains)]
            advance()
            new_states = []
            for n, (hd, d) in enumerate(chains):
                (of_ref, ob_ref)[d][hd, rows[d], :] = qs[n] + av[n]
                dend = dend_ref[hd, slot, pl.ds(i * V7X_SUBLANES, 1), :][:, d * c:d * c + 1]
                new_states.append(states[n] * dend + kv[n])
            states = tuple(new_states)
        for gen in prep_gens:
            for _ in gen:
                pass
        return states

    for hd in range(hps):
        for _ in prep_stages(hd, 0, 0):
            pass

    def fused(tg, states):
        slot = lax.rem(tg, 2)
        return run_group(tg, slot, [prep_stages(hd, tg + 1, 1 - slot) for hd in range(hps)], states)

    zero_state = jnp.zeros((dk, dk), F32)
    states = lax.fori_loop(0, n_groups - 1, fused, (zero_state,) * len(chains))
    run_group(n_groups - 1, (n_groups - 1) % 2, [], states)

    for hd in range(hps):
        cols = slice(hd * dk, (hd + 1) * dk)
        for r0 in range(0, s, slab):
            rs = pl.ds(r0, slab)
            o = of_ref[hd, rs, :] + ob_ref[hd, rs, :]
            o = o * lax.rsqrt(jnp.mean(o * o, axis=-1, keepdims=True) + 1e-6) * nw_ref[...]
            o_ref[rs, cols] = (o * _silu(z_ref[rs, cols].astype(F32))).astype(o_ref.dtype)


def _gdn_core(proj, gates, gate_rows, norm_w, b, s):
    dk = GDN_HEAD_DIM
    hps = GDN_HEADS_PER_STEP
    nhb = GDN_HEADS // hps
    blk = lambda kind: pl.BlockSpec((s, hps * dk), lambda i, h: (i, kind * nhb + h))
    group_rows = min(GDN_PREP_GROUP, s // GDN_CHUNK) * GDN_CHUNK
    seq = lambda: pltpu.VMEM((hps, s, dk), F32)
    slots = lambda: pltpu.VMEM((hps, 2, 2, group_rows, dk), F32)
    return pl.pallas_call(
        _gdn_core_kernel,
        out_shape=jax.ShapeDtypeStruct((b * s, GDN_WIDTH), BF16),
        grid=(b, nhb),
        in_specs=[blk(0), blk(1), blk(2), blk(3),
                  pl.BlockSpec((s, GDN_GATE_LANES), lambda i, h: (i, 0)),
                  pl.BlockSpec((s // GDN_CHUNK, GDN_HEADS, 3 * GDN_GATE_LANES), lambda i, h: (i, 0, 0)),
                  pl.BlockSpec((1, dk), lambda i, h: (0, 0))],
        out_specs=pl.BlockSpec((s, hps * dk), lambda i, h: (i, h)),
        scratch_shapes=[slots(), slots(), slots(), slots(),
                        pltpu.VMEM((hps, 2, group_rows, 2 * GDN_CHUNK), F32),
                        pltpu.VMEM((hps, 2, group_rows // GDN_CHUNK * V7X_SUBLANES, 2 * GDN_CHUNK), F32),
                        seq(), seq()],
        compiler_params=_cparams("parallel", "parallel"),
        name="gdn_core",
    )(proj, proj, proj, proj, gates, gate_rows, norm_w.reshape(1, dk))


def _gdn_mixer_core(xf, w_in, conv_w, a_log, dt_bias, norm_w, b, s):
    n_main = 4 * GDN_WIDTH
    n_gate = 4 * GDN_HEADS
    proj = _gdn_inproj(xf, w_in[:, :n_main].astype(BF16), conv_w.astype(F32), b, s)
    w_gate = jnp.pad(w_in[:, n_main:], ((0, 0), (0, GDN_GATE_LANES - n_gate))).astype(BF16)
    lane_pad = lambda v: jnp.pad(v.astype(F32).reshape(1, 2 * GDN_HEADS), ((0, 0), (0, GDN_GATE_LANES - 2 * GDN_HEADS)))
    gates, gate_rows = _gdn_gates(xf, w_gate, lane_pad(a_log), lane_pad(dt_bias), b, s)
    return _gdn_core(proj, gates, gate_rows, norm_w.astype(F32), b, s)


def kernel(x, a_w_in, a_conv, a_A_log, a_dt_bias, a_norm_w, a_w_out, b_w_in, b_sink, b_w_out, rel_bias, c_w_in, c_w_group, c_scale, c_w_out, router_w, router_b, moe_w_gate, moe_w_up, moe_w_down, ln_g, ln_b):
    b, s, d = x.shape
    t = b * s
    xf = x.reshape(t, d)
    xb = xf
    router_wt = router_w.T.astype(BF16)
    router_b_col = router_b.reshape(N_EXPERTS, 1).astype(F32)
    bias_tbl = _attn_bias_table(rel_bias)
    for i in range(DEPTH):
        kind, j = i % N_MIXERS, i // N_MIXERS
        if kind == 0:
            h = _gdn_mixer_core(xb, a_w_in[j], a_conv[j], a_A_log[j], a_dt_bias[j], a_norm_w[j], b, s)
            w_out = a_w_out[j]
        elif kind == 1:
            proj = _linear(xb, b_w_in[j].astype(BF16), BF16)
            h = _attn_core(proj, bias_tbl, b_sink[j], b, s)
            w_out = b_w_out[j]
        else:
            h = _pool_core(xb.reshape(b, s, d), c_w_in[j].astype(BF16), c_w_group[j].astype(BF16),
                           c_scale[j]).reshape(t, d)
            w_out = c_w_out[j]
        x1w, route_i, counts = _post_mixer(h, w_out.astype(BF16), xf, ln_g[i, 0], ln_b[i, 0], router_wt, router_b_col)
        xf, xb = _moe_layer(x1w, route_i, counts[:, 0], moe_w_gate, moe_w_up, moe_w_down, i, ln_g[i, 1], ln_b[i, 1])
    return xf.reshape(b, s, d)
```
